```python
import math, functools
import jax, jax.numpy as jnp
from jax import lax
import numpy as np

D_MODEL = 1024
BATCH = 4
SEQ = 4096
DEPTH = 2
DEC_BATCH = 128
DEC_SEQ = 1
PAST_LEN = 2048
PAGE_SIZE = 128

BR_WIDTH = 512
N_BRANCH = 4
DN_HEADS = 4
DN_DK = 128
DN_DV = 128
DN_QKV = 2 * DN_HEADS * DN_DK + DN_HEADS * DN_DV
DN_CHUNK = 64
CONV_W = 4
LRU_WIDTH = 512
LRU_BLOCKS = 8
LRU_BW = LRU_WIDTH // LRU_BLOCKS
LRU_C = 8.0
FOX_HEADS = 4
FOX_HD = 128
Q_BLOCK = 128
FOX_F_BIAS = 3.0
MEM_TOKENS = 256
MEM_HEADS = 4
MEM_HD = 128
EPS = 1e-6
F32 = jnp.float32

IN_SIZES = (
    DN_QKV, DN_HEADS * DN_DV, DN_HEADS, DN_HEADS,
    LRU_WIDTH, LRU_WIDTH,
    FOX_HEADS * FOX_HD, FOX_HEADS * FOX_HD, FOX_HEADS * FOX_HD, FOX_HEADS, FOX_HEADS * FOX_HD,
    MEM_HEADS * MEM_HD, MEM_HEADS * MEM_HD,
    N_BRANCH * D_MODEL)
IN_OFFSETS = tuple(int(o) for o in np.cumsum(IN_SIZES)[:-1])
N_IN = int(sum(IN_SIZES))
FOX_F_OFF = int(sum(IN_SIZES[:9]))

kernel_name = 'hybrid_gdn_rglru_fox_mem_step'


def rms_norm(x, w):
    xf = x.astype(F32)
    y = xf * lax.rsqrt(jnp.mean(xf * xf, axis=-1, keepdims=True) + EPS)
    return (y * w.astype(F32)).astype(x.dtype)


def l2norm(x):
    xf = x.astype(F32)
    return (xf * lax.rsqrt(jnp.sum(xf * xf, axis=-1, keepdims=True) + EPS)).astype(x.dtype)


def causal_conv(x, buf, w):
    k_w = w.shape[0]
    s = x.shape[1]
    xp = jnp.concatenate([buf.astype(x.dtype), x], axis=1)
    y = xp[:, 0:s] * w[0]
    for j in range(1, k_w):
        y = y + xp[:, j:j + s] * w[j]
    return y, xp[:, -(k_w - 1):]


def gated_delta_rule(q, k, v, beta, log_alpha, s0):
    bn, s, h, dk = q.shape
    dv = v.shape[-1]
    c = DN_CHUNK if s % DN_CHUNK == 0 else s
    n = s // c

    def chunks(t):
        t = t.astype(F32).reshape((bn, n, c, h) + t.shape[3:])
        return jnp.moveaxis(t, (1, 3), (0, 2))

    qc, kc, vc = chunks(q), chunks(k), chunks(v)
    bc, gc = chunks(beta), chunks(log_alpha)
    g = jnp.cumsum(gc, axis=-1)
    idx = jnp.arange(c)
    incl = idx[:, None] >= idx[None, :]
    strict = idx[:, None] > idx[None, :]
    decay = jnp.exp(jnp.where(incl, g[..., :, None] - g[..., None, :], -jnp.inf))
    kk = jnp.einsum('nbhid,nbhjd->nbhij', kc, kc)
    lower = jnp.eye(c, dtype=F32) + jnp.where(strict, bc[..., :, None] * decay * kk, 0.0)
    rhs = jnp.concatenate([bc[..., None] * vc, (bc * jnp.exp(g))[..., None] * kc], axis=-1)
    sol = lax.linalg.triangular_solve(lower, rhs, left_side=True, lower=True, unit_diagonal=True)
    u0, w = sol[..., :dv], sol[..., dv:]
    qk = decay * jnp.einsum('nbhid,nbhjd->nbhij', qc, kc)
    kdec = jnp.exp(g[..., -1:] - g)[..., None] * kc
    gend = jnp.exp(g[..., -1])

    def step(st, inp):
        u0_n, w_n, q_n, qk_n, kdec_n, g_n, gend_n = inp
        u = u0_n - jnp.einsum('bhcd,bhvd->bhcv', w_n, st)
        o = (jnp.exp(g_n)[..., None] * jnp.einsum('bhcd,bhvd->bhcv', q_n, st)
             + jnp.einsum('bhij,bhjv->bhiv', qk_n, u))
        st_new = gend_n[..., None, None] * st + jnp.einsum('bhcv,bhcd->bhvd', u, kdec_n)
        return st_new, o

    s_fin, o = lax.scan(step, s0.astype(F32), (u0, w, qc, qk, kdec, g, gend))
    o = jnp.moveaxis(o, (0, 2), (1, 3)).reshape(bn, s, h, dv)
    return o, s_fin


def _affine_combine(left, right):
    a1, b1 = left
    a2, b2 = right
    return a1 * a2, a2 * b1 + b2


def rg_lru(xc, h0, w_r, b_r, w_i, b_i, lam):
    bn, s, wd = xc.shape
    xf = xc.astype(F32)
    xb = xf.reshape(bn, s, LRU_BLOCKS, LRU_BW)
    r = jax.nn.sigmoid(jnp.einsum('bsne,nef->bsnf', xb, w_r.astype(F32)).reshape(bn, s, wd) + b_r)
    i = jax.nn.sigmoid(jnp.einsum('bsne,nef->bsnf', xb, w_i.astype(F32)).reshape(bn, s, wd) + b_i)
    log_a = -LRU_C * r * jax.nn.softplus(-lam.astype(F32))
    a = jnp.exp(log_a)
    bx = jnp.sqrt(-jnp.expm1(2.0 * log_a)) * (i * xf)
    a_cum, b_cum = lax.associative_scan(_affine_combine, (a, bx), axis=1)
    hseq = a_cum * h0.astype(F32)[:, None, :] + b_cum
    return hseq, hseq[:, -1]


def fox_block(q, cq, qpos, k, v, ck, kpos):
    sc = jnp.einsum('bqhd,bkhd->bhqk', q, k, preferred_element_type=F32) * (FOX_HD ** -0.5)
    sc = sc + jnp.swapaxes(cq, 1, 2)[..., :, None] - jnp.swapaxes(ck, 1, 2)[..., None, :]
    sc = jnp.where(kpos[None, :] <= qpos[:, None], sc, -jnp.inf)
    p = jax.nn.softmax(sc, axis=-1)
    return jnp.einsum('bhqk,bkhd->bqhd', p.astype(v.dtype), v)


def fox_sweep(q, cq, qpos, k, v, ck, kpos):
    bn, sq, h, hd = q.shape
    if sq % Q_BLOCK:
        return fox_block(q, cq, qpos, k, v, ck, kpos)
    nb = sq // Q_BLOCK
    qb = jnp.moveaxis(q.reshape(bn, nb, Q_BLOCK, h, hd), 1, 0)
    cqb = jnp.moveaxis(cq.reshape(bn, nb, Q_BLOCK, h), 1, 0)
    qposb = qpos.reshape(nb, Q_BLOCK)
    out = lax.map(lambda blk: fox_block(blk[0], blk[1], blk[2], k, v, ck, kpos), (qb, cqb, qposb))
    return jnp.moveaxis(out, 0, 1).reshape(bn, sq, h, hd)


def fox_prompt(q, k, v, logf):
    s = q.shape[1]
    c = jnp.cumsum(logf.astype(F32), axis=1)
    pos = jnp.arange(s)
    return fox_sweep(q, c, pos, k, v, c, pos)


def fox_sample(q, k, v, logf, cache_k_l, cache_v_l, cache_logf_l, page_table):
    bd, sn = q.shape[:2]
    past = page_table.shape[1] * cache_k_l.shape[1]
    kp = cache_k_l[page_table].reshape(bd, past, FOX_HEADS, FOX_HD).astype(k.dtype)
    vp = cache_v_l[page_table].reshape(bd, past, FOX_HEADS, FOX_HD).astype(v.dtype)
    lp = cache_logf_l[page_table].reshape(bd, past, FOX_HEADS).astype(F32)
    k_all = jnp.concatenate([kp, k], axis=1)
    v_all = jnp.concatenate([vp, v], axis=1)
    c = jnp.cumsum(jnp.concatenate([lp, logf.astype(F32)], axis=1), axis=1)
    kpos = jnp.arange(past + sn)
    qpos = past + jnp.arange(sn)
    return fox_sweep(q, c[:, past:], qpos, k_all, v_all, c, kpos)


def memory_kv(mem, mem_norm_w, w_mem_kv, mem_kn_w):
    bn, m, _ = mem.shape
    mn = rms_norm(mem, mem_norm_w)
    kv = jnp.einsum('bmd,de->bme', mn, w_mem_kv)
    mk, mv = jnp.split(kv, [MEM_HEADS * MEM_HD], axis=-1)
    mk = rms_norm(mk.reshape(bn, m, MEM_HEADS, MEM_HD), mem_kn_w)
    return mk, mv.reshape(bn, m, MEM_HEADS, MEM_HD)


def mem_attend(q, mk, mv):
    sc = jnp.einsum('bqhd,bmhd->bhqm', q, mk, preferred_element_type=F32) * (MEM_HD ** -0.5)
    p = jax.nn.softmax(sc, axis=-1)
    return jnp.einsum('bhqm,bmhd->bqhd', p.astype(mv.dtype), mv)


def hybrid_layer(x, lw, dn_s0, dn_buf0, lru_h0, lru_buf0, mem_k, mem_v, fox_attend):
    bn, s, _ = x.shape
    h = rms_norm(x, lw['norm_w'])
    proj = jnp.einsum('bsd,de->bse', h, lw['w_in']) + lw['b_in']
    (dn_qkv, dn_z, dn_b, dn_a, lru_x, lru_g, fq, fk, fv, ff, fg, mq, mg, mgate) = jnp.split(proj, IN_OFFSETS, axis=-1)

    qkv, dn_buf = causal_conv(dn_qkv, dn_buf0, lw['dn_conv_w'])
    qkv = jax.nn.silu(qkv)
    q, k, v = jnp.split(qkv, [DN_HEADS * DN_DK, 2 * DN_HEADS * DN_DK], axis=-1)
    q = l2norm(q.reshape(bn, s, DN_HEADS, DN_DK)) * (DN_DK ** -0.5)
    k = l2norm(k.reshape(bn, s, DN_HEADS, DN_DK))
    v = v.reshape(bn, s, DN_HEADS, DN_DV)
    beta = jax.nn.sigmoid(dn_b.astype(F32))
    log_alpha = -jnp.exp(lw['dn_A_log'].astype(F32)) * jax.nn.softplus(dn_a.astype(F32) + lw['dn_dt_bias'].astype(F32))
    o, dn_s = gated_delta_rule(q, k, v, beta, log_alpha, dn_s0)
    o = rms_norm(o, lw['dn_onorm_w']) * jax.nn.silu(dn_z.reshape(bn, s, DN_HEADS, DN_DV).astype(F32))
    out_a = o.reshape(bn, s, BR_WIDTH).astype(x.dtype)

    xc, lru_buf = causal_conv(lru_x, lru_buf0, lw['lru_conv_w'])
    hseq, lru_h = rg_lru(xc + lw['lru_conv_b'], lru_h0, lw['lru_w_r'], lw['lru_b_r'],
                         lw['lru_w_i'], lw['lru_b_i'], lw['lru_lambda'])
    out_b = (hseq * jax.nn.silu(lru_g.astype(F32))).astype(x.dtype)

    fq = rms_norm(fq.reshape(bn, s, FOX_HEADS, FOX_HD), lw['fox_qn_w'])
    fk = rms_norm(fk.reshape(bn, s, FOX_HEADS, FOX_HD), lw['fox_kn_w'])
    fv = fv.reshape(bn, s, FOX_HEADS, FOX_HD)
    logf = jax.nn.log_sigmoid(ff.astype(F32))
    o = fox_attend(fq, fk, fv, logf)
    out_c = (o.reshape(bn, s, BR_WIDTH) * jax.nn.silu(fg)).astype(x.dtype)

    mq = rms_norm(mq.reshape(bn, s, MEM_HEADS, MEM_HD), lw['mem_qn_w'])
    o = mem_attend(mq, mem_k, mem_v)
    out_m = (o.reshape(bn, s, BR_WIDTH) * jax.nn.silu(mg)).astype(x.dtype)

    branches = jnp.stack([out_a, out_b, out_c, out_m], axis=2)
    up = jnp.einsum('bsnw,nwd->bsnd', branches, lw['w_branch'])
    gates = jax.nn.sigmoid(mgate.reshape(bn, s, N_BRANCH, D_MODEL))
    merged = jnp.sum(gates * up, axis=2)
    y = x + jnp.einsum('bsd,de->bse', merged, lw['w_out'])
    return y, dn_s, dn_buf, lru_h, lru_buf, fk, fv, logf


def setup_inputs(seed: int = 0) -> dict:
    key = jax.random.key(seed)
    keys = jax.random.split(key, 48)
    kit = iter([keys[i] for i in range(48)])

    def nrm(shape, scale):
        return scale * jax.random.normal(next(kit), shape, F32)

    def gain(shape):
        return 1.0 + nrm(shape, 0.1)

    n_pages = PAST_LEN // PAGE_SIZE
    in_use = DEC_BATCH * n_pages
    n_pool = in_use + (in_use + 3) // 4
    page_table = jax.random.permutation(next(kit), n_pool)[:in_use].reshape(DEC_BATCH, n_pages).astype(jnp.int32)

    x_prompt = nrm((BATCH, SEQ, D_MODEL), 1.0)
    x_sample = nrm((DEC_BATCH, DEC_SEQ, D_MODEL), 1.0)
    mem_prompt = nrm((BATCH, MEM_TOKENS, D_MODEL), 1.0)

    cache_fox_k = nrm((DEPTH, n_pool, PAGE_SIZE, FOX_HEADS, FOX_HD), 1.0)
    cache_fox_v = nrm((DEPTH, n_pool, PAGE_SIZE, FOX_HEADS, FOX_HD), 1.0)
    cache_fox_logf = jax.nn.log_sigmoid(FOX_F_BIAS + nrm((DEPTH, n_pool, PAGE_SIZE, FOX_HEADS), 1.0))
    cache_mem_k = nrm((DEPTH, DEC_BATCH, MEM_TOKENS, MEM_HEADS, MEM_HD), 1.0)
    cache_mem_v = nrm((DEPTH, DEC_BATCH, MEM_TOKENS, MEM_HEADS, MEM_HD), 1.0)
    state_dn = nrm((DEPTH, DEC_BATCH, DN_HEADS, DN_DV, DN_DK), 0.5)
    state_dn_conv = nrm((DEPTH, DEC_BATCH, CONV_W - 1, DN_QKV), 1.0)
    state_lru_h = nrm((DEPTH, DEC_BATCH, LRU_WIDTH), 0.5)
    state_lru_conv = nrm((DEPTH, DEC_BATCH, CONV_W - 1, LRU_WIDTH), 1.0)

    norm_w = gain((DEPTH, D_MODEL))
    w_in = nrm((DEPTH, D_MODEL, N_IN), D_MODEL ** -0.5)
    b_in = nrm((DEPTH, N_IN), 0.02).at[:, FOX_F_OFF:FOX_F_OFF + FOX_HEADS].add(FOX_F_BIAS)
    dn_conv_w = nrm((DEPTH, CONV_W, DN_QKV), 0.5)
    dn_A_log = jnp.log(jax.random.uniform(next(kit), (DEPTH, DN_HEADS), F32, 1.0, 16.0))
    dt = jnp.exp(jax.random.uniform(next(kit), (DEPTH, DN_HEADS), F32, math.log(1e-3), math.log(1e-1)))
    dn_dt_bias = dt + jnp.log(-jnp.expm1(-dt))
    dn_onorm_w = gain((DEPTH, DN_DV))
    lru_conv_w = nrm((DEPTH, CONV_W, LRU_WIDTH), 0.5)
    lru_conv_b = nrm((DEPTH, LRU_WIDTH), 0.02)
    lru_w_r = nrm((DEPTH, LRU_BLOCKS, LRU_BW, LRU_BW), LRU_BW ** -0.5)
    lru_b_r = nrm((DEPTH, LRU_WIDTH), 0.02)
    lru_w_i = nrm((DEPTH, LRU_BLOCKS, LRU_BW, LRU_BW), LRU_BW ** -0.5)
    lru_b_i = nrm((DEPTH, LRU_WIDTH), 0.02)
    a_c = jax.random.uniform(next(kit), (DEPTH, LRU_WIDTH), F32, 0.9, 0.999)
    sig = a_c ** (1.0 / LRU_C)
    lru_lambda = jnp.log(sig) - jnp.log1p(-sig)
    fox_qn_w = gain((DEPTH, FOX_HD))
    fox_kn_w = gain((DEPTH, FOX_HD))
    mem_norm_w = gain((DEPTH, D_MODEL))
    w_mem_kv = nrm((DEPTH, D_MODEL, 2 * MEM_HEADS * MEM_HD), D_MODEL ** -0.5)
    mem_qn_w = gain((DEPTH, MEM_HD))
    mem_kn_w = gain((DEPTH, MEM_HD))
    w_branch = nrm((DEPTH, N_BRANCH, BR_WIDTH, D_MODEL), BR_WIDTH ** -0.5)
    w_out = nrm((DEPTH, D_MODEL, D_MODEL), D_MODEL ** -0.5)

    return {'x_prompt': x_prompt, 'x_sample': x_sample,
            'cache_fox_k': cache_fox_k, 'cache_fox_v': cache_fox_v, 'cache_fox_logf': cache_fox_logf,
            'cache_mem_k': cache_mem_k, 'cache_mem_v': cache_mem_v,
            'state_dn': state_dn, 'state_dn_conv': state_dn_conv,
            'state_lru_h': state_lru_h, 'state_lru_conv': state_lru_conv,
            'page_table': page_table, 'mem_prompt': mem_prompt,
            'norm_w': norm_w, 'w_in': w_in, 'b_in': b_in,
            'dn_conv_w': dn_conv_w, 'dn_A_log': dn_A_log, 'dn_dt_bias': dn_dt_bias, 'dn_onorm_w': dn_onorm_w,
            'lru_conv_w': lru_conv_w, 'lru_conv_b': lru_conv_b, 'lru_w_r': lru_w_r, 'lru_b_r': lru_b_r,
            'lru_w_i': lru_w_i, 'lru_b_i': lru_b_i, 'lru_lambda': lru_lambda,
            'fox_qn_w': fox_qn_w, 'fox_kn_w': fox_kn_w,
            'mem_norm_w': mem_norm_w, 'w_mem_kv': w_mem_kv, 'mem_qn_w': mem_qn_w, 'mem_kn_w': mem_kn_w,
            'w_branch': w_branch, 'w_out': w_out}


def reference(x_prompt, x_sample, cache_fox_k, cache_fox_v, cache_fox_logf, cache_mem_k, cache_mem_v,
              state_dn, state_dn_conv, state_lru_h, state_lru_conv, page_table, mem_prompt,
              norm_w, w_in, b_in, dn_conv_w, dn_A_log, dn_dt_bias, dn_onorm_w,
              lru_conv_w, lru_conv_b, lru_w_r, lru_b_r, lru_w_i, lru_b_i, lru_lambda,
              fox_qn_w, fox_kn_w, mem_norm_w, w_mem_kv, mem_qn_w, mem_kn_w, w_branch, w_out):
    bp = x_prompt.shape[0]
    yp, ys = x_prompt, x_sample
    names = ('fox_k', 'fox_v', 'fox_logf', 'mem_k', 'mem_v', 'dn', 'dn_conv', 'lru_h', 'lru_conv')
    accp = {n: [] for n in names}
    accs = {n: [] for n in names}
    for l in range(DEPTH):
        lw = {'norm_w': norm_w[l], 'w_in': w_in[l], 'b_in': b_in[l],
              'dn_conv_w': dn_conv_w[l], 'dn_A_log': dn_A_log[l], 'dn_dt_bias': dn_dt_bias[l],
              'dn_onorm_w': dn_onorm_w[l], 'lru_conv_w': lru_conv_w[l], 'lru_conv_b': lru_conv_b[l],
              'lru_w_r': lru_w_r[l], 'lru_b_r': lru_b_r[l], 'lru_w_i': lru_w_i[l], 'lru_b_i': lru_b_i[l],
              'lru_lambda': lru_lambda[l], 'fox_qn_w': fox_qn_w[l], 'fox_kn_w': fox_kn_w[l],
              'mem_qn_w': mem_qn_w[l], 'w_branch': w_branch[l], 'w_out': w_out[l]}

        mk, mv = memory_kv(mem_prompt, mem_norm_w[l], w_mem_kv[l], mem_kn_w[l])
        yp, dn_s, dn_b, lh, lb, fk, fv, lf = hybrid_layer(
            yp, lw,
            jnp.zeros((bp, DN_HEADS, DN_DV, DN_DK), F32),
            jnp.zeros((bp, CONV_W - 1, DN_QKV), x_prompt.dtype),
            jnp.zeros((bp, LRU_WIDTH), F32),
            jnp.zeros((bp, CONV_W - 1, LRU_WIDTH), x_prompt.dtype),
            mk, mv, fox_prompt)
        for n, val in zip(names, (fk, fv, lf, mk, mv, dn_s, dn_b, lh, lb)):
            accp[n].append(val)

        fox_fn = functools.partial(fox_sample, cache_k_l=cache_fox_k[l], cache_v_l=cache_fox_v[l],
                                   cache_logf_l=cache_fox_logf[l], page_table=page_table)
        ys, dn_s, dn_b, lh, lb, fk, fv, lf = hybrid_layer(
            ys, lw, state_dn[l], state_dn_conv[l], state_lru_h[l], state_lru_conv[l],
            cache_mem_k[l], cache_mem_v[l], fox_fn)
        for n, val in zip(('fox_k', 'fox_v', 'fox_logf', 'dn', 'dn_conv', 'lru_h', 'lru_conv'),
                          (fk, fv, lf, dn_s, dn_b, lh, lb)):
            accs[n].append(val)

    return (yp, ys,
            jnp.stack(accp['fox_k']), jnp.stack(accp['fox_v']), jnp.stack(accp['fox_logf']),
            jnp.stack(accp['mem_k']), jnp.stack(accp['mem_v']),
            jnp.stack(accp['dn']), jnp.stack(accp['dn_conv']),
            jnp.stack(accp['lru_h']), jnp.stack(accp['lru_conv']),
            jnp.stack(accs['fox_k']), jnp.stack(accs['fox_v']), jnp.stack(accs['fox_logf']),
            jnp.stack(accs['dn']), jnp.stack(accs['dn_conv']),
            jnp.stack(accs['lru_h']), jnp.stack(accs['lru_conv']))
```

```python
import functools

import jax
import jax.numpy as jnp
from jax import lax
from jax.experimental import pallas as pl
from jax.experimental.pallas import tpu as pltpu

F32 = jnp.float32
BF16 = jnp.bfloat16
HIGHEST = lax.Precision.HIGHEST

EPS = 1e-6
LRU_C = 8.0
CONV_W = 4
N_HEADS = 4
HEAD_DIM = 128
BR_WIDTH = N_HEADS * HEAD_DIM
N_BRANCH = 4
LANES = 128
SUBLANES = 8
DN_CHUNK = 128
NEG_BIG = -1e30
VMEM_LIMIT_BYTES = 56 * 1024 * 1024

C_DNQKV = 0
C_DNZ = 1536
C_LRUX = 2048
C_LRUG = 2560
C_FQ = 3072
C_FK = 3584
C_FV = 4096
C_FG = 4608
C_MQ = 5120
C_MG = 5632
C_MGATE = 6144
C_SMALL = 10240
N_PERM = 10368
SM_BETA = 0
SM_DECAY = 4
SM_FORGET = 8

NT_DIMS = (((1,), (1,)), ((), ()))


def _params(*sem):
    return pltpu.CompilerParams(dimension_semantics=sem, vmem_limit_bytes=VMEM_LIMIT_BYTES)


def _rms(x, w):
    return x * lax.rsqrt(jnp.mean(x * x, axis=-1, keepdims=True) + EPS) * w


def _l2norm(x):
    return x * lax.rsqrt(jnp.sum(x * x, axis=-1, keepdims=True) + EPS)


def _sigmoid(x):
    return 1.0 / (1.0 + jnp.exp(-x))


def _silu(x):
    return x * _sigmoid(x)


def _softplus(x):
    return jnp.maximum(x, 0.0) + jnp.log1p(jnp.exp(-jnp.abs(x)))


def _log_sigmoid(x):
    return -_softplus(-x)


def _dot(a, b):
    return jnp.dot(a.astype(BF16), b.astype(BF16), preferred_element_type=F32)


def _dot_nt(a, b):
    return lax.dot_general(a.astype(BF16), b.astype(BF16), NT_DIMS, preferred_element_type=F32)


def _dot_exact(a, b):
    return jnp.dot(a, b, precision=HIGHEST, preferred_element_type=F32)


def _iota2(shape, axis):
    return lax.broadcasted_iota(jnp.int32, shape, axis)


def _head(h):
    return slice(h * HEAD_DIM, (h + 1) * HEAD_DIM)


def _inproj_kernel(x_ref, nw_ref, w_ref, b_ref, o_ref, h_ref):
    @pl.when(pl.program_id(1) == 0)
    def _():
        h_ref[...] = _rms(x_ref[...], nw_ref[...]).astype(BF16)

    o_ref[...] = jnp.dot(h_ref[...], w_ref[...], preferred_element_type=F32) + b_ref[...]


def _inproj(x, norm_w, w, b):
    t, d = x.shape
    n = w.shape[1]
    tm = min(t, 1024)
    tn = 1152
    return pl.pallas_call(
        _inproj_kernel,
        out_shape=jax.ShapeDtypeStruct((t, n), F32),
        grid=(t // tm, n // tn),
        in_specs=[pl.BlockSpec((tm, d), lambda i, j: (i, 0)),
                  pl.BlockSpec((1, d), lambda i, j: (0, 0)),
                  pl.BlockSpec((d, tn), lambda i, j: (0, j)),
                  pl.BlockSpec((1, tn), lambda i, j: (0, j))],
        out_specs=pl.BlockSpec((tm, tn), lambda i, j: (i, j)),
        scratch_shapes=[pltpu.VMEM((tm, d), BF16)],
        compiler_params=_params("parallel", "arbitrary"),
        name="inproj",
    )(x, norm_w, w, b)


def _fox_prep_kernel(q_ref, k_ref, v_ref, sm_ref, qw_ref, kw_ref,
                     qn_ref, kn_ref, vo_ref, lf_ref, c_ref, ct_ref, carry_ref, *, tm):
    @pl.when(pl.program_id(1) == 0)
    def _():
        carry_ref[...] = jnp.zeros_like(carry_ref)

    scale = HEAD_DIM ** -0.5
    for h in range(N_HEADS):
        qn_ref[:, _head(h)] = _rms(q_ref[:, _head(h)], qw_ref[...]) * scale
        kn_ref[:, _head(h)] = _rms(k_ref[:, _head(h)], kw_ref[...])
    vo_ref[...] = v_ref[...]
    lf = _log_sigmoid(sm_ref[...])
    lf_ref[...] = lf[:, SM_FORGET:SM_FORGET + N_HEADS]
    tri = (_iota2((tm, tm), 0) >= _iota2((tm, tm), 1)).astype(F32)
    c = _dot_exact(tri, lf) + carry_ref[...]
    c_ref[...] = c
    ct_ref[...] = c.T
    carry_ref[...] = c[tm - 1:tm, :]


def _fox_prep(proj, qn_w, kn_w, nb, s):
    t = proj.shape[0]
    tm = min(s, 256)
    nj = s // tm
    row = lambda b, j: b * nj + j
    wide = lambda off: pl.BlockSpec((tm, BR_WIDTH), lambda b, j: (row(b, j), off // BR_WIDTH))
    out_rows = pl.BlockSpec((tm, BR_WIDTH), lambda b, j: (row(b, j), 0))
    return pl.pallas_call(
        functools.partial(_fox_prep_kernel, tm=tm),
        out_shape=(jax.ShapeDtypeStruct((t, BR_WIDTH), F32),
                   jax.ShapeDtypeStruct((t, BR_WIDTH), F32),
                   jax.ShapeDtypeStruct((t, BR_WIDTH), F32),
                   jax.ShapeDtypeStruct((t, N_HEADS), F32),
                   jax.ShapeDtypeStruct((t, LANES), F32),
                   jax.ShapeDtypeStruct((nb, LANES, s), F32)),
        grid=(nb, nj),
        in_specs=[wide(C_FQ), wide(C_FK), wide(C_FV),
                  pl.BlockSpec((tm, LANES), lambda b, j: (row(b, j), C_SMALL // LANES)),
                  pl.BlockSpec((1, HEAD_DIM), lambda b, j: (0, 0)),
                  pl.BlockSpec((1, HEAD_DIM), lambda b, j: (0, 0))],
        out_specs=(out_rows, out_rows, out_rows,
                   pl.BlockSpec((tm, N_HEADS), lambda b, j: (row(b, j), 0)),
                   pl.BlockSpec((tm, LANES), lambda b, j: (row(b, j), 0)),
                   pl.BlockSpec((None, LANES, tm), lambda b, j: (b, 0, j))),
        scratch_shapes=[pltpu.VMEM((1, LANES), F32)],
        compiler_params=_params("parallel", "arbitrary"),
        name="fox_prep",
    )(proj, proj, proj, proj, qn_w, kn_w)


def _fox_flash_kernel(q_ref, k_ref, v_ref, cq_ref, ck_ref, g_ref, o_ref, m_ref, l_ref, acc_ref, *, tq):
    i = pl.program_id(1)
    j = pl.program_id(2)

    @pl.when(j == 0)
    def _():
        m_ref[...] = jnp.full_like(m_ref, NEG_BIG)
        l_ref[...] = jnp.zeros_like(l_ref)
        acc_ref[...] = jnp.zeros_like(acc_ref)

    def update(diagonal):
        for h in range(N_HEADS):
            s = _dot_nt(q_ref[:, _head(h)], k_ref[:, _head(h)])
            s = s + (cq_ref[:, SM_FORGET + h:SM_FORGET + h + 1] - ck_ref[h:h + 1, :])
            if diagonal:
                s = jnp.where(_iota2((tq, tq), 0) >= _iota2((tq, tq), 1), s, NEG_BIG)
            m_prev = m_ref[h]
            m_new = jnp.maximum(m_prev, jnp.max(s, axis=-1, keepdims=True))
            alpha = jnp.exp(m_prev - m_new)
            p = jnp.exp(s - m_new)
            l_ref[h] = alpha * l_ref[h] + jnp.sum(p, axis=-1, keepdims=True)
            acc_ref[h] = alpha * acc_ref[h] + _dot(p, v_ref[:, _head(h)])
            m_ref[h] = m_new

    @pl.when(j < i)
    def _():
        update(False)

    @pl.when(j == i)
    def _():
        update(True)
        for h in range(N_HEADS):
            o = acc_ref[h] / l_ref[h]
            o_ref[:, _head(h)] = (o * _silu(g_ref[:, _head(h)])).astype(BF16)


def _fox_flash(qn, kn, v, c, ct, proj, nb, s):
    t = qn.shape[0]
    tq = min(s, 512)
    nq = s // tq
    qrow = lambda b, i, j: (b * nq + i, 0)
    krow = lambda b, i, j: (b * nq + jnp.minimum(i, j), 0)
    return pl.pallas_call(
        functools.partial(_fox_flash_kernel, tq=tq),
        out_shape=jax.ShapeDtypeStruct((t, BR_WIDTH), BF16),
        grid=(nb, nq, nq),
        in_specs=[pl.BlockSpec((tq, BR_WIDTH), qrow),
                  pl.BlockSpec((tq, BR_WIDTH), krow),
                  pl.BlockSpec((tq, BR_WIDTH), krow),
                  pl.BlockSpec((tq, LANES), qrow),
                  pl.BlockSpec((None, SUBLANES, tq), lambda b, i, j: (b, SM_FORGET // SUBLANES, jnp.minimum(i, j))),
                  pl.BlockSpec((tq, BR_WIDTH), lambda b, i, j: (b * nq + i, C_FG // BR_WIDTH))],
        out_specs=pl.BlockSpec((tq, BR_WIDTH), qrow),
        scratch_shapes=[pltpu.VMEM((N_HEADS, tq, 1), F32),
                        pltpu.VMEM((N_HEADS, tq, 1), F32),
                        pltpu.VMEM((N_HEADS, tq, HEAD_DIM), F32)],
        compiler_params=_params("parallel", "parallel", "arbitrary"),
        name="fox_flash",
    )(qn, kn, v, c, ct, proj)


def _mem_kv_kernel(m_ref, nw_ref, w_ref, kw_ref, mk_ref, mv_ref):
    kv = _dot(_rms(m_ref[...], nw_ref[...]), w_ref[...])
    for h in range(N_HEADS):
        mk_ref[:, _head(h)] = _rms(kv[:, _head(h)], kw_ref[...])
    mv_ref[...] = kv[:, BR_WIDTH:]


def _mem_kv(mem, norm_w, w_kv, kn_w):
    t, d = mem.shape
    tm = min(t, 256)
    return pl.pallas_call(
        _mem_kv_kernel,
        out_shape=(jax.ShapeDtypeStruct((t, BR_WIDTH), F32), jax.ShapeDtypeStruct((t, BR_WIDTH), F32)),
        grid=(t // tm,),
        in_specs=[pl.BlockSpec((tm, d), lambda i: (i, 0)),
                  pl.BlockSpec((1, d), lambda i: (0, 0)),
                  pl.BlockSpec((d, 2 * BR_WIDTH), lambda i: (0, 0)),
                  pl.BlockSpec((1, HEAD_DIM), lambda i: (0, 0))],
        out_specs=(pl.BlockSpec((tm, BR_WIDTH), lambda i: (i, 0)),
                   pl.BlockSpec((tm, BR_WIDTH), lambda i: (i, 0))),
        compiler_params=_params("parallel"),
        name="mem_kv",
    )(mem, norm_w, w_kv, kn_w)


def _mem_attn_kernel(q_ref, g_ref, k_ref, v_ref, qw_ref, o_ref):
    scale = HEAD_DIM ** -0.5
    for h in range(N_HEADS):
        q = _rms(q_ref[:, _head(h)], qw_ref[...]) * scale
        s = _dot_nt(q, k_ref[:, _head(h)])
        p = jnp.exp(s - jnp.max(s, axis=-1, keepdims=True))
        o = _dot(p, v_ref[:, _head(h)]) / jnp.sum(p, axis=-1, keepdims=True)
        o_ref[:, _head(h)] = (o * _silu(g_ref[:, _head(h)])).astype(BF16)


def _mem_attn(proj, mk, mv, qn_w, nb, s):
    t = proj.shape[0]
    m = mk.shape[0] // nb
    tq = min(s, 512)
    nq = s // tq
    return pl.pallas_call(
        _mem_attn_kernel,
        out_shape=jax.ShapeDtypeStruct((t, BR_WIDTH), BF16),
        grid=(nb, nq),
        in_specs=[pl.BlockSpec((tq, BR_WIDTH), lambda b, i: (b * nq + i, C_MQ // BR_WIDTH)),
                  pl.BlockSpec((tq, BR_WIDTH), lambda b, i: (b * nq + i, C_MG // BR_WIDTH)),
                  pl.BlockSpec((m, BR_WIDTH), lambda b, i: (b, 0)),
                  pl.BlockSpec((m, BR_WIDTH), lambda b, i: (b, 0)),
                  pl.BlockSpec((1, HEAD_DIM), lambda b, i: (0, 0))],
        out_specs=pl.BlockSpec((tq, BR_WIDTH), lambda b, i: (b * nq + i, 0)),
        compiler_params=_params("parallel", "parallel"),
        name="mem_attn",
    )(proj, proj, mk, mv, qn_w)


def _causal_conv_tile(x, halo, w_ref):
    rows8 = _iota2(halo.shape, 0)
    y = x * w_ref[CONV_W - 1:CONV_W, :]
    for k in range(1, CONV_W):
        xr = pltpu.roll(x, k, axis=0)
        hr = pltpu.roll(halo, k, axis=0)
        top = jnp.where(rows8 < k, hr, xr[:SUBLANES])
        xs = jnp.concatenate([top, xr[SUBLANES:]], axis=0)
        y = y + xs * w_ref[CONV_W - 1 - k:CONV_W - k, :]
    return y


def _unit_lower_inverse(a, row, col):
    n = a.shape[0]

    def off_block(shift):
        return (((row >> (shift + 1)) == (col >> (shift + 1)))
                & (((row >> shift) & 1) == 1) & (((col >> shift) & 1) == 0))

    d = (row == col).astype(F32) - jnp.where(off_block(0), a, 0.0)
    shift = 1
    while (1 << shift) < n:
        low = jnp.where(off_block(shift), a, 0.0)
        d = d - _dot(d, _dot(low, d))
        shift += 1
    return d


def _dn_kernel(x_ref, z_ref, sm_ref, cw_ref, al_ref, dt_ref, ow_ref,
               o_ref, st_ref, stt_ref, halo_ref, *, tl):
    j = pl.program_id(1)

    @pl.when(j == 0)
    def _():
        stt_ref[...] = jnp.zeros_like(stt_ref)
        halo_ref[...] = jnp.zeros_like(halo_ref)

    x = x_ref[...]
    qkv = _silu(_causal_conv_tile(x, halo_ref[...], cw_ref))
    halo_ref[...] = x[tl - SUBLANES:, :]
    sm = sm_ref[...]
    beta_all = _sigmoid(sm)
    la_all = -jnp.exp(al_ref[...]) * _softplus(sm + dt_ref[...])

    cc = DN_CHUNK
    row = _iota2((cc, cc), 0)
    col = _iota2((cc, cc), 1)
    incl = row >= col
    strict = row > col
    tri = incl.astype(F32)
    kd = N_HEADS * HEAD_DIM

    for c in range(tl // cc):
        rows = slice(c * cc, (c + 1) * cc)
        g_all = _dot_exact(tri, la_all[rows])
        gt_all = g_all.T
        eg_all = jnp.exp(g_all)
        for h in range(N_HEADS):
            q = _l2norm(qkv[rows, h * HEAD_DIM:(h + 1) * HEAD_DIM]) * (HEAD_DIM ** -0.5)
            k = _l2norm(qkv[rows, kd + h * HEAD_DIM:kd + (h + 1) * HEAD_DIM])
            v = qkv[rows, 2 * kd + h * HEAD_DIM:2 * kd + (h + 1) * HEAD_DIM]
            beta = beta_all[rows, SM_BETA + h:SM_BETA + h + 1]
            gc = g_all[:, SM_DECAY + h:SM_DECAY + h + 1]
            gr = gt_all[SM_DECAY + h:SM_DECAY + h + 1, :]
            eg = eg_all[:, SM_DECAY + h:SM_DECAY + h + 1]
            g_last = g_all[cc - 1:cc, SM_DECAY + h:SM_DECAY + h + 1]
            decay = jnp.exp(jnp.where(incl, gc - gr, NEG_BIG))
            a = jnp.where(strict, beta * decay * _dot_nt(k, k), 0.0)
            qk = decay * _dot_nt(q, k)
            rhs = jnp.concatenate([beta * v, (beta * eg) * k], axis=1)
            sol = _dot(_unit_lower_inverse(a, row, col), rhs)
            u0, w = sol[:, :HEAD_DIM], sol[:, HEAD_DIM:]
            kdec = jnp.exp(g_last - gc) * k
            stt = stt_ref[h]
            u = u0 - _dot(w, stt)
            o = eg * _dot(q, stt) + _dot(qk, u)
            stt_ref[h] = jnp.exp(g_last) * stt + _dot(kdec.T, u)
            o = _rms(o, ow_ref[...]) * _silu(z_ref[rows, _head(h)])
            o_ref[rows, _head(h)] = o.astype(BF16)

    @pl.when(j == pl.num_programs(1) - 1)
    def _():
        for h in range(N_HEADS):
            st_ref[h] = stt_ref[h].T


def _dn_prompt(proj, conv_w, al_row, dt_row, onorm_w, nb, s):
    t = proj.shape[0]
    tl = min(s, 256)
    nj = s // tl
    qkv_w = 3 * BR_WIDTH
    row = lambda b, j: b * nj + j
    return pl.pallas_call(
        functools.partial(_dn_kernel, tl=tl),
        out_shape=(jax.ShapeDtypeStruct((t, BR_WIDTH), BF16),
                   jax.ShapeDtypeStruct((nb, N_HEADS, HEAD_DIM, HEAD_DIM), F32)),
        grid=(nb, nj),
        in_specs=[pl.BlockSpec((tl, qkv_w), lambda b, j: (row(b, j), 0)),
                  pl.BlockSpec((tl, BR_WIDTH), lambda b, j: (row(b, j), C_DNZ // BR_WIDTH)),
                  pl.BlockSpec((tl, LANES), lambda b, j: (row(b, j), C_SMALL // LANES)),
                  pl.BlockSpec((CONV_W, qkv_w), lambda b, j: (0, 0)),
                  pl.BlockSpec((1, LANES), lambda b, j: (0, 0)),
                  pl.BlockSpec((1, LANES), lambda b, j: (0, 0)),
                  pl.BlockSpec((1, HEAD_DIM), lambda b, j: (0, 0))],
        out_specs=(pl.BlockSpec((tl, BR_WIDTH), lambda b, j: (row(b, j), 0)),
                   pl.BlockSpec((None, N_HEADS, HEAD_DIM, HEAD_DIM), lambda b, j: (b, 0, 0, 0))),
        scratch_shapes=[pltpu.VMEM((N_HEADS, HEAD_DIM, HEAD_DIM), F32),
                        pltpu.VMEM((SUBLANES, qkv_w), F32)],
        compiler_params=_params("parallel", "arbitrary"),
        name="dn_prompt",
    )(proj, proj, proj, conv_w, al_row, dt_row, onorm_w)


def _lru_gates(xc, wr_ref, br_ref, wi_ref, bi_ref, lam_ref):
    r = _sigmoid(_dot(xc, wr_ref[...]) + br_ref[...])
    i = _sigmoid(_dot(xc, wi_ref[...]) + bi_ref[...])
    log_a = -LRU_C * r * _softplus(-lam_ref[...])
    a = jnp.exp(log_a)
    one_minus_a2 = -jnp.tanh(log_a) * (a * a + 1.0)
    return a, jnp.sqrt(one_minus_a2) * (i * xc)


def _lru_kernel(x_ref, g_ref, cw_ref, cb_ref, wr_ref, br_ref, wi_ref, bi_ref, lam_ref,
                o_ref, hl_ref, hc_ref, halo_ref, *, tl):
    @pl.when(pl.program_id(1) == 0)
    def _():
        hc_ref[...] = jnp.zeros_like(hc_ref)
        halo_ref[...] = jnp.zeros_like(halo_ref)

    x = x_ref[...]
    xc = _causal_conv_tile(x, halo_ref[...], cw_ref) + cb_ref[...]
    halo_ref[...] = x[tl - SUBLANES:, :]
    a, b = _lru_gates(xc, wr_ref, br_ref, wi_ref, bi_ref, lam_ref)
    rows = _iota2(a.shape, 0)
    d = 1
    while d < tl:
        a_up = jnp.where(rows >= d, pltpu.roll(a, d, axis=0), 1.0)
        b_up = jnp.where(rows >= d, pltpu.roll(b, d, axis=0), 0.0)
        b = a * b_up + b
        a = a * a_up
        d *= 2
    hseq = a * hc_ref[...] + b
    hc_ref[...] = hseq[tl - 1:tl, :]
    hl_ref[...] = hseq[tl - 1:tl, :]
    o_ref[...] = (hseq * _silu(g_ref[...])).astype(BF16)


def _lru_prompt(proj, conv_w, conv_b, w_r, b_r, w_i, b_i, lam, nb, s):
    t = proj.shape[0]
    tl = min(s, 256)
    nj = s // tl
    row = lambda b, j: b * nj + j
    vec = pl.BlockSpec((1, BR_WIDTH), lambda b, j: (0, 0))
    mat = pl.BlockSpec((BR_WIDTH, BR_WIDTH), lambda b, j: (0, 0))
    return pl.pallas_call(
        functools.partial(_lru_kernel, tl=tl),
        out_shape=(jax.ShapeDtypeStruct((t, BR_WIDTH), BF16),
                   jax.ShapeDtypeStruct((nb, 1, BR_WIDTH), F32)),
        grid=(nb, nj),
        in_specs=[pl.BlockSpec((tl, BR_WIDTH), lambda b, j: (row(b, j), C_LRUX // BR_WIDTH)),
                  pl.BlockSpec((tl, BR_WIDTH), lambda b, j: (row(b, j), C_LRUG // BR_WIDTH)),
                  pl.BlockSpec((CONV_W, BR_WIDTH), lambda b, j: (0, 0)),
                  vec, mat, vec, mat, vec, vec],
        out_specs=(pl.BlockSpec((tl, BR_WIDTH), lambda b, j: (row(b, j), 0)),
                   pl.BlockSpec((None, 1, BR_WIDTH), lambda b, j: (b, 0, 0))),
        scratch_shapes=[pltpu.VMEM((1, BR_WIDTH), F32), pltpu.VMEM((SUBLANES, BR_WIDTH), F32)],
        compiler_params=_params("parallel", "arbitrary"),
        name="lru_prompt",
    )(proj, proj, conv_w, conv_b, w_r, b_r, w_i, b_i, lam)


def _merge_kernel(x_ref, a_ref, b_ref, c_ref, m_ref, g0_ref, g1_ref, g2_ref, g3_ref, wb_ref, wo_ref, y_ref):
    merged = None
    for n, (br, gate) in enumerate(((a_ref, g0_ref), (b_ref, g1_ref), (c_ref, g2_ref), (m_ref, g3_ref))):
        up = jnp.dot(br[...], wb_ref[n], preferred_element_type=F32)
        term = _sigmoid(gate[...]) * up
        merged = term if merged is None else merged + term
    y_ref[...] = x_ref[...] + jnp.dot(merged.astype(BF16), wo_ref[...], preferred_element_type=F32)


def _merge(x, out_a, out_b, out_c, out_m, proj, w_branch, w_out):
    t, d = x.shape
    tm = min(t, 512)
    rows = lambda w: pl.BlockSpec((tm, w), lambda i: (i, 0))
    gate = lambda n: pl.BlockSpec((tm, d), lambda i: (i, C_MGATE // d + n))
    return pl.pallas_call(
        _merge_kernel,
        out_shape=jax.ShapeDtypeStruct((t, d), F32),
        grid=(t // tm,),
        in_specs=[rows(d), rows(BR_WIDTH), rows(BR_WIDTH), rows(BR_WIDTH), rows(BR_WIDTH),
                  gate(0), gate(1), gate(2), gate(3),
                  pl.BlockSpec((N_BRANCH, BR_WIDTH, d), lambda i: (0, 0, 0)),
                  pl.BlockSpec((d, d), lambda i: (0, 0))],
        out_specs=rows(d),
        compiler_params=_params("parallel"),
        name="merge",
    )(x, out_a, out_b, out_c, out_m, proj, proj, proj, proj, w_branch, w_out)


def _sample_step_kernel(qkv_ref, z_ref, lx_ref, lg_ref, sm_ref, st_ref, dbuf_ref, lh_ref, lbuf_ref,
                        dcw_ref, al_ref, dt_ref, ow_ref, lcw_ref, lcb_ref, wr_ref, br_ref, wi_ref, bi_ref, lam_ref,
                        oa_ref, ob_ref, sto_ref, dbo_ref, lho_ref, lbo_ref, o_scr, *, bs):
    kd = N_HEADS * HEAD_DIM

    def step_conv(x, buf_ref, bufo_ref, w_ref):
        y = x * w_ref[CONV_W - 1:CONV_W, :]
        for k in range(CONV_W - 1):
            y = y + buf_ref[:, k, :] * w_ref[k:k + 1, :]
        for k in range(CONV_W - 2):
            bufo_ref[:, k, :] = buf_ref[:, k + 1, :]
        bufo_ref[:, CONV_W - 2, :] = x
        return y

    qkv = _silu(step_conv(qkv_ref[...], dbuf_ref, dbo_ref, dcw_ref))
    sm = sm_ref[...]
    beta_all = _sigmoid(sm)
    eg_all = jnp.exp(-jnp.exp(al_ref[...]) * _softplus(sm + dt_ref[...]))
    eye = _iota2((HEAD_DIM, HEAD_DIM), 0) == _iota2((HEAD_DIM, HEAD_DIM), 1)
    pad = jnp.zeros((SUBLANES - 2, HEAD_DIM), F32)
    for h in range(N_HEADS):
        q = _l2norm(qkv[:, h * HEAD_DIM:(h + 1) * HEAD_DIM]) * (HEAD_DIM ** -0.5)
        k = _l2norm(qkv[:, kd + h * HEAD_DIM:kd + (h + 1) * HEAD_DIM])
        v = qkv[:, 2 * kd + h * HEAD_DIM:2 * kd + (h + 1) * HEAD_DIM]
        beta = beta_all[:, SM_BETA + h:SM_BETA + h + 1]
        eg = eg_all[:, SM_DECAY + h:SM_DECAY + h + 1]
        qk = jnp.sum(q * k, axis=-1, keepdims=True)
        for b in range(bs):
            st = st_ref[b, h]
            kq = jnp.concatenate([k[b:b + 1], q[b:b + 1], pad], axis=0)
            sk_sq = _dot_nt(kq, st)
            eg_b = eg[b:b + 1]
            u = beta[b:b + 1] * (v[b:b + 1] - eg_b * sk_sq[0:1])
            o_scr[b:b + 1, _head(h)] = eg_b * sk_sq[1:2] + qk[b:b + 1] * u
            u_diag = jnp.where(eye, jnp.broadcast_to(u, (HEAD_DIM, HEAD_DIM)), 0.0)
            k_rows = jnp.broadcast_to(k[b:b + 1], (HEAD_DIM, HEAD_DIM))
            sto_ref[b, h] = eg_b * st + _dot(u_diag, k_rows)
    o = o_scr[...]
    for h in range(N_HEADS):
        oa_ref[:, _head(h)] = (_rms(o[:, _head(h)], ow_ref[...]) * _silu(z_ref[:, _head(h)])).astype(BF16)

    xc = step_conv(lx_ref[...], lbuf_ref, lbo_ref, lcw_ref) + lcb_ref[...]
    a, bx = _lru_gates(xc, wr_ref, br_ref, wi_ref, bi_ref, lam_ref)
    hnew = a * lh_ref[...] + bx
    lho_ref[...] = hnew
    ob_ref[...] = (hnew * _silu(lg_ref[...])).astype(BF16)


def _sample_step(proj, layer, state_dn, state_dn_conv, state_lru_h, state_lru_conv,
                 dn_conv_w, al_row, dt_row, onorm_w, lru_conv_w, lru_conv_b, w_r, b_r, w_i, b_i, lam):
    nb = proj.shape[0]
    bs = SUBLANES
    qkv_w = 3 * BR_WIDTH
    wide = lambda off: pl.BlockSpec((bs, BR_WIDTH), lambda i: (i, off // BR_WIDTH))
    const = lambda shape: pl.BlockSpec(shape, lambda i: (0,) * len(shape))
    st_in = pl.BlockSpec((None, bs, N_HEADS, HEAD_DIM, HEAD_DIM), lambda i: (layer, i, 0, 0, 0))
    dbuf_in = pl.BlockSpec((None, bs, CONV_W - 1, qkv_w), lambda i: (layer, i, 0, 0))
    lh_in = pl.BlockSpec((None, bs, BR_WIDTH), lambda i: (layer, i, 0))
    lbuf_in = pl.BlockSpec((None, bs, CONV_W - 1, BR_WIDTH), lambda i: (layer, i, 0, 0))
    return pl.pallas_call(
        functools.partial(_sample_step_kernel, bs=bs),
        out_shape=(jax.ShapeDtypeStruct((nb, BR_WIDTH), BF16),
                   jax.ShapeDtypeStruct((nb, BR_WIDTH), BF16),
                   jax.ShapeDtypeStruct((nb, N_HEADS, HEAD_DIM, HEAD_DIM), F32),
                   jax.ShapeDtypeStruct((nb, CONV_W - 1, qkv_w), F32),
                   jax.ShapeDtypeStruct((nb, BR_WIDTH), F32),
                   jax.ShapeDtypeStruct((nb, CONV_W - 1, BR_WIDTH), F32)),
        grid=(nb // bs,),
        in_specs=[pl.BlockSpec((bs, qkv_w), lambda i: (i, 0)), wide(C_DNZ), wide(C_LRUX), wide(C_LRUG),
                  pl.BlockSpec((bs, LANES), lambda i: (i, C_SMALL // LANES)),
                  st_in, dbuf_in, lh_in, lbuf_in,
                  const((CONV_W, qkv_w)), const((1, LANES)), const((1, LANES)), const((1, HEAD_DIM)),
                  const((CONV_W, BR_WIDTH)), const((1, BR_WIDTH)),
                  const((BR_WIDTH, BR_WIDTH)), const((1, BR_WIDTH)),
                  const((BR_WIDTH, BR_WIDTH)), const((1, BR_WIDTH)), const((1, BR_WIDTH))],
        out_specs=(pl.BlockSpec((bs, BR_WIDTH), lambda i: (i, 0)),
                   pl.BlockSpec((bs, BR_WIDTH), lambda i: (i, 0)),
                   pl.BlockSpec((bs, N_HEADS, HEAD_DIM, HEAD_DIM), lambda i: (i, 0, 0, 0)),
                   pl.BlockSpec((bs, CONV_W - 1, qkv_w), lambda i: (i, 0, 0)),
                   pl.BlockSpec((bs, BR_WIDTH), lambda i: (i, 0)),
                   pl.BlockSpec((bs, CONV_W - 1, BR_WIDTH), lambda i: (i, 0, 0))),
        scratch_shapes=[pltpu.VMEM((bs, BR_WIDTH), F32)],
        compiler_params=_params("parallel"),
        name="sample_step",
    )(proj, proj, proj, proj, proj, state_dn, state_dn_conv, state_lru_h, state_lru_conv,
      dn_conv_w, al_row, dt_row, onorm_w, lru_conv_w, lru_conv_b, w_r, b_r, w_i, b_i, lam)


def _head_rows_mask():
    return _iota2((SUBLANES, BR_WIDTH), 0) == (_iota2((SUBLANES, BR_WIDTH), 1) // HEAD_DIM)


def _spread_heads(q_row, mask):
    return jnp.where(mask, jnp.broadcast_to(q_row, mask.shape), 0.0)


def _gather_heads(o, mask):
    return jnp.sum(jnp.where(mask, o, 0.0), axis=0, keepdims=True)


def _fox_decode_kernel(pt_ref, q_ref, kn_ref, vn_ref, lfn_ref, g_ref, *rest, n_pages, page):
    k_refs = rest[:n_pages]
    v_refs = rest[n_pages:2 * n_pages]
    lf_refs = rest[2 * n_pages:3 * n_pages]
    o_ref = rest[3 * n_pages]
    lf_scr = rest[3 * n_pages + 1]
    del pt_ref
    mask = _head_rows_mask()
    q_bd = _spread_heads(q_ref[...], mask)
    q_bf = q_bd.astype(BF16)

    lf_scr[...] = jnp.zeros_like(lf_scr)
    for p in range(n_pages):
        lf_scr[p, 0:N_HEADS, :] = lf_refs[p][...]
    n_rows = n_pages * SUBLANES
    lf_all = lf_scr[...].reshape(n_rows, page)
    after = (_iota2((page, page), 0) > _iota2((page, page), 1)).astype(F32)
    in_page = _dot_exact(lf_all, after)
    totals = jnp.broadcast_to(jnp.sum(lf_all, axis=-1, keepdims=True), (n_rows, page))
    r = _iota2((n_rows, n_rows), 0)
    c = _iota2((n_rows, n_rows), 1)
    later_pages = (((r % SUBLANES) == (c % SUBLANES)) & ((c // SUBLANES) > (r // SUBLANES))).astype(F32)
    bias_all = in_page + _dot_exact(later_pages, totals)
    lf_new = lfn_ref[...]

    scores = []
    for p in range(n_pages):
        s = _dot_nt(q_bf, k_refs[p][...])
        scores.append(s + bias_all[p * SUBLANES:(p + 1) * SUBLANES] + lf_new)
    s_self = jnp.sum(q_bd * kn_ref[...], axis=-1, keepdims=True)
    m = s_self
    for s in scores:
        m = jnp.maximum(m, jnp.max(s, axis=-1, keepdims=True))
    p_self = jnp.exp(s_self - m)
    l = p_self
    acc = p_self * vn_ref[...]
    for p in range(n_pages):
        pr = jnp.exp(scores[p] - m)
        l = l + jnp.sum(pr, axis=-1, keepdims=True)
        acc = acc + _dot(pr, v_refs[p][...])
    o = _gather_heads(acc / l, mask)
    o_ref[...] = (o * _silu(g_ref[...])).astype(BF16)


def _fox_decode(qn, kn, vn, lf_new_rows, fg, cache_k, cache_v, cache_lf_t, page_table, layer, n_pool):
    nb = qn.shape[0]
    n_pages = page_table.shape[0] // nb
    page = cache_k.shape[1]
    base = layer * n_pool
    row = pl.BlockSpec((None, 1, BR_WIDTH), lambda b, pt: (b, 0, 0))

    def paged(shape, p):
        return pl.BlockSpec((None,) + shape, lambda b, pt: (base + pt[b * n_pages + p], 0, 0))

    in_specs = [row, row, row, pl.BlockSpec((None, SUBLANES, LANES), lambda b, pt: (b, 0, 0)), row]
    in_specs += [paged((page, BR_WIDTH), p) for p in range(n_pages)]
    in_specs += [paged((page, BR_WIDTH), p) for p in range(n_pages)]
    in_specs += [paged((N_HEADS, page), p) for p in range(n_pages)]
    grid_spec = pltpu.PrefetchScalarGridSpec(
        num_scalar_prefetch=1,
        grid=(nb,),
        in_specs=in_specs,
        out_specs=pl.BlockSpec((None, 1, BR_WIDTH), lambda b, pt: (b, 0, 0)),
        scratch_shapes=[pltpu.VMEM((n_pages, SUBLANES, page), F32)])
    return pl.pallas_call(
        functools.partial(_fox_decode_kernel, n_pages=n_pages, page=page),
        out_shape=jax.ShapeDtypeStruct((nb, 1, BR_WIDTH), BF16),
        grid_spec=grid_spec,
        compiler_params=_params("parallel"),
        name="fox_decode",
    )(page_table, qn, kn, vn, lf_new_rows, fg,
      *([cache_k] * n_pages), *([cache_v] * n_pages), *([cache_lf_t] * n_pages))


def _mem_decode_kernel(q_ref, g_ref, k_ref, v_ref, qw_ref, o_ref, o_scr, *, bs):
    mask = _head_rows_mask()
    scale = HEAD_DIM ** -0.5
    q = jnp.concatenate([_rms(q_ref[:, _head(h)], qw_ref[...]) * scale for h in range(N_HEADS)], axis=1)
    for b in range(bs):
        q_bd = _spread_heads(q[b:b + 1], mask)
        s = _dot_nt(q_bd, k_ref[b])
        p = jnp.exp(s - jnp.max(s, axis=-1, keepdims=True))
        o = _dot(p, v_ref[b]) / jnp.sum(p, axis=-1, keepdims=True)
        o_scr[b:b + 1, :] = _gather_heads(o, mask)
    o_ref[...] = (o_scr[...] * _silu(g_ref[...])).astype(BF16)


def _mem_decode(proj, cache_k, cache_v, qn_w, layer):
    nb = proj.shape[0]
    bs = SUBLANES
    m = cache_k.shape[1]
    base = layer * (nb // bs)
    kv = pl.BlockSpec((bs, m, BR_WIDTH), lambda i: (base + i, 0, 0))
    return pl.pallas_call(
        functools.partial(_mem_decode_kernel, bs=bs),
        out_shape=jax.ShapeDtypeStruct((nb, BR_WIDTH), BF16),
        grid=(nb // bs,),
        in_specs=[pl.BlockSpec((bs, BR_WIDTH), lambda i: (i, C_MQ // BR_WIDTH)),
                  pl.BlockSpec((bs, BR_WIDTH), lambda i: (i, C_MG // BR_WIDTH)),
                  kv, kv, pl.BlockSpec((1, HEAD_DIM), lambda i: (0, 0))],
        out_specs=pl.BlockSpec((bs, BR_WIDTH), lambda i: (i, 0)),
        scratch_shapes=[pltpu.VMEM((bs, BR_WIDTH), F32)],
        compiler_params=_params("parallel"),
        name="mem_decode",
    )(proj, proj, cache_k, cache_v, qn_w)


def _permute_columns(a):
    sizes = (3 * BR_WIDTH, BR_WIDTH, N_HEADS, N_HEADS, BR_WIDTH, BR_WIDTH,
             BR_WIDTH, BR_WIDTH, BR_WIDTH, N_HEADS, BR_WIDTH, BR_WIDTH, BR_WIDTH)
    names = ("dn_qkv", "dn_z", "dn_b", "dn_a", "lru_x", "lru_g", "fq", "fk", "fv", "ff", "fg", "mq", "mg")
    parts, off = {}, 0
    for name, size in zip(names, sizes):
        parts[name] = a[..., off:off + size]
        off += size
    parts["mgate"] = a[..., off:]
    small = jnp.concatenate([parts["dn_b"], parts["dn_a"], parts["ff"]], axis=-1)
    small = jnp.pad(small, [(0, 0)] * (a.ndim - 1) + [(0, LANES - small.shape[-1])])
    order = ("dn_qkv", "dn_z", "lru_x", "lru_g", "fq", "fk", "fv", "fg", "mq", "mg", "mgate")
    return jnp.concatenate([parts[n] for n in order] + [small], axis=-1)


def _block_diag(w):
    nblk, e, f = w.shape
    eye = jnp.eye(nblk, dtype=w.dtype)
    return (eye[:, None, :, None] * w[:, :, None, :]).reshape(nblk * e, nblk * f)


def _decay_lane_row(v):
    return jnp.zeros((1, LANES), F32).at[0, SM_DECAY:SM_DECAY + N_HEADS].set(v)


def kernel(x_prompt, x_sample, cache_fox_k, cache_fox_v, cache_fox_logf, cache_mem_k, cache_mem_v, state_dn, state_dn_conv, state_lru_h, state_lru_conv, page_table, mem_prompt, norm_w, w_in, b_in, dn_conv_w, dn_A_log, dn_dt_bias, dn_onorm_w, lru_conv_w, lru_conv_b, lru_w_r, lru_b_r, lru_w_i, lru_b_i, lru_lambda, fox_qn_w, fox_kn_w, mem_norm_w, w_mem_kv, mem_qn_w, mem_kn_w, w_branch, w_out):
    bp, s, d = x_prompt.shape
    bd = x_sample.shape[0]
    depth = w_in.shape[0]
    n_pool, page = cache_fox_k.shape[1], cache_fox_k.shape[2]
    mem_tokens = mem_prompt.shape[1]
    tp = bp * s

    yp = x_prompt.reshape(tp, d)
    ys = x_sample.reshape(bd, d)
    mem2 = mem_prompt.reshape(bp * mem_tokens, d)
    cache_k2 = cache_fox_k.reshape(depth * n_pool, page, BR_WIDTH)
    cache_v2 = cache_fox_v.reshape(depth * n_pool, page, BR_WIDTH)
    cache_lf_t = jnp.swapaxes(cache_fox_logf, 2, 3).reshape(depth * n_pool, N_HEADS, page)
    cmem_k2 = cache_mem_k.reshape(depth * bd, mem_tokens, BR_WIDTH)
    cmem_v2 = cache_mem_v.reshape(depth * bd, mem_tokens, BR_WIDTH)
    pt_flat = page_table.reshape(-1)

    row = lambda v: v.reshape(1, -1)
    acc = {n: [] for n in ("pk", "pv", "plf", "pmk", "pmv", "pdn", "pdc", "plh", "plc",
                           "sk", "sv", "slf", "sdn", "sdc", "slh", "slc")}
    for l in range(depth):
        w_p = _permute_columns(w_in[l]).astype(BF16)
        b_p = _permute_columns(b_in[l]).reshape(1, N_PERM)
        nw = row(norm_w[l])
        al_row = _decay_lane_row(dn_A_log[l])
        dt_row = _decay_lane_row(dn_dt_bias[l])
        ow = row(dn_onorm_w[l])
        wr = _block_diag(lru_w_r[l]).astype(BF16)
        wi = _block_diag(lru_w_i[l]).astype(BF16)
        lru_args = (lru_conv_w[l], row(lru_conv_b[l]), wr, row(lru_b_r[l]), wi, row(lru_b_i[l]), row(lru_lambda[l]))
        wb = w_branch[l].astype(BF16)
        wo = w_out[l].astype(BF16)
        fqw, fkw, mqw = row(fox_qn_w[l]), row(fox_kn_w[l]), row(mem_qn_w[l])

        proj = _inproj(yp, nw, w_p, b_p)
        qn, kn, fv, lf, c, ct = _fox_prep(proj, fqw, fkw, bp, s)
        out_c = _fox_flash(qn, kn, fv, c, ct, proj, bp, s)
        mk, mv = _mem_kv(mem2, row(mem_norm_w[l]), w_mem_kv[l].astype(BF16), row(mem_kn_w[l]))
        out_m = _mem_attn(proj, mk, mv, mqw, bp, s)
        out_a, dn_s = _dn_prompt(proj, dn_conv_w[l], al_row, dt_row, ow, bp, s)
        out_b, lru_h = _lru_prompt(proj, *lru_args, bp, s)
        proj3 = proj.reshape(bp, s, N_PERM)
        acc["pk"].append(kn.reshape(bp, s, N_HEADS, HEAD_DIM))
        acc["pv"].append(fv.reshape(bp, s, N_HEADS, HEAD_DIM))
        acc["plf"].append(lf.reshape(bp, s, N_HEADS))
        acc["pmk"].append(mk.reshape(bp, mem_tokens, N_HEADS, HEAD_DIM))
        acc["pmv"].append(mv.reshape(bp, mem_tokens, N_HEADS, HEAD_DIM))
        acc["pdn"].append(dn_s)
        acc["pdc"].append(proj3[:, s - (CONV_W - 1):, C_DNQKV:C_DNQKV + 3 * BR_WIDTH])
        acc["plh"].append(lru_h.reshape(bp, BR_WIDTH))
        acc["plc"].append(proj3[:, s - (CONV_W - 1):, C_LRUX:C_LRUX + BR_WIDTH])
        yp = _merge(yp, out_a, out_b, out_c, out_m, proj, wb, wo)

        proj_s = _inproj(ys, nw, w_p, b_p)
        qn_s, kn_s, fv_s, lf_s, _, _ = _fox_prep(proj_s, fqw, fkw, 1, bd)
        lf_rows = jnp.pad(jnp.broadcast_to(lf_s[:, :, None], (bd, N_HEADS, LANES)),
                          ((0, 0), (0, SUBLANES - N_HEADS), (0, 0)))
        as_rows = lambda a: a.reshape(bd, 1, BR_WIDTH)
        fg_s = proj_s[:, C_FG:C_FG + BR_WIDTH]
        out_c_s = _fox_decode(as_rows(qn_s), as_rows(kn_s), as_rows(fv_s), lf_rows, as_rows(fg_s),
                              cache_k2, cache_v2, cache_lf_t, pt_flat, l, n_pool).reshape(bd, BR_WIDTH)
        out_m_s = _mem_decode(proj_s, cmem_k2, cmem_v2, mqw, l)
        out_a_s, out_b_s, dn_s_s, dn_c_s, lru_h_s, lru_c_s = _sample_step(
            proj_s, l, state_dn, state_dn_conv, state_lru_h, state_lru_conv,
            dn_conv_w[l], al_row, dt_row, ow, *lru_args)
        acc["sk"].append(kn_s.reshape(bd, 1, N_HEADS, HEAD_DIM))
        acc["sv"].append(fv_s.reshape(bd, 1, N_HEADS, HEAD_DIM))
        acc["slf"].append(lf_s.reshape(bd, 1, N_HEADS))
        acc["sdn"].append(dn_s_s)
        acc["sdc"].append(dn_c_s)
        acc["slh"].append(lru_h_s)
        acc["slc"].append(lru_c_s)
        ys = _merge(ys, out_a_s, out_b_s, out_c_s, out_m_s, proj_s, wb, wo)

    st = lambda n: jnp.stack(acc[n])
    return (yp.reshape(bp, s, d), ys.reshape(bd, 1, d),
            st("pk"), st("pv"), st("plf"), st("pmk"), st("pmv"), st("pdn"), st("pdc"), st("plh"), st("plc"),
            st("sk"), st("sv"), st("slf"), st("sdn"), st("sdc"), st("slh"), st("slc"))
```

```python
import functools

import jax
import jax.numpy as jnp
from jax import lax
from jax.experimental import pallas as pl
from jax.experimental.pallas import tpu as pltpu

F32 = jnp.float32
BF16 = jnp.bfloat16
HIGHEST = lax.Precision.HIGHEST

EPS = 1e-6
LRU_C = 8.0
CONV_W = 4
N_HEADS = 4
HEAD_DIM = 128
BR_WIDTH = N_HEADS * HEAD_DIM
N_BRANCH = 4
LANES = 128
SUBLANES = 8
DN_CHUNK = 128
NEG_BIG = -1e30
VMEM_LIMIT_BYTES = 56 * 1024 * 1024

C_DNQKV = 0
C_DNZ = 1536
C_LRUX = 2048
C_LRUG = 2560
C_FQ = 3072
C_FK = 3584
C_FV = 4096
C_FG = 4608
C_MQ = 5120
C_MG = 5632
C_MGATE = 6144
C_SMALL = 10240
N_PERM = 10368
SM_BETA = 0
SM_DECAY = 4
SM_FORGET = 8

NT_DIMS = (((1,), (1,)), ((), ()))


def _params(*sem):
    return pltpu.CompilerParams(dimension_semantics=sem, vmem_limit_bytes=VMEM_LIMIT_BYTES)


def _rms(x, w):
    return x * lax.rsqrt(jnp.mean(x * x, axis=-1, keepdims=True) + EPS) * w


def _l2norm(x):
    return x * lax.rsqrt(jnp.sum(x * x, axis=-1, keepdims=True) + EPS)


def _sigmoid(x):
    return 1.0 / (1.0 + jnp.exp(-x))


def _silu(x):
    return x * _sigmoid(x)


def _softplus(x):
    return jnp.maximum(x, 0.0) + jnp.log1p(jnp.exp(-jnp.abs(x)))


def _log_sigmoid(x):
    return -_softplus(-x)


def _dot(a, b):
    return jnp.dot(a.astype(BF16), b.astype(BF16), preferred_element_type=F32)


def _dot_nt(a, b):
    return lax.dot_general(a.astype(BF16), b.astype(BF16), NT_DIMS, preferred_element_type=F32)


def _dot_exact(a, b):
    return jnp.dot(a, b, precision=HIGHEST, preferred_element_type=F32)


def _iota2(shape, axis):
    return lax.broadcasted_iota(jnp.int32, shape, axis)


def _head(h):
    return slice(h * HEAD_DIM, (h + 1) * HEAD_DIM)


def _inproj_kernel(x_ref, nw_ref, w_ref, b_ref, o_ref, h_ref):
    @pl.when(pl.program_id(1) == 0)
    def _():
        h_ref[...] = _rms(x_ref[...], nw_ref[...]).astype(BF16)

    o_ref[...] = jnp.dot(h_ref[...], w_ref[...], preferred_element_type=F32) + b_ref[...]


def _inproj(x, norm_w, w, b):
    t, d = x.shape
    n = w.shape[1]
    tm = min(t, 1024)
    tn = 1152
    return pl.pallas_call(
        _inproj_kernel,
        out_shape=jax.ShapeDtypeStruct((t, n), F32),
        grid=(t // tm, n // tn),
        in_specs=[pl.BlockSpec((tm, d), lambda i, j: (i, 0)),
                  pl.BlockSpec((1, d), lambda i, j: (0, 0)),
                  pl.BlockSpec((d, tn), lambda i, j: (0, j)),
                  pl.BlockSpec((1, tn), lambda i, j: (0, j))],
        out_specs=pl.BlockSpec((tm, tn), lambda i, j: (i, j)),
        scratch_shapes=[pltpu.VMEM((tm, d), BF16)],
        compiler_params=_params("parallel", "arbitrary"),
        name="inproj",
    )(x, norm_w, w, b)


def _fox_prep_kernel(q_ref, k_ref, v_ref, sm_ref, qw_ref, kw_ref,
                     qn_ref, kn_ref, vo_ref, kb_ref, vb_ref, lf_ref, c_ref, ct_ref, carry_ref, *, tm):
    @pl.when(pl.program_id(1) == 0)
    def _():
        carry_ref[...] = jnp.zeros_like(carry_ref)

    scale = HEAD_DIM ** -0.5
    for h in range(N_HEADS):
        qn_ref[:, _head(h)] = (_rms(q_ref[:, _head(h)], qw_ref[...]) * scale).astype(qn_ref.dtype)
        kn = _rms(k_ref[:, _head(h)], kw_ref[...])
        kn_ref[:, _head(h)] = kn
        kb_ref[:, _head(h)] = kn.astype(BF16)
    v = v_ref[...]
    vo_ref[...] = v
    vb_ref[...] = v.astype(BF16)
    lf = _log_sigmoid(sm_ref[...])
    lf_ref[...] = lf[:, SM_FORGET:SM_FORGET + N_HEADS]
    tri = (_iota2((tm, tm), 0) >= _iota2((tm, tm), 1)).astype(F32)
    c = _dot_exact(tri, lf) + carry_ref[...]
    c_ref[...] = c
    ct_ref[...] = c.T
    carry_ref[...] = c[tm - 1:tm, :]


def _fox_prep(proj, qn_w, kn_w, nb, s, q_dtype):
    t = proj.shape[0]
    tm = min(s, 256)
    nj = s // tm
    row = lambda b, j: b * nj + j
    wide = lambda off: pl.BlockSpec((tm, BR_WIDTH), lambda b, j: (row(b, j), off // BR_WIDTH))
    out_rows = pl.BlockSpec((tm, BR_WIDTH), lambda b, j: (row(b, j), 0))
    wide_out = lambda dt: jax.ShapeDtypeStruct((t, BR_WIDTH), dt)
    return pl.pallas_call(
        functools.partial(_fox_prep_kernel, tm=tm),
        out_shape=(wide_out(q_dtype), wide_out(F32), wide_out(F32), wide_out(BF16), wide_out(BF16),
                   jax.ShapeDtypeStruct((t, N_HEADS), F32),
                   jax.ShapeDtypeStruct((t, LANES), F32),
                   jax.ShapeDtypeStruct((nb, LANES, s), F32)),
        grid=(nb, nj),
        in_specs=[wide(C_FQ), wide(C_FK), wide(C_FV),
                  pl.BlockSpec((tm, LANES), lambda b, j: (row(b, j), C_SMALL // LANES)),
                  pl.BlockSpec((1, HEAD_DIM), lambda b, j: (0, 0)),
                  pl.BlockSpec((1, HEAD_DIM), lambda b, j: (0, 0))],
        out_specs=(out_rows, out_rows, out_rows, out_rows, out_rows,
                   pl.BlockSpec((tm, N_HEADS), lambda b, j: (row(b, j), 0)),
                   pl.BlockSpec((tm, LANES), lambda b, j: (row(b, j), 0)),
                   pl.BlockSpec((None, LANES, tm), lambda b, j: (b, 0, j))),
        scratch_shapes=[pltpu.VMEM((1, LANES), F32)],
        compiler_params=_params("parallel", "arbitrary"),
        name="fox_prep",
    )(proj, proj, proj, proj, qn_w, kn_w)


def _fox_flash_kernel(q_ref, k_ref, v_ref, cq_ref, ck_ref, g_ref, o_ref, m_ref, l_ref, acc_ref, cqb_ref, *, tq):
    i = pl.program_id(1)
    j = pl.program_id(2)
    lane_tiles = tq // LANES

    @pl.when(j == 0)
    def _():
        m_ref[...] = jnp.full_like(m_ref, NEG_BIG)
        l_ref[...] = jnp.zeros_like(l_ref)
        acc_ref[...] = jnp.zeros_like(acc_ref)
        for h in range(N_HEADS):
            cqb_ref[h] = jnp.broadcast_to(cq_ref[:, SM_FORGET + h:SM_FORGET + h + 1], (tq, LANES))

    def update(diagonal):
        for h in range(N_HEADS):
            s = _dot_nt(q_ref[:, _head(h)], k_ref[:, _head(h)]) - ck_ref[h:h + 1, :]
            if diagonal:
                s = jnp.where(_iota2((tq, tq), 0) >= _iota2((tq, tq), 1), s, NEG_BIG)
            cqb = cqb_ref[h]
            m_prev = m_ref[h]
            m_new = jnp.maximum(m_prev, jnp.max(s, axis=-1, keepdims=True) + cqb)
            shift = m_new - cqb
            p = jnp.exp(s - jnp.concatenate([shift] * lane_tiles, axis=1))
            alpha = jnp.exp(m_prev - m_new)
            l_ref[h] = alpha * l_ref[h] + jnp.sum(p, axis=-1, keepdims=True)
            acc_ref[h] = alpha * acc_ref[h] + _dot(p, v_ref[:, _head(h)])
            m_ref[h] = m_new

    @pl.when(j < i)
    def _():
        update(False)

    @pl.when(j == i)
    def _():
        update(True)
        for h in range(N_HEADS):
            o = acc_ref[h] / l_ref[h]
            o_ref[:, _head(h)] = (o * _silu(g_ref[:, _head(h)])).astype(BF16)


def _fox_flash(qn, kn, v, c, ct, proj, nb, s):
    t = qn.shape[0]
    tq = min(s, 512)
    nq = s // tq
    qrow = lambda b, i, j: (b * nq + i, 0)
    krow = lambda b, i, j: (b * nq + jnp.minimum(i, j), 0)
    stat = pltpu.VMEM((N_HEADS, tq, LANES), F32)
    return pl.pallas_call(
        functools.partial(_fox_flash_kernel, tq=tq),
        out_shape=jax.ShapeDtypeStruct((t, BR_WIDTH), BF16),
        grid=(nb, nq, nq),
        in_specs=[pl.BlockSpec((tq, BR_WIDTH), qrow),
                  pl.BlockSpec((tq, BR_WIDTH), krow),
                  pl.BlockSpec((tq, BR_WIDTH), krow),
                  pl.BlockSpec((tq, LANES), qrow),
                  pl.BlockSpec((None, SUBLANES, tq), lambda b, i, j: (b, SM_FORGET // SUBLANES, jnp.minimum(i, j))),
                  pl.BlockSpec((tq, BR_WIDTH), lambda b, i, j: (b * nq + i, C_FG // BR_WIDTH))],
        out_specs=pl.BlockSpec((tq, BR_WIDTH), qrow),
        scratch_shapes=[stat, stat, stat, stat],
        compiler_params=_params("parallel", "parallel", "arbitrary"),
        name="fox_flash",
    )(qn, kn, v, c, ct, proj)


def _mem_kv_kernel(m_ref, nw_ref, w_ref, kw_ref, mk_ref, mv_ref):
    kv = _dot(_rms(m_ref[...], nw_ref[...]), w_ref[...])
    for h in range(N_HEADS):
        mk_ref[:, _head(h)] = _rms(kv[:, _head(h)], kw_ref[...])
    mv_ref[...] = kv[:, BR_WIDTH:]


def _mem_kv(mem, norm_w, w_kv, kn_w):
    t, d = mem.shape
    tm = min(t, 256)
    return pl.pallas_call(
        _mem_kv_kernel,
        out_shape=(jax.ShapeDtypeStruct((t, BR_WIDTH), F32), jax.ShapeDtypeStruct((t, BR_WIDTH), F32)),
        grid=(t // tm,),
        in_specs=[pl.BlockSpec((tm, d), lambda i: (i, 0)),
                  pl.BlockSpec((1, d), lambda i: (0, 0)),
                  pl.BlockSpec((d, 2 * BR_WIDTH), lambda i: (0, 0)),
                  pl.BlockSpec((1, HEAD_DIM), lambda i: (0, 0))],
        out_specs=(pl.BlockSpec((tm, BR_WIDTH), lambda i: (i, 0)),
                   pl.BlockSpec((tm, BR_WIDTH), lambda i: (i, 0))),
        compiler_params=_params("parallel"),
        name="mem_kv",
    )(mem, norm_w, w_kv, kn_w)


def _mem_attn_kernel(q_ref, g_ref, k_ref, v_ref, qw_ref, o_ref):
    scale = HEAD_DIM ** -0.5
    for h in range(N_HEADS):
        q = _rms(q_ref[:, _head(h)], qw_ref[...]) * scale
        s = _dot_nt(q, k_ref[:, _head(h)])
        p = jnp.exp(s - jnp.max(s, axis=-1, keepdims=True))
        o = _dot(p, v_ref[:, _head(h)]) / jnp.sum(p, axis=-1, keepdims=True)
        o_ref[:, _head(h)] = (o * _silu(g_ref[:, _head(h)])).astype(BF16)


def _mem_attn(proj, mk, mv, qn_w, nb, s):
    t = proj.shape[0]
    m = mk.shape[0] // nb
    tq = min(s, 512)
    nq = s // tq
    return pl.pallas_call(
        _mem_attn_kernel,
        out_shape=jax.ShapeDtypeStruct((t, BR_WIDTH), BF16),
        grid=(nb, nq),
        in_specs=[pl.BlockSpec((tq, BR_WIDTH), lambda b, i: (b * nq + i, C_MQ // BR_WIDTH)),
                  pl.BlockSpec((tq, BR_WIDTH), lambda b, i: (b * nq + i, C_MG // BR_WIDTH)),
                  pl.BlockSpec((m, BR_WIDTH), lambda b, i: (b, 0)),
                  pl.BlockSpec((m, BR_WIDTH), lambda b, i: (b, 0)),
                  pl.BlockSpec((1, HEAD_DIM), lambda b, i: (0, 0))],
        out_specs=pl.BlockSpec((tq, BR_WIDTH), lambda b, i: (b * nq + i, 0)),
        compiler_params=_params("parallel", "parallel"),
        name="mem_attn",
    )(proj, proj, mk, mv, qn_w)


def _causal_conv_tile(x, halo, w_ref):
    rows8 = _iota2(halo.shape, 0)
    y = x * w_ref[CONV_W - 1:CONV_W, :]
    for k in range(1, CONV_W):
        xr = pltpu.roll(x, k, axis=0)
        hr = pltpu.roll(halo, k, axis=0)
        top = jnp.where(rows8 < k, hr, xr[:SUBLANES])
        xs = jnp.concatenate([top, xr[SUBLANES:]], axis=0)
        y = y + xs * w_ref[CONV_W - 1 - k:CONV_W - k, :]
    return y


def _unit_lower_inverses(mats, row, col):
    n = mats[0].shape[0]

    def off_block(shift):
        return (((row >> (shift + 1)) == (col >> (shift + 1)))
                & (((row >> shift) & 1) == 1) & (((col >> shift) & 1) == 0))

    eye = (row == col).astype(F32)
    first = off_block(0)
    ds = [eye - jnp.where(first, a, 0.0) for a in mats]
    shift = 1
    while (1 << shift) < n:
        mask = off_block(shift)
        ts = [_dot(jnp.where(mask, a, 0.0), d) for a, d in zip(mats, ds)]
        ds = [d - _dot(d, t) for d, t in zip(ds, ts)]
        shift += 1
    return ds


def _dn_kernel(x_ref, z_ref, sm_ref, cw_ref, al_ref, dt_ref, ow_ref,
               o_ref, st_ref, stt_ref, halo_ref, *, tl):
    j = pl.program_id(1)

    @pl.when(j == 0)
    def _():
        stt_ref[...] = jnp.zeros_like(stt_ref)
        halo_ref[...] = jnp.zeros_like(halo_ref)

    x = x_ref[...]
    qkv = _silu(_causal_conv_tile(x, halo_ref[...], cw_ref))
    halo_ref[...] = x[tl - SUBLANES:, :]
    sm = sm_ref[...]
    beta_all = _sigmoid(sm)
    la_all = -jnp.exp(al_ref[...]) * _softplus(sm + dt_ref[...])

    cc = DN_CHUNK
    row = _iota2((cc, cc), 0)
    col = _iota2((cc, cc), 1)
    incl = row >= col
    strict = row > col
    tri = incl.astype(F32)
    kd = N_HEADS * HEAD_DIM

    systems = []
    for c in range(tl // cc):
        rows = slice(c * cc, (c + 1) * cc)
        g_all = _dot_exact(tri, la_all[rows])
        gt_all = g_all.T
        eg_all = jnp.exp(g_all)
        for h in range(N_HEADS):
            q = _l2norm(qkv[rows, h * HEAD_DIM:(h + 1) * HEAD_DIM]) * (HEAD_DIM ** -0.5)
            k = _l2norm(qkv[rows, kd + h * HEAD_DIM:kd + (h + 1) * HEAD_DIM])
            v = qkv[rows, 2 * kd + h * HEAD_DIM:2 * kd + (h + 1) * HEAD_DIM]
            beta = beta_all[rows, SM_BETA + h:SM_BETA + h + 1]
            gc = g_all[:, SM_DECAY + h:SM_DECAY + h + 1]
            gr = gt_all[SM_DECAY + h:SM_DECAY + h + 1, :]
            eg = eg_all[:, SM_DECAY + h:SM_DECAY + h + 1]
            g_last = g_all[cc - 1:cc, SM_DECAY + h:SM_DECAY + h + 1]
            decay = jnp.exp(jnp.where(incl, gc - gr, NEG_BIG))
            kb = k.astype(BF16)
            qb = q.astype(BF16)
            systems.append(dict(
                a=jnp.where(strict, beta * decay * _dot_nt(kb, kb), 0.0),
                qk=(decay * _dot_nt(qb, kb)).astype(BF16),
                rhs=jnp.concatenate([beta * v, (beta * eg) * k], axis=1).astype(BF16),
                kdec_t=(jnp.exp(g_last - gc) * k).T.astype(BF16),
                q=qb, eg=eg, g_end=jnp.exp(g_last)))

    ds = _unit_lower_inverses([sy["a"] for sy in systems], row, col)
    sols = [_dot(d, sy["rhs"]) for d, sy in zip(ds, systems)]

    heads = range(N_HEADS)
    stts = [stt_ref[h] for h in heads]
    for c in range(tl // cc):
        rows = slice(c * cc, (c + 1) * cc)
        sys_c = systems[c * N_HEADS:(c + 1) * N_HEADS]
        sol_c = sols[c * N_HEADS:(c + 1) * N_HEADS]
        us = [sol_c[h][:, :HEAD_DIM] - _dot(sol_c[h][:, HEAD_DIM:], stts[h]) for h in heads]
        os = [sys_c[h]["eg"] * _dot(sys_c[h]["q"], stts[h]) + _dot(sys_c[h]["qk"], us[h]) for h in heads]
        stts = [sys_c[h]["g_end"] * stts[h] + _dot(sys_c[h]["kdec_t"], us[h]) for h in heads]
        for h in heads:
            o = _rms(os[h], ow_ref[...]) * _silu(z_ref[rows, _head(h)])
            o_ref[rows, _head(h)] = o.astype(BF16)
    for h in heads:
        stt_ref[h] = stts[h]

    @pl.when(j == pl.num_programs(1) - 1)
    def _():
        for h in range(N_HEADS):
            st_ref[h] = stt_ref[h].T


def _dn_prompt(proj, conv_w, al_row, dt_row, onorm_w, nb, s):
    t = proj.shape[0]
    tl = min(s, 512)
    nj = s // tl
    qkv_w = 3 * BR_WIDTH
    row = lambda b, j: b * nj + j
    return pl.pallas_call(
        functools.partial(_dn_kernel, tl=tl),
        out_shape=(jax.ShapeDtypeStruct((t, BR_WIDTH), BF16),
                   jax.ShapeDtypeStruct((nb, N_HEADS, HEAD_DIM, HEAD_DIM), F32)),
        grid=(nb, nj),
        in_specs=[pl.BlockSpec((tl, qkv_w), lambda b, j: (row(b, j), 0)),
                  pl.BlockSpec((tl, BR_WIDTH), lambda b, j: (row(b, j), C_DNZ // BR_WIDTH)),
                  pl.BlockSpec((tl, LANES), lambda b, j: (row(b, j), C_SMALL // LANES)),
                  pl.BlockSpec((CONV_W, qkv_w), lambda b, j: (0, 0)),
                  pl.BlockSpec((1, LANES), lambda b, j: (0, 0)),
                  pl.BlockSpec((1, LANES), lambda b, j: (0, 0)),
                  pl.BlockSpec((1, HEAD_DIM), lambda b, j: (0, 0))],
        out_specs=(pl.BlockSpec((tl, BR_WIDTH), lambda b, j: (row(b, j), 0)),
                   pl.BlockSpec((None, N_HEADS, HEAD_DIM, HEAD_DIM), lambda b, j: (b, 0, 0, 0))),
        scratch_shapes=[pltpu.VMEM((N_HEADS, HEAD_DIM, HEAD_DIM), F32),
                        pltpu.VMEM((SUBLANES, qkv_w), F32)],
        compiler_params=_params("parallel", "arbitrary"),
        name="dn_prompt",
    )(proj, proj, proj, conv_w, al_row, dt_row, onorm_w)


def _lru_gates(xc, wr_ref, br_ref, wi_ref, bi_ref, lam_ref):
    r = _sigmoid(_dot(xc, wr_ref[...]) + br_ref[...])
    i = _sigmoid(_dot(xc, wi_ref[...]) + bi_ref[...])
    log_a = -LRU_C * r * _softplus(-lam_ref[...])
    a = jnp.exp(log_a)
    one_minus_a2 = -jnp.tanh(log_a) * (a * a + 1.0)
    return a, jnp.sqrt(one_minus_a2) * (i * xc)


def _lru_kernel(x_ref, g_ref, cw_ref, cb_ref, wr_ref, br_ref, wi_ref, bi_ref, lam_ref,
                o_ref, hl_ref, hc_ref, halo_ref, *, tl):
    @pl.when(pl.program_id(1) == 0)
    def _():
        hc_ref[...] = jnp.zeros_like(hc_ref)
        halo_ref[...] = jnp.zeros_like(halo_ref)

    x = x_ref[...]
    xc = _causal_conv_tile(x, halo_ref[...], cw_ref) + cb_ref[...]
    halo_ref[...] = x[tl - SUBLANES:, :]
    a, b = _lru_gates(xc, wr_ref, br_ref, wi_ref, bi_ref, lam_ref)
    rows = _iota2(a.shape, 0)
    d = 1
    while d < tl:
        a_up = jnp.where(rows >= d, pltpu.roll(a, d, axis=0), 1.0)
        b_up = jnp.where(rows >= d, pltpu.roll(b, d, axis=0), 0.0)
        b = a * b_up + b
        a = a * a_up
        d *= 2
    hseq = a * hc_ref[...] + b
    hc_ref[...] = hseq[tl - 1:tl, :]
    hl_ref[...] = hseq[tl - 1:tl, :]
    o_ref[...] = (hseq * _silu(g_ref[...])).astype(BF16)


def _lru_prompt(proj, conv_w, conv_b, w_r, b_r, w_i, b_i, lam, nb, s):
    t = proj.shape[0]
    tl = min(s, 256)
    nj = s // tl
    row = lambda b, j: b * nj + j
    vec = pl.BlockSpec((1, BR_WIDTH), lambda b, j: (0, 0))
    mat = pl.BlockSpec((BR_WIDTH, BR_WIDTH), lambda b, j: (0, 0))
    return pl.pallas_call(
        functools.partial(_lru_kernel, tl=tl),
        out_shape=(jax.ShapeDtypeStruct((t, BR_WIDTH), BF16),
                   jax.ShapeDtypeStruct((nb, 1, BR_WIDTH), F32)),
        grid=(nb, nj),
        in_specs=[pl.BlockSpec((tl, BR_WIDTH), lambda b, j: (row(b, j), C_LRUX // BR_WIDTH)),
                  pl.BlockSpec((tl, BR_WIDTH), lambda b, j: (row(b, j), C_LRUG // BR_WIDTH)),
                  pl.BlockSpec((CONV_W, BR_WIDTH), lambda b, j: (0, 0)),
                  vec, mat, vec, mat, vec, vec],
        out_specs=(pl.BlockSpec((tl, BR_WIDTH), lambda b, j: (row(b, j), 0)),
                   pl.BlockSpec((None, 1, BR_WIDTH), lambda b, j: (b, 0, 0))),
        scratch_shapes=[pltpu.VMEM((1, BR_WIDTH), F32), pltpu.VMEM((SUBLANES, BR_WIDTH), F32)],
        compiler_params=_params("parallel", "arbitrary"),
        name="lru_prompt",
    )(proj, proj, conv_w, conv_b, w_r, b_r, w_i, b_i, lam)


def _merge_kernel(x_ref, a_ref, b_ref, c_ref, m_ref, g0_ref, g1_ref, g2_ref, g3_ref, wb_ref, wo_ref, y_ref):
    merged = None
    for n, (br, gate) in enumerate(((a_ref, g0_ref), (b_ref, g1_ref), (c_ref, g2_ref), (m_ref, g3_ref))):
        up = jnp.dot(br[...], wb_ref[n], preferred_element_type=F32)
        term = _sigmoid(gate[...]) * up
        merged = term if merged is None else merged + term
    y_ref[...] = x_ref[...] + jnp.dot(merged.astype(BF16), wo_ref[...], preferred_element_type=F32)


def _merge(x, out_a, out_b, out_c, out_m, proj, w_branch, w_out):
    t, d = x.shape
    tm = min(t, 512)
    rows = lambda w: pl.BlockSpec((tm, w), lambda i: (i, 0))
    gate = lambda n: pl.BlockSpec((tm, d), lambda i: (i, C_MGATE // d + n))
    return pl.pallas_call(
        _merge_kernel,
        out_shape=jax.ShapeDtypeStruct((t, d), F32),
        grid=(t // tm,),
        in_specs=[rows(d), rows(BR_WIDTH), rows(BR_WIDTH), rows(BR_WIDTH), rows(BR_WIDTH),
                  gate(0), gate(1), gate(2), gate(3),
                  pl.BlockSpec((N_BRANCH, BR_WIDTH, d), lambda i: (0, 0, 0)),
                  pl.BlockSpec((d, d), lambda i: (0, 0))],
        out_specs=rows(d),
        compiler_params=_params("parallel"),
        name="merge",
    )(x, out_a, out_b, out_c, out_m, proj, proj, proj, proj, w_branch, w_out)


def _sample_step_kernel(qkv_ref, z_ref, lx_ref, lg_ref, sm_ref, st_ref, dbuf_ref, lh_ref, lbuf_ref,
                        dcw_ref, al_ref, dt_ref, ow_ref, lcw_ref, lcb_ref, wr_ref, br_ref, wi_ref, bi_ref, lam_ref,
                        oa_ref, ob_ref, sto_ref, dbo_ref, lho_ref, lbo_ref, o_scr, *, bs):
    kd = N_HEADS * HEAD_DIM

    def step_conv(x, buf_ref, bufo_ref, w_ref):
        y = x * w_ref[CONV_W - 1:CONV_W, :]
        for k in range(CONV_W - 1):
            y = y + buf_ref[:, k, :] * w_ref[k:k + 1, :]
        for k in range(CONV_W - 2):
            bufo_ref[:, k, :] = buf_ref[:, k + 1, :]
        bufo_ref[:, CONV_W - 2, :] = x
        return y

    qkv = _silu(step_conv(qkv_ref[...], dbuf_ref, dbo_ref, dcw_ref))
    sm = sm_ref[...]
    beta_all = _sigmoid(sm)
    eg_all = jnp.exp(-jnp.exp(al_ref[...]) * _softplus(sm + dt_ref[...]))
    eye = _iota2((HEAD_DIM, HEAD_DIM), 0) == _iota2((HEAD_DIM, HEAD_DIM), 1)
    pad = jnp.zeros((SUBLANES - 2, HEAD_DIM), F32)
    for h in range(N_HEADS):
        q = _l2norm(qkv[:, h * HEAD_DIM:(h + 1) * HEAD_DIM]) * (HEAD_DIM ** -0.5)
        k = _l2norm(qkv[:, kd + h * HEAD_DIM:kd + (h + 1) * HEAD_DIM])
        v = qkv[:, 2 * kd + h * HEAD_DIM:2 * kd + (h + 1) * HEAD_DIM]
        beta = beta_all[:, SM_BETA + h:SM_BETA + h + 1]
        eg = eg_all[:, SM_DECAY + h:SM_DECAY + h + 1]
        qk = jnp.sum(q * k, axis=-1, keepdims=True)
        for b in range(bs):
            st = st_ref[b, h]
            kq = jnp.concatenate([k[b:b + 1], q[b:b + 1], pad], axis=0)
            sk_sq = _dot_nt(kq, st)
            eg_b = eg[b:b + 1]
            u = beta[b:b + 1] * (v[b:b + 1] - eg_b * sk_sq[0:1])
            o_scr[b:b + 1, _head(h)] = eg_b * sk_sq[1:2] + qk[b:b + 1] * u
            u_diag = jnp.where(eye, jnp.broadcast_to(u, (HEAD_DIM, HEAD_DIM)), 0.0)
            k_rows = jnp.broadcast_to(k[b:b + 1], (HEAD_DIM, HEAD_DIM))
            sto_ref[b, h] = eg_b * st + _dot(u_diag, k_rows)
    o = o_scr[...]
    for h in range(N_HEADS):
        oa_ref[:, _head(h)] = (_rms(o[:, _head(h)], ow_ref[...]) * _silu(z_ref[:, _head(h)])).astype(BF16)

    xc = step_conv(lx_ref[...], lbuf_ref, lbo_ref, lcw_ref) + lcb_ref[...]
    a, bx = _lru_gates(xc, wr_ref, br_ref, wi_ref, bi_ref, lam_ref)
    hnew = a * lh_ref[...] + bx
    lho_ref[...] = hnew
    ob_ref[...] = (hnew * _silu(lg_ref[...])).astype(BF16)


def _sample_step(proj, layer, state_dn, state_dn_conv, state_lru_h, state_lru_conv,
                 dn_conv_w, al_row, dt_row, onorm_w, lru_conv_w, lru_conv_b, w_r, b_r, w_i, b_i, lam):
    nb = proj.shape[0]
    bs = SUBLANES
    qkv_w = 3 * BR_WIDTH
    wide = lambda off: pl.BlockSpec((bs, BR_WIDTH), lambda i: (i, off // BR_WIDTH))
    const = lambda shape: pl.BlockSpec(shape, lambda i: (0,) * len(shape))
    st_in = pl.BlockSpec((None, bs, N_HEADS, HEAD_DIM, HEAD_DIM), lambda i: (layer, i, 0, 0, 0))
    dbuf_in = pl.BlockSpec((None, bs, CONV_W - 1, qkv_w), lambda i: (layer, i, 0, 0))
    lh_in = pl.BlockSpec((None, bs, BR_WIDTH), lambda i: (layer, i, 0))
    lbuf_in = pl.BlockSpec((None, bs, CONV_W - 1, BR_WIDTH), lambda i: (layer, i, 0, 0))
    return pl.pallas_call(
        functools.partial(_sample_step_kernel, bs=bs),
        out_shape=(jax.ShapeDtypeStruct((nb, BR_WIDTH), BF16),
                   jax.ShapeDtypeStruct((nb, BR_WIDTH), BF16),
                   jax.ShapeDtypeStruct((nb, N_HEADS, HEAD_DIM, HEAD_DIM), F32),
                   jax.ShapeDtypeStruct((nb, CONV_W - 1, qkv_w), F32),
                   jax.ShapeDtypeStruct((nb, BR_WIDTH), F32),
                   jax.ShapeDtypeStruct((nb, CONV_W - 1, BR_WIDTH), F32)),
        grid=(nb // bs,),
        in_specs=[pl.BlockSpec((bs, qkv_w), lambda i: (i, 0)), wide(C_DNZ), wide(C_LRUX), wide(C_LRUG),
                  pl.BlockSpec((bs, LANES), lambda i: (i, C_SMALL // LANES)),
                  st_in, dbuf_in, lh_in, lbuf_in,
                  const((CONV_W, qkv_w)), const((1, LANES)), const((1, LANES)), const((1, HEAD_DIM)),
                  const((CONV_W, BR_WIDTH)), const((1, BR_WIDTH)),
                  const((BR_WIDTH, BR_WIDTH)), const((1, BR_WIDTH)),
                  const((BR_WIDTH, BR_WIDTH)), const((1, BR_WIDTH)), const((1, BR_WIDTH))],
        out_specs=(pl.BlockSpec((bs, BR_WIDTH), lambda i: (i, 0)),
                   pl.BlockSpec((bs, BR_WIDTH), lambda i: (i, 0)),
                   pl.BlockSpec((bs, N_HEADS, HEAD_DIM, HEAD_DIM), lambda i: (i, 0, 0, 0)),
                   pl.BlockSpec((bs, CONV_W - 1, qkv_w), lambda i: (i, 0, 0)),
                   pl.BlockSpec((bs, BR_WIDTH), lambda i: (i, 0)),
                   pl.BlockSpec((bs, CONV_W - 1, BR_WIDTH), lambda i: (i, 0, 0))),
        scratch_shapes=[pltpu.VMEM((bs, BR_WIDTH), F32)],
        compiler_params=_params("parallel"),
        name="sample_step",
    )(proj, proj, proj, proj, proj, state_dn, state_dn_conv, state_lru_h, state_lru_conv,
      dn_conv_w, al_row, dt_row, onorm_w, lru_conv_w, lru_conv_b, w_r, b_r, w_i, b_i, lam)


def _own_head_mask(n_rows_kv):
    shape = (SUBLANES, n_rows_kv)
    return (_iota2(shape, 1) % N_HEADS) == _iota2(shape, 0)


def _heads_to_row(o8):
    return jnp.concatenate([o8[h:h + 1] for h in range(N_HEADS)], axis=1)


def _split3(x):
    hi = x.astype(BF16)
    r1 = x - hi.astype(F32)
    mid = r1.astype(BF16)
    lo = (r1 - mid.astype(F32)).astype(BF16)
    return hi, mid, lo


def _pool_bias_kernel(lf_ref, after_ref, same_ref, inpage_ref, total_ref):
    pieces = _split3(lf_ref[...])

    def times(m):
        return sum(jnp.dot(p, m, preferred_element_type=F32) for p in pieces)

    inpage_ref[...] = times(after_ref[...])
    total_ref[...] = times(same_ref[...])


def _pool_bias(cache_lf):
    n, w = cache_lf.shape
    tp = min(n, 512)
    src = jnp.arange(w)[:, None]
    dst = jnp.arange(w)[None, :]
    same = (src % N_HEADS) == (dst % N_HEADS)
    after = (same & (src > dst)).astype(BF16)
    rows = pl.BlockSpec((tp, w), lambda i: (i, 0))
    const = pl.BlockSpec((w, w), lambda i: (0, 0))
    return pl.pallas_call(
        _pool_bias_kernel,
        out_shape=(jax.ShapeDtypeStruct((n, w), F32), jax.ShapeDtypeStruct((n, w), F32)),
        grid=(n // tp,),
        in_specs=[rows, const, const],
        out_specs=(rows, rows),
        compiler_params=_params("parallel"),
        name="pool_bias",
    )(cache_lf, after, same.astype(BF16))


def _fox_decode_kernel(pt_ref, q_ref, kn_ref, vn_ref, lfn_ref, g_ref, *rest, n_pages):
    k_refs = rest[:n_pages]
    v_refs = rest[n_pages:2 * n_pages]
    inpage_refs = rest[2 * n_pages:3 * n_pages]
    total_refs = rest[3 * n_pages:4 * n_pages]
    o_ref = rest[4 * n_pages]
    del pt_ref
    q8 = q_ref[...]
    qb = q8.astype(BF16)
    own = _own_head_mask(k_refs[0].shape[0])

    later = lfn_ref[...]
    scores = [None] * n_pages
    for p in reversed(range(n_pages)):
        s = _dot_nt(qb, k_refs[p][...]) + (inpage_refs[p][...] + later)
        scores[p] = jnp.where(own, s, NEG_BIG)
        later = later + total_refs[p][...]
    s_self = jnp.sum(q8 * kn_ref[...], axis=-1, keepdims=True)
    m = s_self
    for s in scores:
        m = jnp.maximum(m, jnp.max(s, axis=-1, keepdims=True))
    p_self = jnp.exp(s_self - m)
    l = p_self
    acc = p_self * vn_ref[...]
    for p in range(n_pages):
        pr = jnp.exp(scores[p] - m)
        l = l + jnp.sum(pr, axis=-1, keepdims=True)
        acc = acc + _dot(pr, v_refs[p][...])
    o_ref[...] = (_heads_to_row(acc / l) * _silu(g_ref[...])).astype(BF16)


def _fox_decode(q8, k8, v8, lf_new, fg, cache_k, cache_v, in_page, totals, page_table, layer, n_pool):
    nb = q8.shape[0]
    n_pages = page_table.shape[0] // nb
    kv_rows = cache_k.shape[1]
    base = layer * n_pool
    row = pl.BlockSpec((None, 1, BR_WIDTH), lambda b, pt: (b, 0, 0))
    heads = pl.BlockSpec((None, SUBLANES, HEAD_DIM), lambda b, pt: (b, 0, 0))

    def paged(shape, p):
        return pl.BlockSpec((None,) + shape, lambda b, pt: (base + pt[b * n_pages + p], 0, 0))

    in_specs = [heads, heads, heads, pl.BlockSpec((None, 1, kv_rows), lambda b, pt: (b, 0, 0)), row]
    in_specs += [paged((kv_rows, HEAD_DIM), p) for p in range(n_pages)]
    in_specs += [paged((kv_rows, HEAD_DIM), p) for p in range(n_pages)]
    in_specs += [paged((1, kv_rows), p) for p in range(n_pages)]
    in_specs += [paged((1, kv_rows), p) for p in range(n_pages)]
    grid_spec = pltpu.PrefetchScalarGridSpec(
        num_scalar_prefetch=1,
        grid=(nb,),
        in_specs=in_specs,
        out_specs=row)
    return pl.pallas_call(
        functools.partial(_fox_decode_kernel, n_pages=n_pages),
        out_shape=jax.ShapeDtypeStruct((nb, 1, BR_WIDTH), BF16),
        grid_spec=grid_spec,
        compiler_params=_params("parallel"),
        name="fox_decode",
    )(page_table, q8, k8, v8, lf_new, fg,
      *([cache_k] * n_pages), *([cache_v] * n_pages), *([in_page] * n_pages), *([totals] * n_pages))


def _mem_decode_kernel(q_ref, g_ref, k_ref, v_ref, qw_ref, o_ref, o_scr, *, bs):
    own = _own_head_mask(k_ref.shape[1])
    scale = HEAD_DIM ** -0.5
    qs = [_rms(q_ref[:, _head(h)], qw_ref[...]) * scale for h in range(N_HEADS)]
    pad = jnp.zeros((SUBLANES - N_HEADS, HEAD_DIM), F32)
    for b in range(bs):
        q8 = jnp.concatenate([q[b:b + 1] for q in qs] + [pad], axis=0)
        s = jnp.where(own, _dot_nt(q8, k_ref[b]), NEG_BIG)
        p = jnp.exp(s - jnp.max(s, axis=-1, keepdims=True))
        o = _dot(p, v_ref[b]) / jnp.sum(p, axis=-1, keepdims=True)
        o_scr[b:b + 1, :] = _heads_to_row(o)
    o_ref[...] = (o_scr[...] * _silu(g_ref[...])).astype(BF16)


def _mem_decode(proj, cache_k, cache_v, qn_w, layer):
    nb = proj.shape[0]
    bs = SUBLANES
    m = cache_k.shape[1]
    base = layer * (nb // bs)
    kv = pl.BlockSpec((bs, m, HEAD_DIM), lambda i: (base + i, 0, 0))
    return pl.pallas_call(
        functools.partial(_mem_decode_kernel, bs=bs),
        out_shape=jax.ShapeDtypeStruct((nb, BR_WIDTH), BF16),
        grid=(nb // bs,),
        in_specs=[pl.BlockSpec((bs, BR_WIDTH), lambda i: (i, C_MQ // BR_WIDTH)),
                  pl.BlockSpec((bs, BR_WIDTH), lambda i: (i, C_MG // BR_WIDTH)),
                  kv, kv, pl.BlockSpec((1, HEAD_DIM), lambda i: (0, 0))],
        out_specs=pl.BlockSpec((bs, BR_WIDTH), lambda i: (i, 0)),
        scratch_shapes=[pltpu.VMEM((bs, BR_WIDTH), F32)],
        compiler_params=_params("parallel"),
        name="mem_decode",
    )(proj, proj, cache_k, cache_v, qn_w)


def _permute_columns(a):
    sizes = (3 * BR_WIDTH, BR_WIDTH, N_HEADS, N_HEADS, BR_WIDTH, BR_WIDTH,
             BR_WIDTH, BR_WIDTH, BR_WIDTH, N_HEADS, BR_WIDTH, BR_WIDTH, BR_WIDTH)
    names = ("dn_qkv", "dn_z", "dn_b", "dn_a", "lru_x", "lru_g", "fq", "fk", "fv", "ff", "fg", "mq", "mg")
    parts, off = {}, 0
    for name, size in zip(names, sizes):
        parts[name] = a[..., off:off + size]
        off += size
    parts["mgate"] = a[..., off:]
    small = jnp.concatenate([parts["dn_b"], parts["dn_a"], parts["ff"]], axis=-1)
    small = jnp.pad(small, [(0, 0)] * (a.ndim - 1) + [(0, LANES - small.shape[-1])])
    order = ("dn_qkv", "dn_z", "lru_x", "lru_g", "fq", "fk", "fv", "fg", "mq", "mg", "mgate")
    return jnp.concatenate([parts[n] for n in order] + [small], axis=-1)


def _block_diag(w):
    nblk, e, f = w.shape
    eye = jnp.eye(nblk, dtype=w.dtype)
    return (eye[:, None, :, None] * w[:, :, None, :]).reshape(nblk * e, nblk * f)


def _decay_lane_row(v):
    return jnp.zeros((1, LANES), F32).at[0, SM_DECAY:SM_DECAY + N_HEADS].set(v)


def kernel(x_prompt, x_sample, cache_fox_k, cache_fox_v, cache_fox_logf, cache_mem_k, cache_mem_v, state_dn, state_dn_conv, state_lru_h, state_lru_conv, page_table, mem_prompt, norm_w, w_in, b_in, dn_conv_w, dn_A_log, dn_dt_bias, dn_onorm_w, lru_conv_w, lru_conv_b, lru_w_r, lru_b_r, lru_w_i, lru_b_i, lru_lambda, fox_qn_w, fox_kn_w, mem_norm_w, w_mem_kv, mem_qn_w, mem_kn_w, w_branch, w_out):
    bp, s, d = x_prompt.shape
    bd = x_sample.shape[0]
    depth = w_in.shape[0]
    n_pool, page = cache_fox_k.shape[1], cache_fox_k.shape[2]
    mem_tokens = mem_prompt.shape[1]
    tp = bp * s

    yp = x_prompt.reshape(tp, d)
    ys = x_sample.reshape(bd, d)
    mem2 = mem_prompt.reshape(bp * mem_tokens, d)
    kv_rows = page * N_HEADS
    cache_k2 = cache_fox_k.reshape(depth * n_pool, kv_rows, HEAD_DIM)
    cache_v2 = cache_fox_v.reshape(depth * n_pool, kv_rows, HEAD_DIM)
    in_page, totals = _pool_bias(cache_fox_logf.reshape(depth * n_pool, kv_rows))
    in_page = in_page.reshape(depth * n_pool, 1, kv_rows)
    totals = totals.reshape(depth * n_pool, 1, kv_rows)
    cmem_k2 = cache_mem_k.reshape(depth * bd, mem_tokens * N_HEADS, HEAD_DIM)
    cmem_v2 = cache_mem_v.reshape(depth * bd, mem_tokens * N_HEADS, HEAD_DIM)
    pt_flat = page_table.reshape(-1)

    row = lambda v: v.reshape(1, -1)
    acc = {n: [] for n in ("pk", "pv", "plf", "pmk", "pmv", "pdn", "pdc", "plh", "plc",
                           "sk", "sv", "slf", "sdn", "sdc", "slh", "slc")}
    for l in range(depth):
        w_p = _permute_columns(w_in[l]).astype(BF16)
        b_p = _permute_columns(b_in[l]).reshape(1, N_PERM)
        nw = row(norm_w[l])
        al_row = _decay_lane_row(dn_A_log[l])
        dt_row = _decay_lane_row(dn_dt_bias[l])
        ow = row(dn_onorm_w[l])
        wr = _block_diag(lru_w_r[l]).astype(BF16)
        wi = _block_diag(lru_w_i[l]).astype(BF16)
        lru_args = (lru_conv_w[l], row(lru_conv_b[l]), wr, row(lru_b_r[l]), wi, row(lru_b_i[l]), row(lru_lambda[l]))
        wb = w_branch[l].astype(BF16)
        wo = w_out[l].astype(BF16)
        fqw, fkw, mqw = row(fox_qn_w[l]), row(fox_kn_w[l]), row(mem_qn_w[l])

        proj = _inproj(yp, nw, w_p, b_p)
        qb, kn, fv, kb, vb, lf, c, ct = _fox_prep(proj, fqw, fkw, bp, s, BF16)
        out_c = _fox_flash(qb, kb, vb, c, ct, proj, bp, s)
        mk, mv = _mem_kv(mem2, row(mem_norm_w[l]), w_mem_kv[l].astype(BF16), row(mem_kn_w[l]))
        out_m = _mem_attn(proj, mk, mv, mqw, bp, s)
        out_a, dn_s = _dn_prompt(proj, dn_conv_w[l], al_row, dt_row, ow, bp, s)
        out_b, lru_h = _lru_prompt(proj, *lru_args, bp, s)
        proj3 = proj.reshape(bp, s, N_PERM)
        acc["pk"].append(kn.reshape(bp, s, N_HEADS, HEAD_DIM))
        acc["pv"].append(fv.reshape(bp, s, N_HEADS, HEAD_DIM))
        acc["plf"].append(lf.reshape(bp, s, N_HEADS))
        acc["pmk"].append(mk.reshape(bp, mem_tokens, N_HEADS, HEAD_DIM))
        acc["pmv"].append(mv.reshape(bp, mem_tokens, N_HEADS, HEAD_DIM))
        acc["pdn"].append(dn_s)
        acc["pdc"].append(proj3[:, s - (CONV_W - 1):, C_DNQKV:C_DNQKV + 3 * BR_WIDTH])
        acc["plh"].append(lru_h.reshape(bp, BR_WIDTH))
        acc["plc"].append(proj3[:, s - (CONV_W - 1):, C_LRUX:C_LRUX + BR_WIDTH])
        yp = _merge(yp, out_a, out_b, out_c, out_m, proj, wb, wo)

        proj_s = _inproj(ys, nw, w_p, b_p)
        qn_s, kn_s, fv_s, _, _, lf_s, _, _ = _fox_prep(proj_s, fqw, fkw, 1, bd, F32)
        head_rows = lambda a: jnp.pad(a.reshape(bd, N_HEADS, HEAD_DIM), ((0, 0), (0, SUBLANES - N_HEADS), (0, 0)))
        lf_new = jnp.tile(lf_s, (1, page)).reshape(bd, 1, kv_rows)
        fg_s = proj_s[:, C_FG:C_FG + BR_WIDTH].reshape(bd, 1, BR_WIDTH)
        out_c_s = _fox_decode(head_rows(qn_s), head_rows(kn_s), head_rows(fv_s), lf_new, fg_s,
                              cache_k2, cache_v2, in_page, totals, pt_flat, l, n_pool).reshape(bd, BR_WIDTH)
        out_m_s = _mem_decode(proj_s, cmem_k2, cmem_v2, mqw, l)
        out_a_s, out_b_s, dn_s_s, dn_c_s, lru_h_s, lru_c_s = _sample_step(
            proj_s, l, state_dn, state_dn_conv, state_lru_h, state_lru_conv,
            dn_conv_w[l], al_row, dt_row, ow, *lru_args)
        acc["sk"].append(kn_s.reshape(bd, 1, N_HEADS, HEAD_DIM))
        acc["sv"].append(fv_s.reshape(bd, 1, N_HEADS, HEAD_DIM))
        acc["slf"].append(lf_s.reshape(bd, 1, N_HEADS))
        acc["sdn"].append(dn_s_s)
        acc["sdc"].append(dn_c_s)
        acc["slh"].append(lru_h_s)
        acc["slc"].append(lru_c_s)
        ys = _merge(ys, out_a_s, out_b_s, out_c_s, out_m_s, proj_s, wb, wo)

    st = lambda n: jnp.stack(acc[n])
    return (yp.reshape(bp, s, d), ys.reshape(bd, 1, d),
            st("pk"), st("pv"), st("plf"), st("pmk"), st("pmv"), st("pdn"), st("pdc"), st("plh"), st("plc"),
            st("sk"), st("sv"), st("slf"), st("sdn"), st("sdc"), st("slh"), st("slc"))
```

```python
import functools

import jax
import jax.numpy as jnp
from jax import lax
from jax.experimental import pallas as pl
from jax.experimental.pallas import tpu as pltpu

F32 = jnp.float32
BF16 = jnp.bfloat16
HIGHEST = lax.Precision.HIGHEST

EPS = 1e-6
LRU_C = 8.0
CONV_W = 4
N_HEADS = 4
HEAD_DIM = 128
BR_WIDTH = N_HEADS * HEAD_DIM
N_BRANCH = 4
LANES = 128
SUBLANES = 8
DN_CHUNK = 128
NEG_BIG = -1e30
VMEM_LIMIT_BYTES = 56 * 1024 * 1024

C_DNQKV = 0
C_DNZ = 1536
C_LRUX = 2048
C_LRUG = 2560
C_FQ = 3072
C_FK = 3584
C_FV = 4096
C_FG = 4608
C_MQ = 5120
C_MG = 5632
C_MGATE = 6144
C_SMALL = 10240
N_PERM = 10752
INPROJ_TN = 1536
N_SMALL = 3 * N_HEADS
COLUMN_RUNS = ((0, 2048, 0), (2056, 4616, 2048), (4620, 10252, 4608), (2048, 2056, C_SMALL), (4616, 4620, C_SMALL + 8))
SM_BETA = 0
SM_DECAY = 4
SM_FORGET = 8

NT_DIMS = (((1,), (1,)), ((), ()))


def _params(*sem):
    return pltpu.CompilerParams(dimension_semantics=sem, vmem_limit_bytes=VMEM_LIMIT_BYTES)


def _rms(x, w):
    return x * lax.rsqrt(jnp.mean(x * x, axis=-1, keepdims=True) + EPS) * w


def _l2norm(x):
    return x * lax.rsqrt(jnp.sum(x * x, axis=-1, keepdims=True) + EPS)


def _sigmoid(x):
    return 1.0 / (1.0 + jnp.exp(-x))


def _silu(x):
    return x * _sigmoid(x)


def _softplus(x):
    return jnp.maximum(x, 0.0) + jnp.log1p(jnp.exp(-jnp.abs(x)))


def _log_sigmoid(x):
    return -_softplus(-x)


def _dot(a, b):
    return jnp.dot(a.astype(BF16), b.astype(BF16), preferred_element_type=F32)


def _dot_nt(a, b):
    return lax.dot_general(a.astype(BF16), b.astype(BF16), NT_DIMS, preferred_element_type=F32)


def _dot_exact(a, b):
    return jnp.dot(a, b, precision=HIGHEST, preferred_element_type=F32)


def _iota2(shape, axis):
    return lax.broadcasted_iota(jnp.int32, shape, axis)


def _head(h):
    return slice(h * HEAD_DIM, (h + 1) * HEAD_DIM)


def _inproj_kernel(x_ref, nw_ref, w_ref, b_ref, o_ref, h_ref):
    @pl.when(pl.program_id(1) == 0)
    def _():
        h_ref[...] = _rms(x_ref[...], nw_ref[...]).astype(BF16)

    o_ref[...] = jnp.dot(h_ref[...], w_ref[...], preferred_element_type=F32) + b_ref[...]


def _repack_kernel(w_ref, o_ref):
    o_ref[...] = jnp.zeros_like(o_ref)
    for start, stop, dst in COLUMN_RUNS:
        lo = start // LANES * LANES
        hi = min(-(-stop // LANES) * LANES, w_ref.shape[-1])
        run = w_ref[:, lo:hi][:, start - lo:stop - lo]
        o_ref[:, dst:dst + stop - start] = run.astype(BF16)


def _repack_w_in(w_in):
    depth, d, n_in = w_in.shape
    tr = 64
    return pl.pallas_call(
        _repack_kernel,
        out_shape=jax.ShapeDtypeStruct((depth, d, N_PERM), BF16),
        grid=(depth, d // tr),
        in_specs=[pl.BlockSpec((None, tr, n_in), lambda l, i: (l, i, 0))],
        out_specs=pl.BlockSpec((None, tr, N_PERM), lambda l, i: (l, i, 0)),
        compiler_params=_params("parallel", "parallel"),
        name="repack_w_in",
    )(w_in)


def _inproj(x, norm_w, w_all, b, layer):
    t, d = x.shape
    n = w_all.shape[2]
    tm = min(t, 1024)
    tn = INPROJ_TN
    return pl.pallas_call(
        _inproj_kernel,
        out_shape=jax.ShapeDtypeStruct((t, n), F32),
        grid=(t // tm, n // tn),
        in_specs=[pl.BlockSpec((tm, d), lambda i, j: (i, 0)),
                  pl.BlockSpec((1, d), lambda i, j: (0, 0)),
                  pl.BlockSpec((None, d, tn), lambda i, j: (layer, 0, j)),
                  pl.BlockSpec((1, tn), lambda i, j: (0, j))],
        out_specs=pl.BlockSpec((tm, tn), lambda i, j: (i, j)),
        scratch_shapes=[pltpu.VMEM((tm, d), BF16)],
        compiler_params=_params("parallel", "arbitrary"),
        name="inproj",
    )(x, norm_w, w_all, b)


def _fox_prep_kernel(q_ref, k_ref, v_ref, sm_ref, qw_ref, kw_ref,
                     qn_ref, kn_ref, vo_ref, kb_ref, vb_ref, lf_ref, c_ref, ct_ref, carry_ref, *, tm):
    @pl.when(pl.program_id(1) == 0)
    def _():
        carry_ref[...] = jnp.zeros_like(carry_ref)

    scale = HEAD_DIM ** -0.5
    for h in range(N_HEADS):
        qn_ref[:, _head(h)] = (_rms(q_ref[:, _head(h)], qw_ref[...]) * scale).astype(qn_ref.dtype)
        kn = _rms(k_ref[:, _head(h)], kw_ref[...])
        kn_ref[:, _head(h)] = kn
        kb_ref[:, _head(h)] = kn.astype(BF16)
    v = v_ref[...]
    vo_ref[...] = v
    vb_ref[...] = v.astype(BF16)
    lf = _log_sigmoid(sm_ref[...])
    lf_ref[...] = lf[:, SM_FORGET:SM_FORGET + N_HEADS]
    tri = (_iota2((tm, tm), 0) >= _iota2((tm, tm), 1)).astype(F32)
    c = _dot_exact(tri, lf) + carry_ref[...]
    c_ref[...] = c
    ct_ref[...] = c.T
    carry_ref[...] = c[tm - 1:tm, :]


def _fox_prep(proj, qn_w, kn_w, nb, s, q_dtype):
    t = proj.shape[0]
    tm = min(s, 256)
    nj = s // tm
    row = lambda b, j: b * nj + j
    wide = lambda off: pl.BlockSpec((tm, BR_WIDTH), lambda b, j: (row(b, j), off // BR_WIDTH))
    out_rows = pl.BlockSpec((tm, BR_WIDTH), lambda b, j: (row(b, j), 0))
    wide_out = lambda dt: jax.ShapeDtypeStruct((t, BR_WIDTH), dt)
    return pl.pallas_call(
        functools.partial(_fox_prep_kernel, tm=tm),
        out_shape=(wide_out(q_dtype), wide_out(F32), wide_out(F32), wide_out(BF16), wide_out(BF16),
                   jax.ShapeDtypeStruct((t, N_HEADS), F32),
                   jax.ShapeDtypeStruct((t, LANES), F32),
                   jax.ShapeDtypeStruct((nb, LANES, s), F32)),
        grid=(nb, nj),
        in_specs=[wide(C_FQ), wide(C_FK), wide(C_FV),
                  pl.BlockSpec((tm, LANES), lambda b, j: (row(b, j), C_SMALL // LANES)),
                  pl.BlockSpec((1, HEAD_DIM), lambda b, j: (0, 0)),
                  pl.BlockSpec((1, HEAD_DIM), lambda b, j: (0, 0))],
        out_specs=(out_rows, out_rows, out_rows, out_rows, out_rows,
                   pl.BlockSpec((tm, N_HEADS), lambda b, j: (row(b, j), 0)),
                   pl.BlockSpec((tm, LANES), lambda b, j: (row(b, j), 0)),
                   pl.BlockSpec((None, LANES, tm), lambda b, j: (b, 0, j))),
        scratch_shapes=[pltpu.VMEM((1, LANES), F32)],
        compiler_params=_params("parallel", "arbitrary"),
        name="fox_prep",
    )(proj, proj, proj, proj, qn_w, kn_w)


def _fox_flash_kernel(q_ref, k_ref, v_ref, cq_ref, ck_ref, g_ref, o_ref, m_ref, l_ref, acc_ref, cqb_ref, *, tq):
    i = pl.program_id(1)
    j = pl.program_id(2)
    lane_tiles = tq // LANES

    @pl.when(j == 0)
    def _():
        m_ref[...] = jnp.full_like(m_ref, NEG_BIG)
        l_ref[...] = jnp.zeros_like(l_ref)
        acc_ref[...] = jnp.zeros_like(acc_ref)
        for h in range(N_HEADS):
            cqb_ref[h] = jnp.broadcast_to(cq_ref[:, SM_FORGET + h:SM_FORGET + h + 1], (tq, LANES))

    def update(diagonal):
        for h in range(N_HEADS):
            s = _dot_nt(q_ref[:, _head(h)], k_ref[:, _head(h)]) - ck_ref[h:h + 1, :]
            if diagonal:
                s = jnp.where(_iota2((tq, tq), 0) >= _iota2((tq, tq), 1), s, NEG_BIG)
            cqb = cqb_ref[h]
            m_prev = m_ref[h]
            m_new = jnp.maximum(m_prev, jnp.max(s, axis=-1, keepdims=True) + cqb)
            shift = m_new - cqb
            p = jnp.exp(s - jnp.concatenate([shift] * lane_tiles, axis=1))
            alpha = jnp.exp(m_prev - m_new)
            l_ref[h] = alpha * l_ref[h] + jnp.sum(p, axis=-1, keepdims=True)
            acc_ref[h] = alpha * acc_ref[h] + _dot(p, v_ref[:, _head(h)])
            m_ref[h] = m_new

    @pl.when(j < i)
    def _():
        update(False)

    @pl.when(j == i)
    def _():
        update(True)
        for h in range(N_HEADS):
            o = acc_ref[h] / l_ref[h]
            o_ref[:, _head(h)] = (o * _silu(g_ref[:, _head(h)])).astype(BF16)


def _fox_flash(qn, kn, v, c, ct, proj, nb, s):
    t = qn.shape[0]
    tq = min(s, 512)
    nq = s // tq
    qrow = lambda b, i, j: (b * nq + i, 0)
    krow = lambda b, i, j: (b * nq + jnp.minimum(i, j), 0)
    stat = pltpu.VMEM((N_HEADS, tq, LANES), F32)
    return pl.pallas_call(
        functools.partial(_fox_flash_kernel, tq=tq),
        out_shape=jax.ShapeDtypeStruct((t, BR_WIDTH), BF16),
        grid=(nb, nq, nq),
        in_specs=[pl.BlockSpec((tq, BR_WIDTH), qrow),
                  pl.BlockSpec((tq, BR_WIDTH), krow),
                  pl.BlockSpec((tq, BR_WIDTH), krow),
                  pl.BlockSpec((tq, LANES), qrow),
                  pl.BlockSpec((None, SUBLANES, tq), lambda b, i, j: (b, SM_FORGET // SUBLANES, jnp.minimum(i, j))),
                  pl.BlockSpec((tq, BR_WIDTH), lambda b, i, j: (b * nq + i, C_FG // BR_WIDTH))],
        out_specs=pl.BlockSpec((tq, BR_WIDTH), qrow),
        scratch_shapes=[stat, stat, stat, stat],
        compiler_params=_params("parallel", "parallel", "arbitrary"),
        name="fox_flash",
    )(qn, kn, v, c, ct, proj)


def _mem_kv_kernel(m_ref, nw_ref, w_ref, kw_ref, mk_ref, mv_ref):
    kv = _dot(_rms(m_ref[...], nw_ref[...]), w_ref[...])
    for h in range(N_HEADS):
        mk_ref[:, _head(h)] = _rms(kv[:, _head(h)], kw_ref[...])
    mv_ref[...] = kv[:, BR_WIDTH:]


def _mem_kv(mem, norm_w, w_kv, kn_w):
    t, d = mem.shape
    tm = min(t, 256)
    return pl.pallas_call(
        _mem_kv_kernel,
        out_shape=(jax.ShapeDtypeStruct((t, BR_WIDTH), F32), jax.ShapeDtypeStruct((t, BR_WIDTH), F32)),
        grid=(t // tm,),
        in_specs=[pl.BlockSpec((tm, d), lambda i: (i, 0)),
                  pl.BlockSpec((1, d), lambda i: (0, 0)),
                  pl.BlockSpec((d, 2 * BR_WIDTH), lambda i: (0, 0)),
                  pl.BlockSpec((1, HEAD_DIM), lambda i: (0, 0))],
        out_specs=(pl.BlockSpec((tm, BR_WIDTH), lambda i: (i, 0)),
                   pl.BlockSpec((tm, BR_WIDTH), lambda i: (i, 0))),
        compiler_params=_params("parallel"),
        name="mem_kv",
    )(mem, norm_w, w_kv, kn_w)


def _mem_attn_kernel(q_ref, g_ref, k_ref, v_ref, qw_ref, o_ref):
    scale = HEAD_DIM ** -0.5
    for h in range(N_HEADS):
        q = _rms(q_ref[:, _head(h)], qw_ref[...]) * scale
        s = _dot_nt(q, k_ref[:, _head(h)])
        p = jnp.exp(s - jnp.max(s, axis=-1, keepdims=True))
        o = _dot(p, v_ref[:, _head(h)]) / jnp.sum(p, axis=-1, keepdims=True)
        o_ref[:, _head(h)] = (o * _silu(g_ref[:, _head(h)])).astype(BF16)


def _mem_attn(proj, mk, mv, qn_w, nb, s):
    t = proj.shape[0]
    m = mk.shape[0] // nb
    tq = min(s, 512)
    nq = s // tq
    return pl.pallas_call(
        _mem_attn_kernel,
        out_shape=jax.ShapeDtypeStruct((t, BR_WIDTH), BF16),
        grid=(nb, nq),
        in_specs=[pl.BlockSpec((tq, BR_WIDTH), lambda b, i: (b * nq + i, C_MQ // BR_WIDTH)),
                  pl.BlockSpec((tq, BR_WIDTH), lambda b, i: (b * nq + i, C_MG // BR_WIDTH)),
                  pl.BlockSpec((m, BR_WIDTH), lambda b, i: (b, 0)),
                  pl.BlockSpec((m, BR_WIDTH), lambda b, i: (b, 0)),
                  pl.BlockSpec((1, HEAD_DIM), lambda b, i: (0, 0))],
        out_specs=pl.BlockSpec((tq, BR_WIDTH), lambda b, i: (b * nq + i, 0)),
        compiler_params=_params("parallel", "parallel"),
        name="mem_attn",
    )(proj, proj, mk, mv, qn_w)


def _causal_conv_tile(x, halo, w_ref):
    rows8 = _iota2(halo.shape, 0)
    y = x * w_ref[CONV_W - 1:CONV_W, :]
    for k in range(1, CONV_W):
        xr = pltpu.roll(x, k, axis=0)
        hr = pltpu.roll(halo, k, axis=0)
        top = jnp.where(rows8 < k, hr, xr[:SUBLANES])
        xs = jnp.concatenate([top, xr[SUBLANES:]], axis=0)
        y = y + xs * w_ref[CONV_W - 1 - k:CONV_W - k, :]
    return y


def _unit_lower_inverses(mats, row, col):
    n = mats[0].shape[0]

    def off_block(shift):
        return (((row >> (shift + 1)) == (col >> (shift + 1)))
                & (((row >> shift) & 1) == 1) & (((col >> shift) & 1) == 0))

    eye = (row == col).astype(F32)
    first = off_block(0)
    ds = [eye - jnp.where(first, a, 0.0) for a in mats]
    shift = 1
    while (1 << shift) < n:
        mask = off_block(shift)
        ts = [_dot(jnp.where(mask, a, 0.0), d) for a, d in zip(mats, ds)]
        ds = [d - _dot(d, t) for d, t in zip(ds, ts)]
        shift += 1
    return ds


def _dn_kernel(x_ref, z_ref, sm_ref, cw_ref, al_ref, dt_ref, ow_ref,
               o_ref, st_ref, stt_ref, halo_ref, *, tl):
    j = pl.program_id(1)

    @pl.when(j == 0)
    def _():
        stt_ref[...] = jnp.zeros_like(stt_ref)
        halo_ref[...] = jnp.zeros_like(halo_ref)

    x = x_ref[...]
    qkv = _silu(_causal_conv_tile(x, halo_ref[...], cw_ref))
    halo_ref[...] = x[tl - SUBLANES:, :]
    sm = sm_ref[...]
    beta_all = _sigmoid(sm)
    la_all = -jnp.exp(al_ref[...]) * _softplus(sm + dt_ref[...])

    cc = DN_CHUNK
    row = _iota2((cc, cc), 0)
    col = _iota2((cc, cc), 1)
    incl = row >= col
    strict = row > col
    tri = incl.astype(F32)
    kd = N_HEADS * HEAD_DIM

    systems = []
    for c in range(tl // cc):
        rows = slice(c * cc, (c + 1) * cc)
        g_all = _dot_exact(tri, la_all[rows])
        gt_all = g_all.T
        eg_all = jnp.exp(g_all)
        for h in range(N_HEADS):
            q = _l2norm(qkv[rows, h * HEAD_DIM:(h + 1) * HEAD_DIM]) * (HEAD_DIM ** -0.5)
            k = _l2norm(qkv[rows, kd + h * HEAD_DIM:kd + (h + 1) * HEAD_DIM])
            v = qkv[rows, 2 * kd + h * HEAD_DIM:2 * kd + (h + 1) * HEAD_DIM]
            beta = beta_all[rows, SM_BETA + h:SM_BETA + h + 1]
            gc = g_all[:, SM_DECAY + h:SM_DECAY + h + 1]
            gr = gt_all[SM_DECAY + h:SM_DECAY + h + 1, :]
            eg = eg_all[:, SM_DECAY + h:SM_DECAY + h + 1]
            g_last = g_all[cc - 1:cc, SM_DECAY + h:SM_DECAY + h + 1]
            decay = jnp.exp(jnp.where(incl, gc - gr, NEG_BIG))
            kb = k.astype(BF16)
            qb = q.astype(BF16)
            systems.append(dict(
                a=jnp.where(strict, beta * decay * _dot_nt(kb, kb), 0.0),
                qk=(decay * _dot_nt(qb, kb)).astype(BF16),
                rhs=jnp.concatenate([beta * v, (beta * eg) * k], axis=1).astype(BF16),
                kdec_t=(jnp.exp(g_last - gc) * k).T.astype(BF16),
                q=qb, eg=eg, g_end=jnp.exp(g_last)))

    ds = _unit_lower_inverses([sy["a"] for sy in systems], row, col)
    sols = [_dot(d, sy["rhs"]) for d, sy in zip(ds, systems)]

    heads = range(N_HEADS)
    stts = [stt_ref[h] for h in heads]
    for c in range(tl // cc):
        rows = slice(c * cc, (c + 1) * cc)
        sys_c = systems[c * N_HEADS:(c + 1) * N_HEADS]
        sol_c = sols[c * N_HEADS:(c + 1) * N_HEADS]
        us = [sol_c[h][:, :HEAD_DIM] - _dot(sol_c[h][:, HEAD_DIM:], stts[h]) for h in heads]
        os = [sys_c[h]["eg"] * _dot(sys_c[h]["q"], stts[h]) + _dot(sys_c[h]["qk"], us[h]) for h in heads]
        stts = [sys_c[h]["g_end"] * stts[h] + _dot(sys_c[h]["kdec_t"], us[h]) for h in heads]
        for h in heads:
            o = _rms(os[h], ow_ref[...]) * _silu(z_ref[rows, _head(h)])
            o_ref[rows, _head(h)] = o.astype(BF16)
    for h in heads:
        stt_ref[h] = stts[h]

    @pl.when(j == pl.num_programs(1) - 1)
    def _():
        for h in range(N_HEADS):
            st_ref[h] = stt_ref[h].T


def _dn_prompt(proj, conv_w, al_row, dt_row, onorm_w, nb, s):
    t = proj.shape[0]
    tl = min(s, 512)
    nj = s // tl
    qkv_w = 3 * BR_WIDTH
    row = lambda b, j: b * nj + j
    return pl.pallas_call(
        functools.partial(_dn_kernel, tl=tl),
        out_shape=(jax.ShapeDtypeStruct((t, BR_WIDTH), BF16),
                   jax.ShapeDtypeStruct((nb, N_HEADS, HEAD_DIM, HEAD_DIM), F32)),
        grid=(nb, nj),
        in_specs=[pl.BlockSpec((tl, qkv_w), lambda b, j: (row(b, j), 0)),
                  pl.BlockSpec((tl, BR_WIDTH), lambda b, j: (row(b, j), C_DNZ // BR_WIDTH)),
                  pl.BlockSpec((tl, LANES), lambda b, j: (row(b, j), C_SMALL // LANES)),
                  pl.BlockSpec((CONV_W, qkv_w), lambda b, j: (0, 0)),
                  pl.BlockSpec((1, LANES), lambda b, j: (0, 0)),
                  pl.BlockSpec((1, LANES), lambda b, j: (0, 0)),
                  pl.BlockSpec((1, HEAD_DIM), lambda b, j: (0, 0))],
        out_specs=(pl.BlockSpec((tl, BR_WIDTH), lambda b, j: (row(b, j), 0)),
                   pl.BlockSpec((None, N_HEADS, HEAD_DIM, HEAD_DIM), lambda b, j: (b, 0, 0, 0))),
        scratch_shapes=[pltpu.VMEM((N_HEADS, HEAD_DIM, HEAD_DIM), F32),
                        pltpu.VMEM((SUBLANES, qkv_w), F32)],
        compiler_params=_params("parallel", "arbitrary"),
        name="dn_prompt",
    )(proj, proj, proj, conv_w, al_row, dt_row, onorm_w)


def _lru_gates(xc, wr_ref, br_ref, wi_ref, bi_ref, lam_ref):
    r = _sigmoid(_dot(xc, wr_ref[...]) + br_ref[...])
    i = _sigmoid(_dot(xc, wi_ref[...]) + bi_ref[...])
    log_a = -LRU_C * r * _softplus(-lam_ref[...])
    a = jnp.exp(log_a)
    one_minus_a2 = -jnp.tanh(log_a) * (a * a + 1.0)
    return a, jnp.sqrt(one_minus_a2) * (i * xc)


def _lru_kernel(x_ref, g_ref, cw_ref, cb_ref, wr_ref, br_ref, wi_ref, bi_ref, lam_ref,
                o_ref, hl_ref, hc_ref, halo_ref, *, tl):
    @pl.when(pl.program_id(1) == 0)
    def _():
        hc_ref[...] = jnp.zeros_like(hc_ref)
        halo_ref[...] = jnp.zeros_like(halo_ref)

    x = x_ref[...]
    xc = _causal_conv_tile(x, halo_ref[...], cw_ref) + cb_ref[...]
    halo_ref[...] = x[tl - SUBLANES:, :]
    a, b = _lru_gates(xc, wr_ref, br_ref, wi_ref, bi_ref, lam_ref)
    rows = _iota2(a.shape, 0)
    d = 1
    while d < tl:
        a_up = jnp.where(rows >= d, pltpu.roll(a, d, axis=0), 1.0)
        b_up = jnp.where(rows >= d, pltpu.roll(b, d, axis=0), 0.0)
        b = a * b_up + b
        a = a * a_up
        d *= 2
    hseq = a * hc_ref[...] + b
    hc_ref[...] = hseq[tl - 1:tl, :]
    hl_ref[...] = hseq[tl - 1:tl, :]
    o_ref[...] = (hseq * _silu(g_ref[...])).astype(BF16)


def _lru_prompt(proj, conv_w, conv_b, w_r, b_r, w_i, b_i, lam, nb, s):
    t = proj.shape[0]
    tl = min(s, 256)
    nj = s // tl
    row = lambda b, j: b * nj + j
    vec = pl.BlockSpec((1, BR_WIDTH), lambda b, j: (0, 0))
    mat = pl.BlockSpec((BR_WIDTH, BR_WIDTH), lambda b, j: (0, 0))
    return pl.pallas_call(
        functools.partial(_lru_kernel, tl=tl),
        out_shape=(jax.ShapeDtypeStruct((t, BR_WIDTH), BF16),
                   jax.ShapeDtypeStruct((nb, 1, BR_WIDTH), F32)),
        grid=(nb, nj),
        in_specs=[pl.BlockSpec((tl, BR_WIDTH), lambda b, j: (row(b, j), C_LRUX // BR_WIDTH)),
                  pl.BlockSpec((tl, BR_WIDTH), lambda b, j: (row(b, j), C_LRUG // BR_WIDTH)),
                  pl.BlockSpec((CONV_W, BR_WIDTH), lambda b, j: (0, 0)),
                  vec, mat, vec, mat, vec, vec],
        out_specs=(pl.BlockSpec((tl, BR_WIDTH), lambda b, j: (row(b, j), 0)),
                   pl.BlockSpec((None, 1, BR_WIDTH), lambda b, j: (b, 0, 0))),
        scratch_shapes=[pltpu.VMEM((1, BR_WIDTH), F32), pltpu.VMEM((SUBLANES, BR_WIDTH), F32)],
        compiler_params=_params("parallel", "arbitrary"),
        name="lru_prompt",
    )(proj, proj, conv_w, conv_b, w_r, b_r, w_i, b_i, lam)


def _merge_kernel(x_ref, a_ref, b_ref, c_ref, m_ref, g0_ref, g1_ref, g2_ref, g3_ref, wb_ref, wo_ref, y_ref):
    merged = None
    for n, (br, gate) in enumerate(((a_ref, g0_ref), (b_ref, g1_ref), (c_ref, g2_ref), (m_ref, g3_ref))):
        up = jnp.dot(br[...], wb_ref[n], preferred_element_type=F32)
        term = _sigmoid(gate[...]) * up
        merged = term if merged is None else merged + term
    y_ref[...] = x_ref[...] + jnp.dot(merged.astype(BF16), wo_ref[...], preferred_element_type=F32)


def _merge(x, out_a, out_b, out_c, out_m, proj, w_branch, w_out):
    t, d = x.shape
    tm = min(t, 512)
    rows = lambda w: pl.BlockSpec((tm, w), lambda i: (i, 0))
    gate = lambda n: pl.BlockSpec((tm, d), lambda i: (i, C_MGATE // d + n))
    return pl.pallas_call(
        _merge_kernel,
        out_shape=jax.ShapeDtypeStruct((t, d), F32),
        grid=(t // tm,),
        in_specs=[rows(d), rows(BR_WIDTH), rows(BR_WIDTH), rows(BR_WIDTH), rows(BR_WIDTH),
                  gate(0), gate(1), gate(2), gate(3),
                  pl.BlockSpec((N_BRANCH, BR_WIDTH, d), lambda i: (0, 0, 0)),
                  pl.BlockSpec((d, d), lambda i: (0, 0))],
        out_specs=rows(d),
        compiler_params=_params("parallel"),
        name="merge",
    )(x, out_a, out_b, out_c, out_m, proj, proj, proj, proj, w_branch, w_out)


def _sample_step_kernel(qkv_ref, z_ref, lx_ref, lg_ref, sm_ref, st_ref, dbuf_ref, lh_ref, lbuf_ref,
                        dcw_ref, al_ref, dt_ref, ow_ref, lcw_ref, lcb_ref, wr_ref, br_ref, wi_ref, bi_ref, lam_ref,
                        oa_ref, ob_ref, sto_ref, dbo_ref, lho_ref, lbo_ref, o_scr, *, bs):
    kd = N_HEADS * HEAD_DIM

    def step_conv(x, buf_ref, bufo_ref, w_ref):
        y = x * w_ref[CONV_W - 1:CONV_W, :]
        for k in range(CONV_W - 1):
            y = y + buf_ref[:, k, :] * w_ref[k:k + 1, :]
        for k in range(CONV_W - 2):
            bufo_ref[:, k, :] = buf_ref[:, k + 1, :]
        bufo_ref[:, CONV_W - 2, :] = x
        return y

    qkv = _silu(step_conv(qkv_ref[...], dbuf_ref, dbo_ref, dcw_ref))
    sm = sm_ref[...]
    beta_all = _sigmoid(sm)
    eg_all = jnp.exp(-jnp.exp(al_ref[...]) * _softplus(sm + dt_ref[...]))
    eye = _iota2((HEAD_DIM, HEAD_DIM), 0) == _iota2((HEAD_DIM, HEAD_DIM), 1)
    pad = jnp.zeros((SUBLANES - 2, HEAD_DIM), F32)
    for h in range(N_HEADS):
        q = _l2norm(qkv[:, h * HEAD_DIM:(h + 1) * HEAD_DIM]) * (HEAD_DIM ** -0.5)
        k = _l2norm(qkv[:, kd + h * HEAD_DIM:kd + (h + 1) * HEAD_DIM])
        v = qkv[:, 2 * kd + h * HEAD_DIM:2 * kd + (h + 1) * HEAD_DIM]
        beta = beta_all[:, SM_BETA + h:SM_BETA + h + 1]
        eg = eg_all[:, SM_DECAY + h:SM_DECAY + h + 1]
        qk = jnp.sum(q * k, axis=-1, keepdims=True)
        for b in range(bs):
            st = st_ref[b, h]
            kq = jnp.concatenate([k[b:b + 1], q[b:b + 1], pad], axis=0)
            sk_sq = _dot_nt(kq, st)
            eg_b = eg[b:b + 1]
            u = beta[b:b + 1] * (v[b:b + 1] - eg_b * sk_sq[0:1])
            o_scr[b:b + 1, _head(h)] = eg_b * sk_sq[1:2] + qk[b:b + 1] * u
            u_diag = jnp.where(eye, jnp.broadcast_to(u, (HEAD_DIM, HEAD_DIM)), 0.0)
            k_rows = jnp.broadcast_to(k[b:b + 1], (HEAD_DIM, HEAD_DIM))
            sto_ref[b, h] = eg_b * st + _dot(u_diag, k_rows)
    o = o_scr[...]
    for h in range(N_HEADS):
        oa_ref[:, _head(h)] = (_rms(o[:, _head(h)], ow_ref[...]) * _silu(z_ref[:, _head(h)])).astype(BF16)

    xc = step_conv(lx_ref[...], lbuf_ref, lbo_ref, lcw_ref) + lcb_ref[...]
    a, bx = _lru_gates(xc, wr_ref, br_ref, wi_ref, bi_ref, lam_ref)
    hnew = a * lh_ref[...] + bx
    lho_ref[...] = hnew
    ob_ref[...] = (hnew * _silu(lg_ref[...])).astype(BF16)


def _sample_step(proj, layer, state_dn, state_dn_conv, state_lru_h, state_lru_conv,
                 dn_conv_w, al_row, dt_row, onorm_w, lru_conv_w, lru_conv_b, w_r, b_r, w_i, b_i, lam):
    nb = proj.shape[0]
    bs = SUBLANES
    qkv_w = 3 * BR_WIDTH
    wide = lambda off: pl.BlockSpec((bs, BR_WIDTH), lambda i: (i, off // BR_WIDTH))
    const = lambda shape: pl.BlockSpec(shape, lambda i: (0,) * len(shape))
    st_in = pl.BlockSpec((None, bs, N_HEADS, HEAD_DIM, HEAD_DIM), lambda i: (layer, i, 0, 0, 0))
    dbuf_in = pl.BlockSpec((None, bs, CONV_W - 1, qkv_w), lambda i: (layer, i, 0, 0))
    lh_in = pl.BlockSpec((None, bs, BR_WIDTH), lambda i: (layer, i, 0))
    lbuf_in = pl.BlockSpec((None, bs, CONV_W - 1, BR_WIDTH), lambda i: (layer, i, 0, 0))
    return pl.pallas_call(
        functools.partial(_sample_step_kernel, bs=bs),
        out_shape=(jax.ShapeDtypeStruct((nb, BR_WIDTH), BF16),
                   jax.ShapeDtypeStruct((nb, BR_WIDTH), BF16),
                   jax.ShapeDtypeStruct((nb, N_HEADS, HEAD_DIM, HEAD_DIM), F32),
                   jax.ShapeDtypeStruct((nb, CONV_W - 1, qkv_w), F32),
                   jax.ShapeDtypeStruct((nb, BR_WIDTH), F32),
                   jax.ShapeDtypeStruct((nb, CONV_W - 1, BR_WIDTH), F32)),
        grid=(nb // bs,),
        in_specs=[pl.BlockSpec((bs, qkv_w), lambda i: (i, 0)), wide(C_DNZ), wide(C_LRUX), wide(C_LRUG),
                  pl.BlockSpec((bs, LANES), lambda i: (i, C_SMALL // LANES)),
                  st_in, dbuf_in, lh_in, lbuf_in,
                  const((CONV_W, qkv_w)), const((1, LANES)), const((1, LANES)), const((1, HEAD_DIM)),
                  const((CONV_W, BR_WIDTH)), const((1, BR_WIDTH)),
                  const((BR_WIDTH, BR_WIDTH)), const((1, BR_WIDTH)),
                  const((BR_WIDTH, BR_WIDTH)), const((1, BR_WIDTH)), const((1, BR_WIDTH))],
        out_specs=(pl.BlockSpec((bs, BR_WIDTH), lambda i: (i, 0)),
                   pl.BlockSpec((bs, BR_WIDTH), lambda i: (i, 0)),
                   pl.BlockSpec((bs, N_HEADS, HEAD_DIM, HEAD_DIM), lambda i: (i, 0, 0, 0)),
                   pl.BlockSpec((bs, CONV_W - 1, qkv_w), lambda i: (i, 0, 0)),
                   pl.BlockSpec((bs, BR_WIDTH), lambda i: (i, 0)),
                   pl.BlockSpec((bs, CONV_W - 1, BR_WIDTH), lambda i: (i, 0, 0))),
        scratch_shapes=[pltpu.VMEM((bs, BR_WIDTH), F32)],
        compiler_params=_params("parallel"),
        name="sample_step",
    )(proj, proj, proj, proj, proj, state_dn, state_dn_conv, state_lru_h, state_lru_conv,
      dn_conv_w, al_row, dt_row, onorm_w, lru_conv_w, lru_conv_b, w_r, b_r, w_i, b_i, lam)


def _own_head_mask(n_rows_kv):
    shape = (SUBLANES, n_rows_kv)
    return (_iota2(shape, 1) % N_HEADS) == _iota2(shape, 0)


def _heads_to_row(o8):
    return jnp.concatenate([o8[h:h + 1] for h in range(N_HEADS)], axis=1)


def _split3(x):
    hi = x.astype(BF16)
    r1 = x - hi.astype(F32)
    mid = r1.astype(BF16)
    lo = (r1 - mid.astype(F32)).astype(BF16)
    return hi, mid, lo


def _pool_bias_kernel(lf_ref, after_ref, same_ref, inpage_ref, total_ref):
    in_page = None
    total = None
    for h in range(N_HEADS):
        for piece in _split3(lf_ref[:, h, :]):
            a = jnp.dot(piece, after_ref[h], preferred_element_type=F32)
            t = jnp.dot(piece, same_ref[h], preferred_element_type=F32)
            in_page = a if in_page is None else in_page + a
            total = t if total is None else total + t
    inpage_ref[...] = in_page
    total_ref[...] = total


def _pool_bias(cache_lf):
    n, nh, page = cache_lf.shape
    w = page * nh
    tp = min(n, 512)
    src_tok = jnp.arange(page)[None, :, None]
    src_head = jnp.arange(nh)[:, None, None]
    dst = jnp.arange(w)[None, None, :]
    same = (dst % nh) == src_head
    after = same & (src_tok > dst // nh)
    rows = pl.BlockSpec((tp, w), lambda i: (i, 0))
    const = pl.BlockSpec((nh, page, w), lambda i: (0, 0, 0))
    return pl.pallas_call(
        _pool_bias_kernel,
        out_shape=(jax.ShapeDtypeStruct((n, w), F32), jax.ShapeDtypeStruct((n, w), F32)),
        grid=(n // tp,),
        in_specs=[pl.BlockSpec((tp, nh, page), lambda i: (i, 0, 0)), const, const],
        out_specs=(rows, rows),
        compiler_params=_params("parallel"),
        name="pool_bias",
    )(cache_lf, after.astype(BF16), jnp.broadcast_to(same, after.shape).astype(BF16))


def _fox_decode_kernel(pt_ref, q_ref, kn_ref, vn_ref, lfn_ref, g_ref, *rest, n_pages):
    k_refs = rest[:n_pages]
    v_refs = rest[n_pages:2 * n_pages]
    inpage_refs = rest[2 * n_pages:3 * n_pages]
    total_refs = rest[3 * n_pages:4 * n_pages]
    o_ref = rest[4 * n_pages]
    del pt_ref
    q8 = q_ref[...]
    qb = q8.astype(BF16)
    own = _own_head_mask(k_refs[0].shape[0])

    later = lfn_ref[...]
    scores = [None] * n_pages
    for p in reversed(range(n_pages)):
        s = _dot_nt(qb, k_refs[p][...]) + (inpage_refs[p][...] + later)
        scores[p] = jnp.where(own, s, NEG_BIG)
        later = later + total_refs[p][...]
    s_self = jnp.sum(q8 * kn_ref[...], axis=-1, keepdims=True)
    m = s_self
    for s in scores:
        m = jnp.maximum(m, jnp.max(s, axis=-1, keepdims=True))
    p_self = jnp.exp(s_self - m)
    l = p_self
    acc = p_self * vn_ref[...]
    for p in range(n_pages):
        pr = jnp.exp(scores[p] - m)
        l = l + jnp.sum(pr, axis=-1, keepdims=True)
        acc = acc + _dot(pr, v_refs[p][...])
    o_ref[...] = (_heads_to_row(acc / l) * _silu(g_ref[...])).astype(BF16)


def _fox_decode(q8, k8, v8, lf_new, fg, cache_k, cache_v, in_page, totals, page_table, layer, n_pool):
    nb = q8.shape[0]
    n_pages = page_table.shape[0] // nb
    kv_rows = cache_k.shape[1]
    base = layer * n_pool
    row = pl.BlockSpec((None, 1, BR_WIDTH), lambda b, pt: (b, 0, 0))
    heads = pl.BlockSpec((None, SUBLANES, HEAD_DIM), lambda b, pt: (b, 0, 0))

    def paged(shape, p):
        return pl.BlockSpec((None,) + shape, lambda b, pt: (base + pt[b * n_pages + p], 0, 0))

    in_specs = [heads, heads, heads, pl.BlockSpec((None, 1, kv_rows), lambda b, pt: (b, 0, 0)), row]
    in_specs += [paged((kv_rows, HEAD_DIM), p) for p in range(n_pages)]
    in_specs += [paged((kv_rows, HEAD_DIM), p) for p in range(n_pages)]
    in_specs += [paged((1, kv_rows), p) for p in range(n_pages)]
    in_specs += [paged((1, kv_rows), p) for p in range(n_pages)]
    grid_spec = pltpu.PrefetchScalarGridSpec(
        num_scalar_prefetch=1,
        grid=(nb,),
        in_specs=in_specs,
        out_specs=row)
    return pl.pallas_call(
        functools.partial(_fox_decode_kernel, n_pages=n_pages),
        out_shape=jax.ShapeDtypeStruct((nb, 1, BR_WIDTH), BF16),
        grid_spec=grid_spec,
        compiler_params=_params("parallel"),
        name="fox_decode",
    )(page_table, q8, k8, v8, lf_new, fg,
      *([cache_k] * n_pages), *([cache_v] * n_pages), *([in_page] * n_pages), *([totals] * n_pages))


def _mem_decode_kernel(q_ref, g_ref, k_ref, v_ref, qw_ref, o_ref, o_scr, *, bs):
    own = _own_head_mask(k_ref.shape[1])
    scale = HEAD_DIM ** -0.5
    qs = [_rms(q_ref[:, _head(h)], qw_ref[...]) * scale for h in range(N_HEADS)]
    pad = jnp.zeros((SUBLANES - N_HEADS, HEAD_DIM), F32)
    for b in range(bs):
        q8 = jnp.concatenate([q[b:b + 1] for q in qs] + [pad], axis=0)
        s = jnp.where(own, _dot_nt(q8, k_ref[b]), NEG_BIG)
        p = jnp.exp(s - jnp.max(s, axis=-1, keepdims=True))
        o = _dot(p, v_ref[b]) / jnp.sum(p, axis=-1, keepdims=True)
        o_scr[b:b + 1, :] = _heads_to_row(o)
    o_ref[...] = (o_scr[...] * _silu(g_ref[...])).astype(BF16)


def _mem_decode(proj, cache_k, cache_v, qn_w, layer):
    nb = proj.shape[0]
    bs = SUBLANES
    m = cache_k.shape[1]
    base = layer * (nb // bs)
    kv = pl.BlockSpec((bs, m, HEAD_DIM), lambda i: (base + i, 0, 0))
    return pl.pallas_call(
        functools.partial(_mem_decode_kernel, bs=bs),
        out_shape=jax.ShapeDtypeStruct((nb, BR_WIDTH), BF16),
        grid=(nb // bs,),
        in_specs=[pl.BlockSpec((bs, BR_WIDTH), lambda i: (i, C_MQ // BR_WIDTH)),
                  pl.BlockSpec((bs, BR_WIDTH), lambda i: (i, C_MG // BR_WIDTH)),
                  kv, kv, pl.BlockSpec((1, HEAD_DIM), lambda i: (0, 0))],
        out_specs=pl.BlockSpec((bs, BR_WIDTH), lambda i: (i, 0)),
        scratch_shapes=[pltpu.VMEM((bs, BR_WIDTH), F32)],
        compiler_params=_params("parallel"),
        name="mem_decode",
    )(proj, proj, cache_k, cache_v, qn_w)


def _permute_columns(a):
    runs = sorted(COLUMN_RUNS, key=lambda r: r[2])
    pad = jnp.zeros(a.shape[:-1] + (N_PERM - C_SMALL - N_SMALL,), a.dtype)
    return jnp.concatenate([a[..., start:stop] for start, stop, _ in runs] + [pad], axis=-1)


def _block_diag(w):
    nblk, e, f = w.shape
    eye = jnp.eye(nblk, dtype=w.dtype)
    return (eye[:, None, :, None] * w[:, :, None, :]).reshape(nblk * e, nblk * f)


def _decay_lane_row(v):
    return jnp.zeros((1, LANES), F32).at[0, SM_DECAY:SM_DECAY + N_HEADS].set(v)


def kernel(x_prompt, x_sample, cache_fox_k, cache_fox_v, cache_fox_logf, cache_mem_k, cache_mem_v, state_dn, state_dn_conv, state_lru_h, state_lru_conv, page_table, mem_prompt, norm_w, w_in, b_in, dn_conv_w, dn_A_log, dn_dt_bias, dn_onorm_w, lru_conv_w, lru_conv_b, lru_w_r, lru_b_r, lru_w_i, lru_b_i, lru_lambda, fox_qn_w, fox_kn_w, mem_norm_w, w_mem_kv, mem_qn_w, mem_kn_w, w_branch, w_out):
    bp, s, d = x_prompt.shape
    bd = x_sample.shape[0]
    depth = w_in.shape[0]
    n_pool, page = cache_fox_k.shape[1], cache_fox_k.shape[2]
    mem_tokens = mem_prompt.shape[1]
    tp = bp * s

    yp = x_prompt.reshape(tp, d)
    ys = x_sample.reshape(bd, d)
    mem2 = mem_prompt.reshape(bp * mem_tokens, d)
    kv_rows = page * N_HEADS
    cache_k2 = cache_fox_k.reshape(depth * n_pool, kv_rows, HEAD_DIM)
    cache_v2 = cache_fox_v.reshape(depth * n_pool, kv_rows, HEAD_DIM)
    in_page, totals = _pool_bias(jnp.swapaxes(cache_fox_logf, 2, 3).reshape(depth * n_pool, N_HEADS, page))
    in_page = in_page.reshape(depth * n_pool, 1, kv_rows)
    totals = totals.reshape(depth * n_pool, 1, kv_rows)
    cmem_k2 = cache_mem_k.reshape(depth * bd, mem_tokens * N_HEADS, HEAD_DIM)
    cmem_v2 = cache_mem_v.reshape(depth * bd, mem_tokens * N_HEADS, HEAD_DIM)
    pt_flat = page_table.reshape(-1)

    row = lambda v: v.reshape(1, -1)
    acc = {n: [] for n in ("pk", "pv", "plf", "pmk", "pmv", "pdn", "pdc", "plh", "plc",
                           "sk", "sv", "slf", "sdn", "sdc", "slh", "slc")}
    w_p = _repack_w_in(w_in)
    for l in range(depth):
        b_p = _permute_columns(b_in[l]).reshape(1, N_PERM)
        nw = row(norm_w[l])
        al_row = _decay_lane_row(dn_A_log[l])
        dt_row = _decay_lane_row(dn_dt_bias[l])
        ow = row(dn_onorm_w[l])
        wr = _block_diag(lru_w_r[l]).astype(BF16)
        wi = _block_diag(lru_w_i[l]).astype(BF16)
        lru_args = (lru_conv_w[l], row(lru_conv_b[l]), wr, row(lru_b_r[l]), wi, row(lru_b_i[l]), row(lru_lambda[l]))
        wb = w_branch[l].astype(BF16)
        wo = w_out[l].astype(BF16)
        fqw, fkw, mqw = row(fox_qn_w[l]), row(fox_kn_w[l]), row(mem_qn_w[l])

        proj = _inproj(yp, nw, w_p, b_p, l)
        qb, kn, fv, kb, vb, lf, c, ct = _fox_prep(proj, fqw, fkw, bp, s, BF16)
        out_c = _fox_flash(qb, kb, vb, c, ct, proj, bp, s)
        mk, mv = _mem_kv(mem2, row(mem_norm_w[l]), w_mem_kv[l].astype(BF16), row(mem_kn_w[l]))
        out_m = _mem_attn(proj, mk, mv, mqw, bp, s)
        out_a, dn_s = _dn_prompt(proj, dn_conv_w[l], al_row, dt_row, ow, bp, s)
        out_b, lru_h = _lru_prompt(proj, *lru_args, bp, s)
        proj3 = proj.reshape(bp, s, N_PERM)
        acc["pk"].append(kn.reshape(bp, s, N_HEADS, HEAD_DIM))
        acc["pv"].append(fv.reshape(bp, s, N_HEADS, HEAD_DIM))
        acc["plf"].append(lf.reshape(bp, s, N_HEADS))
        acc["pmk"].append(mk.reshape(bp, mem_tokens, N_HEADS, HEAD_DIM))
        acc["pmv"].append(mv.reshape(bp, mem_tokens, N_HEADS, HEAD_DIM))
        acc["pdn"].append(dn_s)
        acc["pdc"].append(proj3[:, s - (CONV_W - 1):, C_DNQKV:C_DNQKV + 3 * BR_WIDTH])
        acc["plh"].append(lru_h.reshape(bp, BR_WIDTH))
        acc["plc"].append(proj3[:, s - (CONV_W - 1):, C_LRUX:C_LRUX + BR_WIDTH])
        yp = _merge(yp, out_a, out_b, out_c, out_m, proj, wb, wo)

        proj_s = _inproj(ys, nw, w_p, b_p, l)
        qn_s, kn_s, fv_s, _, _, lf_s, _, _ = _fox_prep(proj_s, fqw, fkw, 1, bd, F32)
        head_rows = lambda a: jnp.pad(a.reshape(bd, N_HEADS, HEAD_DIM), ((0, 0), (0, SUBLANES - N_HEADS), (0, 0)))
        lf_new = jnp.tile(lf_s, (1, page)).reshape(bd, 1, kv_rows)
        fg_s = proj_s[:, C_FG:C_FG + BR_WIDTH].reshape(bd, 1, BR_WIDTH)
        out_c_s = _fox_decode(head_rows(qn_s), head_rows(kn_s), head_rows(fv_s), lf_new, fg_s,
                              cache_k2, cache_v2, in_page, totals, pt_flat, l, n_pool).reshape(bd, BR_WIDTH)
        out_m_s = _mem_decode(proj_s, cmem_k2, cmem_v2, mqw, l)
        out_a_s, out_b_s, dn_s_s, dn_c_s, lru_h_s, lru_c_s = _sample_step(
            proj_s, l, state_dn, state_dn_conv, state_lru_h, state_lru_conv,
            dn_conv_w[l], al_row, dt_row, ow, *lru_args)
        acc["sk"].append(kn_s.reshape(bd, 1, N_HEADS, HEAD_DIM))
        acc["sv"].append(fv_s.reshape(bd, 1, N_HEADS, HEAD_DIM))
        acc["slf"].append(lf_s.reshape(bd, 1, N_HEADS))
        acc["sdn"].append(dn_s_s)
        acc["sdc"].append(dn_c_s)
        acc["slh"].append(lru_h_s)
        acc["slc"].append(lru_c_s)
        ys = _merge(ys, out_a_s, out_b_s, out_c_s, out_m_s, proj_s, wb, wo)

    st = lambda n: jnp.stack(acc[n])
    return (yp.reshape(bp, s, d), ys.reshape(bd, 1, d),
            st("pk"), st("pv"), st("plf"), st("pmk"), st("pmv"), st("pdn"), st("pdc"), st("plh"), st("plc"),
            st("sk"), st("sv"), st("slf"), st("sdn"), st("sdc"), st("slh"), st("slc"))
```

```python
import functools

import jax
import jax.numpy as jnp
from jax import lax
from jax.experimental import pallas as pl
from jax.experimental.pallas import tpu as pltpu

F32 = jnp.float32
BF16 = jnp.bfloat16
HIGHEST = lax.Precision.HIGHEST

EPS = 1e-6
LRU_C = 8.0
CONV_W = 4
N_HEADS = 4
HEAD_DIM = 128
BR_WIDTH = N_HEADS * HEAD_DIM
N_BRANCH = 4
LANES = 128
SUBLANES = 8
DN_CHUNK = 128
NEG_BIG = -1e30
VMEM_LIMIT_BYTES = 56 * 1024 * 1024

C_DNQKV = 0
C_DNZ = 1536
C_LRUX = 2048
C_LRUG = 2560
C_FQ = 3072
C_FK = 3584
C_FV = 4096
C_FG = 4608
C_MQ = 5120
C_MG = 5632
C_SMALL = 6144
N_PERM = 6400
INPROJ_TN = 1280
N_SMALL = 3 * N_HEADS
N_GATE = 4096
COLUMN_RUNS = ((0, 2048, 0), (2056, 4616, 2048), (4620, 6156, 4608), (2048, 2056, C_SMALL), (4616, 4620, C_SMALL + 8))
GATE_RUN = (6156, 10252)
SM_BETA = 0
SM_DECAY = 4
SM_FORGET = 8

NT_DIMS = (((1,), (1,)), ((), ()))


def _params(*sem):
    return pltpu.CompilerParams(dimension_semantics=sem, vmem_limit_bytes=VMEM_LIMIT_BYTES)


def _rms(x, w):
    return x * lax.rsqrt(jnp.mean(x * x, axis=-1, keepdims=True) + EPS) * w


def _l2norm(x):
    return x * lax.rsqrt(jnp.sum(x * x, axis=-1, keepdims=True) + EPS)


def _sigmoid(x):
    return 0.5 * jnp.tanh(0.5 * x) + 0.5


def _silu(x):
    return x * _sigmoid(x)


def _softplus(x):
    return jnp.maximum(x, 0.0) + jnp.log1p(jnp.exp(-jnp.abs(x)))


def _log_sigmoid(x):
    return -_softplus(-x)


def _dot(a, b):
    return jnp.dot(a.astype(BF16), b.astype(BF16), preferred_element_type=F32)


def _dot_nt(a, b):
    return lax.dot_general(a.astype(BF16), b.astype(BF16), NT_DIMS, preferred_element_type=F32)


def _dot_exact(a, b):
    return jnp.dot(a, b, precision=HIGHEST, preferred_element_type=F32)


def _iota2(shape, axis):
    return lax.broadcasted_iota(jnp.int32, shape, axis)


def _head(h):
    return slice(h * HEAD_DIM, (h + 1) * HEAD_DIM)


def _inproj_kernel(x_ref, nw_ref, w_ref, b_ref, o_ref, h_ref):
    @pl.when(pl.program_id(1) == 0)
    def _():
        h_ref[...] = _rms(x_ref[...], nw_ref[...]).astype(BF16)

    o_ref[...] = jnp.dot(h_ref[...], w_ref[...], preferred_element_type=F32) + b_ref[...]


def _repack_kernel(w_ref, o_ref, g_ref):
    def run(start, stop):
        lo = start // LANES * LANES
        hi = min(-(-stop // LANES) * LANES, w_ref.shape[-1])
        return w_ref[:, lo:hi][:, start - lo:stop - lo].astype(BF16)

    o_ref[...] = jnp.zeros_like(o_ref)
    for start, stop, dst in COLUMN_RUNS:
        o_ref[:, dst:dst + stop - start] = run(start, stop)
    g_ref[...] = run(*GATE_RUN)


def _repack_w_in(w_in):
    depth, d, n_in = w_in.shape
    tr = 64
    return pl.pallas_call(
        _repack_kernel,
        out_shape=(jax.ShapeDtypeStruct((depth, d, N_PERM), BF16),
                   jax.ShapeDtypeStruct((depth, d, N_GATE), BF16)),
        grid=(depth, d // tr),
        in_specs=[pl.BlockSpec((None, tr, n_in), lambda l, i: (l, i, 0))],
        out_specs=(pl.BlockSpec((None, tr, N_PERM), lambda l, i: (l, i, 0)),
                   pl.BlockSpec((None, tr, N_GATE), lambda l, i: (l, i, 0))),
        compiler_params=_params("parallel", "parallel"),
        name="repack_w_in",
    )(w_in)


def _inproj(x, norm_w, w_all, b, layer):
    t, d = x.shape
    n = w_all.shape[2]
    tm = min(t, 1024)
    tn = INPROJ_TN
    return pl.pallas_call(
        _inproj_kernel,
        out_shape=jax.ShapeDtypeStruct((t, n), F32),
        grid=(t // tm, n // tn),
        in_specs=[pl.BlockSpec((tm, d), lambda i, j: (i, 0)),
                  pl.BlockSpec((1, d), lambda i, j: (0, 0)),
                  pl.BlockSpec((None, d, tn), lambda i, j: (layer, 0, j)),
                  pl.BlockSpec((1, tn), lambda i, j: (0, j))],
        out_specs=pl.BlockSpec((tm, tn), lambda i, j: (i, j)),
        scratch_shapes=[pltpu.VMEM((tm, d), BF16)],
        compiler_params=_params("parallel", "arbitrary"),
        name="inproj",
    )(x, norm_w, w_all, b)


def _fox_prep_kernel(q_ref, k_ref, v_ref, sm_ref, qw_ref, kw_ref,
                     qn_ref, kn_ref, vo_ref, kb_ref, vb_ref, lf_ref, c_ref, ct_ref, carry_ref, *, tm):
    @pl.when(pl.program_id(1) == 0)
    def _():
        carry_ref[...] = jnp.zeros_like(carry_ref)

    scale = HEAD_DIM ** -0.5
    for h in range(N_HEADS):
        qn_ref[:, _head(h)] = (_rms(q_ref[:, _head(h)], qw_ref[...]) * scale).astype(qn_ref.dtype)
        kn = _rms(k_ref[:, _head(h)], kw_ref[...])
        kn_ref[:, _head(h)] = kn
        kb_ref[:, _head(h)] = kn.astype(BF16)
    v = v_ref[...]
    vo_ref[...] = v
    vb_ref[...] = v.astype(BF16)
    lf = _log_sigmoid(sm_ref[...])
    lf_ref[...] = lf[:, SM_FORGET:SM_FORGET + N_HEADS]
    tri = (_iota2((tm, tm), 0) >= _iota2((tm, tm), 1)).astype(F32)
    c = _dot_exact(tri, lf) + carry_ref[...]
    c_ref[...] = c
    ct_ref[...] = c.T
    carry_ref[...] = c[tm - 1:tm, :]


def _fox_prep(proj, qn_w, kn_w, nb, s, q_dtype):
    t = proj.shape[0]
    tm = min(s, 256)
    nj = s // tm
    row = lambda b, j: b * nj + j
    wide = lambda off: pl.BlockSpec((tm, BR_WIDTH), lambda b, j: (row(b, j), off // BR_WIDTH))
    out_rows = pl.BlockSpec((tm, BR_WIDTH), lambda b, j: (row(b, j), 0))
    wide_out = lambda dt: jax.ShapeDtypeStruct((t, BR_WIDTH), dt)
    return pl.pallas_call(
        functools.partial(_fox_prep_kernel, tm=tm),
        out_shape=(wide_out(q_dtype), wide_out(F32), wide_out(F32), wide_out(BF16), wide_out(BF16),
                   jax.ShapeDtypeStruct((t, N_HEADS), F32),
                   jax.ShapeDtypeStruct((t, LANES), F32),
                   jax.ShapeDtypeStruct((nb, LANES, s), F32)),
        grid=(nb, nj),
        in_specs=[wide(C_FQ), wide(C_FK), wide(C_FV),
                  pl.BlockSpec((tm, LANES), lambda b, j: (row(b, j), C_SMALL // LANES)),
                  pl.BlockSpec((1, HEAD_DIM), lambda b, j: (0, 0)),
                  pl.BlockSpec((1, HEAD_DIM), lambda b, j: (0, 0))],
        out_specs=(out_rows, out_rows, out_rows, out_rows, out_rows,
                   pl.BlockSpec((tm, N_HEADS), lambda b, j: (row(b, j), 0)),
                   pl.BlockSpec((tm, LANES), lambda b, j: (row(b, j), 0)),
                   pl.BlockSpec((None, LANES, tm), lambda b, j: (b, 0, j))),
        scratch_shapes=[pltpu.VMEM((1, LANES), F32)],
        compiler_params=_params("parallel", "arbitrary"),
        name="fox_prep",
    )(proj, proj, proj, proj, qn_w, kn_w)


def _fox_flash_kernel(q_ref, k_ref, v_ref, cq_ref, ck_ref, g_ref, o_ref, m_ref, acc_ref, cqb_ref, *, tq):
    i = pl.program_id(1)
    j = pl.program_id(2)
    lane_tiles = tq // LANES

    @pl.when(j == 0)
    def _():
        m_ref[...] = jnp.full_like(m_ref, NEG_BIG)
        acc_ref[...] = jnp.zeros_like(acc_ref)
        for h in range(N_HEADS):
            cqb_ref[h] = jnp.broadcast_to(cq_ref[:, SM_FORGET + h:SM_FORGET + h + 1], (tq, LANES))

    def update(diagonal):
        ones = jnp.ones((tq, HEAD_DIM), BF16)
        for h in range(N_HEADS):
            s = _dot_nt(q_ref[:, _head(h)], k_ref[:, _head(h)]) - ck_ref[h:h + 1, :]
            if diagonal:
                s = jnp.where(_iota2((tq, tq), 0) >= _iota2((tq, tq), 1), s, NEG_BIG)
            cqb = cqb_ref[h]
            m_prev = m_ref[h]
            m_new = jnp.maximum(m_prev, jnp.max(s, axis=-1, keepdims=True) + cqb)
            shift = m_new - cqb
            p = jnp.exp((s - jnp.concatenate([shift] * lane_tiles, axis=1)).astype(BF16))
            alpha = jnp.exp(m_prev - m_new)
            v_ones = jnp.concatenate([v_ref[:, _head(h)], ones], axis=1)
            acc_ref[h] = (jnp.concatenate([alpha, alpha], axis=1) * acc_ref[h]
                          + jnp.dot(p, v_ones, preferred_element_type=F32))
            m_ref[h] = m_new

    @pl.when(j < i)
    def _():
        update(False)

    @pl.when(j == i)
    def _():
        update(True)
        for h in range(N_HEADS):
            acc = acc_ref[h]
            o = acc[:, :HEAD_DIM] / acc[:, HEAD_DIM:]
            o_ref[:, _head(h)] = (o * _silu(g_ref[:, _head(h)])).astype(BF16)


def _fox_flash(qn, kn, v, c, ct, proj, nb, s):
    t = qn.shape[0]
    tq = min(s, 512)
    nq = s // tq
    qrow = lambda b, i, j: (b * nq + i, 0)
    krow = lambda b, i, j: (b * nq + jnp.minimum(i, j), 0)
    stat = pltpu.VMEM((N_HEADS, tq, LANES), F32)
    return pl.pallas_call(
        functools.partial(_fox_flash_kernel, tq=tq),
        out_shape=jax.ShapeDtypeStruct((t, BR_WIDTH), BF16),
        grid=(nb, nq, nq),
        in_specs=[pl.BlockSpec((tq, BR_WIDTH), qrow),
                  pl.BlockSpec((tq, BR_WIDTH), krow),
                  pl.BlockSpec((tq, BR_WIDTH), krow),
                  pl.BlockSpec((tq, LANES), qrow),
                  pl.BlockSpec((None, SUBLANES, tq), lambda b, i, j: (b, SM_FORGET // SUBLANES, jnp.minimum(i, j))),
                  pl.BlockSpec((tq, BR_WIDTH), lambda b, i, j: (b * nq + i, C_FG // BR_WIDTH))],
        out_specs=pl.BlockSpec((tq, BR_WIDTH), qrow),
        scratch_shapes=[stat, pltpu.VMEM((N_HEADS, tq, 2 * HEAD_DIM), F32), stat],
        compiler_params=_params("parallel", "parallel", "arbitrary"),
        name="fox_flash",
    )(qn, kn, v, c, ct, proj)


def _mem_kv_kernel(m_ref, nw_ref, w_ref, kw_ref, mk_ref, mv_ref):
    kv = _dot(_rms(m_ref[...], nw_ref[...]), w_ref[...])
    for h in range(N_HEADS):
        mk_ref[:, _head(h)] = _rms(kv[:, _head(h)], kw_ref[...])
    mv_ref[...] = kv[:, BR_WIDTH:]


def _mem_kv(mem, norm_w, w_kv, kn_w):
    t, d = mem.shape
    tm = min(t, 256)
    return pl.pallas_call(
        _mem_kv_kernel,
        out_shape=(jax.ShapeDtypeStruct((t, BR_WIDTH), F32), jax.ShapeDtypeStruct((t, BR_WIDTH), F32)),
        grid=(t // tm,),
        in_specs=[pl.BlockSpec((tm, d), lambda i: (i, 0)),
                  pl.BlockSpec((1, d), lambda i: (0, 0)),
                  pl.BlockSpec((d, 2 * BR_WIDTH), lambda i: (0, 0)),
                  pl.BlockSpec((1, HEAD_DIM), lambda i: (0, 0))],
        out_specs=(pl.BlockSpec((tm, BR_WIDTH), lambda i: (i, 0)),
                   pl.BlockSpec((tm, BR_WIDTH), lambda i: (i, 0))),
        compiler_params=_params("parallel"),
        name="mem_kv",
    )(mem, norm_w, w_kv, kn_w)


def _mem_attn_kernel(q_ref, g_ref, k_ref, v_ref, qw_ref, o_ref):
    scale = HEAD_DIM ** -0.5
    for h in range(N_HEADS):
        q = _rms(q_ref[:, _head(h)], qw_ref[...]) * scale
        s = _dot_nt(q, k_ref[:, _head(h)])
        p = jnp.exp(s - jnp.max(s, axis=-1, keepdims=True))
        o = _dot(p, v_ref[:, _head(h)]) / jnp.sum(p, axis=-1, keepdims=True)
        o_ref[:, _head(h)] = (o * _silu(g_ref[:, _head(h)])).astype(BF16)


def _mem_attn(proj, mk, mv, qn_w, nb, s):
    t = proj.shape[0]
    m = mk.shape[0] // nb
    tq = min(s, 512)
    nq = s // tq
    return pl.pallas_call(
        _mem_attn_kernel,
        out_shape=jax.ShapeDtypeStruct((t, BR_WIDTH), BF16),
        grid=(nb, nq),
        in_specs=[pl.BlockSpec((tq, BR_WIDTH), lambda b, i: (b * nq + i, C_MQ // BR_WIDTH)),
                  pl.BlockSpec((tq, BR_WIDTH), lambda b, i: (b * nq + i, C_MG // BR_WIDTH)),
                  pl.BlockSpec((m, BR_WIDTH), lambda b, i: (b, 0)),
                  pl.BlockSpec((m, BR_WIDTH), lambda b, i: (b, 0)),
                  pl.BlockSpec((1, HEAD_DIM), lambda b, i: (0, 0))],
        out_specs=pl.BlockSpec((tq, BR_WIDTH), lambda b, i: (b * nq + i, 0)),
        compiler_params=_params("parallel", "parallel"),
        name="mem_attn",
    )(proj, proj, mk, mv, qn_w)


def _causal_conv_tile(x, halo, w_ref):
    rows8 = _iota2(halo.shape, 0)
    y = x * w_ref[CONV_W - 1:CONV_W, :]
    for k in range(1, CONV_W):
        xr = pltpu.roll(x, k, axis=0)
        hr = pltpu.roll(halo, k, axis=0)
        top = jnp.where(rows8 < k, hr, xr[:SUBLANES])
        xs = jnp.concatenate([top, xr[SUBLANES:]], axis=0)
        y = y + xs * w_ref[CONV_W - 1 - k:CONV_W - k, :]
    return y


def _unit_lower_inverses(mats, row, col):
    n = mats[0].shape[0]

    def off_block(shift):
        return (((row >> (shift + 1)) == (col >> (shift + 1)))
                & (((row >> shift) & 1) == 1) & (((col >> shift) & 1) == 0))

    eye = (row == col).astype(F32)
    first = off_block(0)
    ds = [eye - jnp.where(first, a, 0.0) for a in mats]
    shift = 1
    while (1 << shift) < n:
        mask = off_block(shift)
        ts = [_dot(jnp.where(mask, a, 0.0), d) for a, d in zip(mats, ds)]
        ds = [d - _dot(d, t) for d, t in zip(ds, ts)]
        shift += 1
    return ds


def _dn_kernel(x_ref, z_ref, sm_ref, cw_ref, al_ref, dt_ref, ow_ref,
               o_ref, st_ref, stt_ref, halo_ref, *, tl):
    j = pl.program_id(1)

    @pl.when(j == 0)
    def _():
        stt_ref[...] = jnp.zeros_like(stt_ref)
        halo_ref[...] = jnp.zeros_like(halo_ref)

    x = x_ref[...]
    qkv = _silu(_causal_conv_tile(x, halo_ref[...], cw_ref))
    halo_ref[...] = x[tl - SUBLANES:, :]
    sm = sm_ref[...]
    beta_all = _sigmoid(sm)
    la_all = -jnp.exp(al_ref[...]) * _softplus(sm + dt_ref[...])

    cc = DN_CHUNK
    row = _iota2((cc, cc), 0)
    col = _iota2((cc, cc), 1)
    incl = row >= col
    strict = row > col
    tri = incl.astype(F32)
    kd = N_HEADS * HEAD_DIM

    systems = []
    for c in range(tl // cc):
        rows = slice(c * cc, (c + 1) * cc)
        g_all = _dot_exact(tri, la_all[rows])
        gt_all = g_all.T
        eg_all = jnp.exp(g_all)
        for h in range(N_HEADS):
            q = _l2norm(qkv[rows, h * HEAD_DIM:(h + 1) * HEAD_DIM]) * (HEAD_DIM ** -0.5)
            k = _l2norm(qkv[rows, kd + h * HEAD_DIM:kd + (h + 1) * HEAD_DIM])
            v = qkv[rows, 2 * kd + h * HEAD_DIM:2 * kd + (h + 1) * HEAD_DIM]
            beta = beta_all[rows, SM_BETA + h:SM_BETA + h + 1]
            gc = g_all[:, SM_DECAY + h:SM_DECAY + h + 1]
            gr = gt_all[SM_DECAY + h:SM_DECAY + h + 1, :]
            eg = eg_all[:, SM_DECAY + h:SM_DECAY + h + 1]
            g_last = g_all[cc - 1:cc, SM_DECAY + h:SM_DECAY + h + 1]
            decay = jnp.exp(jnp.where(incl, gc - gr, NEG_BIG))
            kb = k.astype(BF16)
            qb = q.astype(BF16)
            systems.append(dict(
                a=jnp.where(strict, beta * decay * _dot_nt(kb, kb), 0.0),
                qk=(decay * _dot_nt(qb, kb)).astype(BF16),
                rhs=jnp.concatenate([beta * v, (beta * eg) * k], axis=1).astype(BF16),
                kdec_t=(jnp.exp(g_last - gc) * k).T.astype(BF16),
                q=qb, eg=eg, g_end=jnp.exp(g_last)))

    ds = _unit_lower_inverses([sy["a"] for sy in systems], row, col)
    sols = [_dot(d, sy["rhs"]) for d, sy in zip(ds, systems)]

    heads = range(N_HEADS)
    stts = [stt_ref[h] for h in heads]
    for c in range(tl // cc):
        rows = slice(c * cc, (c + 1) * cc)
        sys_c = systems[c * N_HEADS:(c + 1) * N_HEADS]
        sol_c = sols[c * N_HEADS:(c + 1) * N_HEADS]
        us = [sol_c[h][:, :HEAD_DIM] - _dot(sol_c[h][:, HEAD_DIM:], stts[h]) for h in heads]
        os = [sys_c[h]["eg"] * _dot(sys_c[h]["q"], stts[h]) + _dot(sys_c[h]["qk"], us[h]) for h in heads]
        stts = [sys_c[h]["g_end"] * stts[h] + _dot(sys_c[h]["kdec_t"], us[h]) for h in heads]
        for h in heads:
            o = _rms(os[h], ow_ref[...]) * _silu(z_ref[rows, _head(h)])
            o_ref[rows, _head(h)] = o.astype(BF16)
    for h in heads:
        stt_ref[h] = stts[h]

    @pl.when(j == pl.num_programs(1) - 1)
    def _():
        for h in range(N_HEADS):
            st_ref[h] = stt_ref[h].T


def _dn_prompt(proj, conv_w, al_row, dt_row, onorm_w, nb, s):
    t = proj.shape[0]
    tl = min(s, 512)
    nj = s // tl
    qkv_w = 3 * BR_WIDTH
    row = lambda b, j: b * nj + j
    return pl.pallas_call(
        functools.partial(_dn_kernel, tl=tl),
        out_shape=(jax.ShapeDtypeStruct((t, BR_WIDTH), BF16),
                   jax.ShapeDtypeStruct((nb, N_HEADS, HEAD_DIM, HEAD_DIM), F32)),
        grid=(nb, nj),
        in_specs=[pl.BlockSpec((tl, qkv_w), lambda b, j: (row(b, j), 0)),
                  pl.BlockSpec((tl, BR_WIDTH), lambda b, j: (row(b, j), C_DNZ // BR_WIDTH)),
                  pl.BlockSpec((tl, LANES), lambda b, j: (row(b, j), C_SMALL // LANES)),
                  pl.BlockSpec((CONV_W, qkv_w), lambda b, j: (0, 0)),
                  pl.BlockSpec((1, LANES), lambda b, j: (0, 0)),
                  pl.BlockSpec((1, LANES), lambda b, j: (0, 0)),
                  pl.BlockSpec((1, HEAD_DIM), lambda b, j: (0, 0))],
        out_specs=(pl.BlockSpec((tl, BR_WIDTH), lambda b, j: (row(b, j), 0)),
                   pl.BlockSpec((None, N_HEADS, HEAD_DIM, HEAD_DIM), lambda b, j: (b, 0, 0, 0))),
        scratch_shapes=[pltpu.VMEM((N_HEADS, HEAD_DIM, HEAD_DIM), F32),
                        pltpu.VMEM((SUBLANES, qkv_w), F32)],
        compiler_params=_params("parallel", "arbitrary"),
        name="dn_prompt",
    )(proj, proj, proj, conv_w, al_row, dt_row, onorm_w)


def _lru_gates(xc, wr_ref, br_ref, wi_ref, bi_ref, lam_ref):
    r = _sigmoid(_dot(xc, wr_ref[...]) + br_ref[...])
    i = _sigmoid(_dot(xc, wi_ref[...]) + bi_ref[...])
    log_a = -LRU_C * r * _softplus(-lam_ref[...])
    a = jnp.exp(log_a)
    one_minus_a2 = -jnp.tanh(log_a) * (a * a + 1.0)
    return a, jnp.sqrt(one_minus_a2) * (i * xc)


def _lru_kernel(x_ref, g_ref, cw_ref, cb_ref, wr_ref, br_ref, wi_ref, bi_ref, lam_ref,
                o_ref, hl_ref, hc_ref, halo_ref, *, tl):
    @pl.when(pl.program_id(1) == 0)
    def _():
        hc_ref[...] = jnp.zeros_like(hc_ref)
        halo_ref[...] = jnp.zeros_like(halo_ref)

    x = x_ref[...]
    xc = _causal_conv_tile(x, halo_ref[...], cw_ref) + cb_ref[...]
    halo_ref[...] = x[tl - SUBLANES:, :]
    a, b = _lru_gates(xc, wr_ref, br_ref, wi_ref, bi_ref, lam_ref)
    rows = _iota2(a.shape, 0)
    d = 1
    while d < tl:
        a_up = jnp.where(rows >= d, pltpu.roll(a, d, axis=0), 1.0)
        b_up = jnp.where(rows >= d, pltpu.roll(b, d, axis=0), 0.0)
        b = a * b_up + b
        a = a * a_up
        d *= 2
    hseq = a * hc_ref[...] + b
    hc_ref[...] = hseq[tl - 1:tl, :]
    hl_ref[...] = hseq[tl - 1:tl, :]
    o_ref[...] = (hseq * _silu(g_ref[...])).astype(BF16)


def _lru_prompt(proj, conv_w, conv_b, w_r, b_r, w_i, b_i, lam, nb, s):
    t = proj.shape[0]
    tl = min(s, 256)
    nj = s // tl
    row = lambda b, j: b * nj + j
    vec = pl.BlockSpec((1, BR_WIDTH), lambda b, j: (0, 0))
    mat = pl.BlockSpec((BR_WIDTH, BR_WIDTH), lambda b, j: (0, 0))
    return pl.pallas_call(
        functools.partial(_lru_kernel, tl=tl),
        out_shape=(jax.ShapeDtypeStruct((t, BR_WIDTH), BF16),
                   jax.ShapeDtypeStruct((nb, 1, BR_WIDTH), F32)),
        grid=(nb, nj),
        in_specs=[pl.BlockSpec((tl, BR_WIDTH), lambda b, j: (row(b, j), C_LRUX // BR_WIDTH)),
                  pl.BlockSpec((tl, BR_WIDTH), lambda b, j: (row(b, j), C_LRUG // BR_WIDTH)),
                  pl.BlockSpec((CONV_W, BR_WIDTH), lambda b, j: (0, 0)),
                  vec, mat, vec, mat, vec, vec],
        out_specs=(pl.BlockSpec((tl, BR_WIDTH), lambda b, j: (row(b, j), 0)),
                   pl.BlockSpec((None, 1, BR_WIDTH), lambda b, j: (b, 0, 0))),
        scratch_shapes=[pltpu.VMEM((1, BR_WIDTH), F32), pltpu.VMEM((SUBLANES, BR_WIDTH), F32)],
        compiler_params=_params("parallel", "arbitrary"),
        name="lru_prompt",
    )(proj, proj, conv_w, conv_b, w_r, b_r, w_i, b_i, lam)


def _merge_kernel(x_ref, nw_ref, a_ref, b_ref, c_ref, m_ref, wg_ref, bg_ref, wb_ref, wo_ref, y_ref):
    x = x_ref[...]
    d = x.shape[1]
    h = _rms(x, nw_ref[...]).astype(BF16)
    merged = None
    for n, br in enumerate((a_ref, b_ref, c_ref, m_ref)):
        cols = slice(n * d, (n + 1) * d)
        gate = jnp.dot(h, wg_ref[:, cols], preferred_element_type=F32) + bg_ref[:, cols]
        up = jnp.dot(br[...], wb_ref[n], preferred_element_type=F32)
        term = _sigmoid(gate) * up
        merged = term if merged is None else merged + term
    y_ref[...] = x + jnp.dot(merged.astype(BF16), wo_ref[...], preferred_element_type=F32)


def _merge(x, norm_w, out_a, out_b, out_c, out_m, w_gate, b_gate, w_branch, w_out, layer):
    t, d = x.shape
    tm = min(t, 512)
    rows = lambda w: pl.BlockSpec((tm, w), lambda i: (i, 0))
    return pl.pallas_call(
        _merge_kernel,
        out_shape=jax.ShapeDtypeStruct((t, d), F32),
        grid=(t // tm,),
        in_specs=[rows(d), pl.BlockSpec((1, d), lambda i: (0, 0)),
                  rows(BR_WIDTH), rows(BR_WIDTH), rows(BR_WIDTH), rows(BR_WIDTH),
                  pl.BlockSpec((None, d, N_GATE), lambda i: (layer, 0, 0)),
                  pl.BlockSpec((1, N_GATE), lambda i: (0, 0)),
                  pl.BlockSpec((N_BRANCH, BR_WIDTH, d), lambda i: (0, 0, 0)),
                  pl.BlockSpec((d, d), lambda i: (0, 0))],
        out_specs=rows(d),
        compiler_params=_params("parallel"),
        name="merge",
    )(x, norm_w, out_a, out_b, out_c, out_m, w_gate, b_gate, w_branch, w_out)


def _sample_step_kernel(qkv_ref, z_ref, lx_ref, lg_ref, sm_ref, st_ref, dbuf_ref, lh_ref, lbuf_ref,
                        dcw_ref, al_ref, dt_ref, ow_ref, lcw_ref, lcb_ref, wr_ref, br_ref, wi_ref, bi_ref, lam_ref,
                        oa_ref, ob_ref, sto_ref, dbo_ref, lho_ref, lbo_ref, o_scr, *, bs):
    kd = N_HEADS * HEAD_DIM

    def step_conv(x, buf_ref, bufo_ref, w_ref):
        y = x * w_ref[CONV_W - 1:CONV_W, :]
        for k in range(CONV_W - 1):
            y = y + buf_ref[:, k, :] * w_ref[k:k + 1, :]
        for k in range(CONV_W - 2):
            bufo_ref[:, k, :] = buf_ref[:, k + 1, :]
        bufo_ref[:, CONV_W - 2, :] = x
        return y

    qkv = _silu(step_conv(qkv_ref[...], dbuf_ref, dbo_ref, dcw_ref))
    sm = sm_ref[...]
    beta_all = _sigmoid(sm)
    eg_all = jnp.exp(-jnp.exp(al_ref[...]) * _softplus(sm + dt_ref[...]))
    eye = _iota2((HEAD_DIM, HEAD_DIM), 0) == _iota2((HEAD_DIM, HEAD_DIM), 1)
    pad = jnp.zeros((SUBLANES - 2, HEAD_DIM), F32)
    items = []
    for h in range(N_HEADS):
        q = _l2norm(qkv[:, h * HEAD_DIM:(h + 1) * HEAD_DIM]) * (HEAD_DIM ** -0.5)
        k = _l2norm(qkv[:, kd + h * HEAD_DIM:kd + (h + 1) * HEAD_DIM])
        v = qkv[:, 2 * kd + h * HEAD_DIM:2 * kd + (h + 1) * HEAD_DIM]
        beta = beta_all[:, SM_BETA + h:SM_BETA + h + 1]
        eg = eg_all[:, SM_DECAY + h:SM_DECAY + h + 1]
        qk = jnp.sum(q * k, axis=-1, keepdims=True)
        for b in range(bs):
            one = slice(b, b + 1)
            items.append(dict(b=b, h=h, q=q[one], k=k[one], v=v[one], beta=beta[one], eg=eg[one], qk=qk[one]))
    sk_sqs = [_dot_nt(jnp.concatenate([it["k"], it["q"], pad], axis=0), st_ref[it["b"], it["h"]])
              for it in items]
    us = [it["beta"] * (it["v"] - it["eg"] * r[0:1]) for it, r in zip(items, sk_sqs)]
    for it, r, u in zip(items, sk_sqs, us):
        o_scr[it["b"]:it["b"] + 1, _head(it["h"])] = it["eg"] * r[1:2] + it["qk"] * u
    updates = [_dot(jnp.where(eye, jnp.broadcast_to(u, (HEAD_DIM, HEAD_DIM)), 0.0),
                    jnp.broadcast_to(it["k"], (HEAD_DIM, HEAD_DIM))) for it, u in zip(items, us)]
    for it, upd in zip(items, updates):
        sto_ref[it["b"], it["h"]] = it["eg"] * st_ref[it["b"], it["h"]] + upd
    o = o_scr[...]
    for h in range(N_HEADS):
        oa_ref[:, _head(h)] = (_rms(o[:, _head(h)], ow_ref[...]) * _silu(z_ref[:, _head(h)])).astype(BF16)

    xc = step_conv(lx_ref[...], lbuf_ref, lbo_ref, lcw_ref) + lcb_ref[...]
    a, bx = _lru_gates(xc, wr_ref, br_ref, wi_ref, bi_ref, lam_ref)
    hnew = a * lh_ref[...] + bx
    lho_ref[...] = hnew
    ob_ref[...] = (hnew * _silu(lg_ref[...])).astype(BF16)


def _sample_step(proj, layer, state_dn, state_dn_conv, state_lru_h, state_lru_conv,
                 dn_conv_w, al_row, dt_row, onorm_w, lru_conv_w, lru_conv_b, w_r, b_r, w_i, b_i, lam):
    nb = proj.shape[0]
    bs = SUBLANES
    qkv_w = 3 * BR_WIDTH
    wide = lambda off: pl.BlockSpec((bs, BR_WIDTH), lambda i: (i, off // BR_WIDTH))
    const = lambda shape: pl.BlockSpec(shape, lambda i: (0,) * len(shape))
    st_in = pl.BlockSpec((None, bs, N_HEADS, HEAD_DIM, HEAD_DIM), lambda i: (layer, i, 0, 0, 0))
    dbuf_in = pl.BlockSpec((None, bs, CONV_W - 1, qkv_w), lambda i: (layer, i, 0, 0))
    lh_in = pl.BlockSpec((None, bs, BR_WIDTH), lambda i: (layer, i, 0))
    lbuf_in = pl.BlockSpec((None, bs, CONV_W - 1, BR_WIDTH), lambda i: (layer, i, 0, 0))
    return pl.pallas_call(
        functools.partial(_sample_step_kernel, bs=bs),
        out_shape=(jax.ShapeDtypeStruct((nb, BR_WIDTH), BF16),
                   jax.ShapeDtypeStruct((nb, BR_WIDTH), BF16),
                   jax.ShapeDtypeStruct((nb, N_HEADS, HEAD_DIM, HEAD_DIM), F32),
                   jax.ShapeDtypeStruct((nb, CONV_W - 1, qkv_w), F32),
                   jax.ShapeDtypeStruct((nb, BR_WIDTH), F32),
                   jax.ShapeDtypeStruct((nb, CONV_W - 1, BR_WIDTH), F32)),
        grid=(nb // bs,),
        in_specs=[pl.BlockSpec((bs, qkv_w), lambda i: (i, 0)), wide(C_DNZ), wide(C_LRUX), wide(C_LRUG),
                  pl.BlockSpec((bs, LANES), lambda i: (i, C_SMALL // LANES)),
                  st_in, dbuf_in, lh_in, lbuf_in,
                  const((CONV_W, qkv_w)), const((1, LANES)), const((1, LANES)), const((1, HEAD_DIM)),
                  const((CONV_W, BR_WIDTH)), const((1, BR_WIDTH)),
                  const((BR_WIDTH, BR_WIDTH)), const((1, BR_WIDTH)),
                  const((BR_WIDTH, BR_WIDTH)), const((1, BR_WIDTH)), const((1, BR_WIDTH))],
        out_specs=(pl.BlockSpec((bs, BR_WIDTH), lambda i: (i, 0)),
                   pl.BlockSpec((bs, BR_WIDTH), lambda i: (i, 0)),
                   pl.BlockSpec((bs, N_HEADS, HEAD_DIM, HEAD_DIM), lambda i: (i, 0, 0, 0)),
                   pl.BlockSpec((bs, CONV_W - 1, qkv_w), lambda i: (i, 0, 0)),
                   pl.BlockSpec((bs, BR_WIDTH), lambda i: (i, 0)),
                   pl.BlockSpec((bs, CONV_W - 1, BR_WIDTH), lambda i: (i, 0, 0))),
        scratch_shapes=[pltpu.VMEM((bs, BR_WIDTH), F32)],
        compiler_params=_params("parallel"),
        name="sample_step",
    )(proj, proj, proj, proj, proj, state_dn, state_dn_conv, state_lru_h, state_lru_conv,
      dn_conv_w, al_row, dt_row, onorm_w, lru_conv_w, lru_conv_b, w_r, b_r, w_i, b_i, lam)


def _own_head_mask(n_rows_kv):
    shape = (SUBLANES, n_rows_kv)
    return (_iota2(shape, 1) % N_HEADS) == _iota2(shape, 0)


def _heads_to_row(o8):
    return jnp.concatenate([o8[h:h + 1] for h in range(N_HEADS)], axis=1)


def _split3(x):
    hi = x.astype(BF16)
    r1 = x - hi.astype(F32)
    mid = r1.astype(BF16)
    lo = (r1 - mid.astype(F32)).astype(BF16)
    return hi, mid, lo


def _pool_bias_kernel(lf_ref, after_ref, same_ref, inpage_ref, total_ref):
    in_page = None
    total = None
    for h in range(N_HEADS):
        for piece in _split3(lf_ref[:, h, :]):
            a = jnp.dot(piece, after_ref[h], preferred_element_type=F32)
            t = jnp.dot(piece, same_ref[h], preferred_element_type=F32)
            in_page = a if in_page is None else in_page + a
            total = t if total is None else total + t
    inpage_ref[...] = in_page
    total_ref[...] = total


def _pool_bias(cache_lf):
    n, nh, page = cache_lf.shape
    w = page * nh
    tp = min(n, 512)
    src_tok = jnp.arange(page)[None, :, None]
    src_head = jnp.arange(nh)[:, None, None]
    dst = jnp.arange(w)[None, None, :]
    same = (dst % nh) == src_head
    after = same & (src_tok > dst // nh)
    rows = pl.BlockSpec((tp, w), lambda i: (i, 0))
    const = pl.BlockSpec((nh, page, w), lambda i: (0, 0, 0))
    return pl.pallas_call(
        _pool_bias_kernel,
        out_shape=(jax.ShapeDtypeStruct((n, w), F32), jax.ShapeDtypeStruct((n, w), F32)),
        grid=(n // tp,),
        in_specs=[pl.BlockSpec((tp, nh, page), lambda i: (i, 0, 0)), const, const],
        out_specs=(rows, rows),
        compiler_params=_params("parallel"),
        name="pool_bias",
    )(cache_lf, after.astype(BF16), jnp.broadcast_to(same, after.shape).astype(BF16))


def _fox_decode_kernel(pt_ref, q_ref, kn_ref, vn_ref, lfn_ref, g_ref, *rest, n_pages):
    k_refs = rest[:n_pages]
    v_refs = rest[n_pages:2 * n_pages]
    inpage_refs = rest[2 * n_pages:3 * n_pages]
    total_refs = rest[3 * n_pages:4 * n_pages]
    o_ref = rest[4 * n_pages]
    del pt_ref
    q8 = q_ref[...]
    qb = q8.astype(BF16)
    own = _own_head_mask(k_refs[0].shape[0])

    later = lfn_ref[...]
    scores = [None] * n_pages
    for p in reversed(range(n_pages)):
        s = _dot_nt(qb, k_refs[p][...]) + (inpage_refs[p][...] + later)
        scores[p] = jnp.where(own, s, NEG_BIG)
        later = later + total_refs[p][...]
    s_self = jnp.sum(q8 * kn_ref[...], axis=-1, keepdims=True)
    m = s_self
    for s in scores:
        m = jnp.maximum(m, jnp.max(s, axis=-1, keepdims=True))
    p_self = jnp.exp(s_self - m)
    l = p_self
    acc = p_self * vn_ref[...]
    for p in range(n_pages):
        pr = jnp.exp(scores[p] - m)
        l = l + jnp.sum(pr, axis=-1, keepdims=True)
        acc = acc + _dot(pr, v_refs[p][...])
    o_ref[...] = (_heads_to_row(acc / l) * _silu(g_ref[...])).astype(BF16)


def _fox_decode(q8, k8, v8, lf_new, fg, cache_k, cache_v, in_page, totals, page_table, layer, n_pool):
    nb = q8.shape[0]
    n_pages = page_table.shape[0] // nb
    kv_rows = cache_k.shape[1]
    base = layer * n_pool
    row = pl.BlockSpec((None, 1, BR_WIDTH), lambda b, pt: (b, 0, 0))
    heads = pl.BlockSpec((None, SUBLANES, HEAD_DIM), lambda b, pt: (b, 0, 0))

    def paged(shape, p):
        return pl.BlockSpec((None,) + shape, lambda b, pt: (base + pt[b * n_pages + p], 0, 0))

    in_specs = [heads, heads, heads, pl.BlockSpec((None, 1, kv_rows), lambda b, pt: (b, 0, 0)), row]
    in_specs += [paged((kv_rows, HEAD_DIM), p) for p in range(n_pages)]
    in_specs += [paged((kv_rows, HEAD_DIM), p) for p in range(n_pages)]
    in_specs += [paged((1, kv_rows), p) for p in range(n_pages)]
    in_specs += [paged((1, kv_rows), p) for p in range(n_pages)]
    grid_spec = pltpu.PrefetchScalarGridSpec(
        num_scalar_prefetch=1,
        grid=(nb,),
        in_specs=in_specs,
        out_specs=row)
    return pl.pallas_call(
        functools.partial(_fox_decode_kernel, n_pages=n_pages),
        out_shape=jax.ShapeDtypeStruct((nb, 1, BR_WIDTH), BF16),
        grid_spec=grid_spec,
        compiler_params=_params("parallel"),
        name="fox_decode",
    )(page_table, q8, k8, v8, lf_new, fg,
      *([cache_k] * n_pages), *([cache_v] * n_pages), *([in_page] * n_pages), *([totals] * n_pages))


def _mem_decode_kernel(q_ref, g_ref, k_ref, v_ref, qw_ref, o_ref, o_scr, *, bs):
    own = _own_head_mask(k_ref.shape[1])
    scale = HEAD_DIM ** -0.5
    qs = [_rms(q_ref[:, _head(h)], qw_ref[...]) * scale for h in range(N_HEADS)]
    pad = jnp.zeros((SUBLANES - N_HEADS, HEAD_DIM), F32)
    for b in range(bs):
        q8 = jnp.concatenate([q[b:b + 1] for q in qs] + [pad], axis=0)
        s = jnp.where(own, _dot_nt(q8, k_ref[b]), NEG_BIG)
        p = jnp.exp(s - jnp.max(s, axis=-1, keepdims=True))
        o = _dot(p, v_ref[b]) / jnp.sum(p, axis=-1, keepdims=True)
        o_scr[b:b + 1, :] = _heads_to_row(o)
    o_ref[...] = (o_scr[...] * _silu(g_ref[...])).astype(BF16)


def _mem_decode(proj, cache_k, cache_v, qn_w, layer):
    nb = proj.shape[0]
    bs = SUBLANES
    m = cache_k.shape[1]
    base = layer * (nb // bs)
    kv = pl.BlockSpec((bs, m, HEAD_DIM), lambda i: (base + i, 0, 0))
    return pl.pallas_call(
        functools.partial(_mem_decode_kernel, bs=bs),
        out_shape=jax.ShapeDtypeStruct((nb, BR_WIDTH), BF16),
        grid=(nb // bs,),
        in_specs=[pl.BlockSpec((bs, BR_WIDTH), lambda i: (i, C_MQ // BR_WIDTH)),
                  pl.BlockSpec((bs, BR_WIDTH), lambda i: (i, C_MG // BR_WIDTH)),
                  kv, kv, pl.BlockSpec((1, HEAD_DIM), lambda i: (0, 0))],
        out_specs=pl.BlockSpec((bs, BR_WIDTH), lambda i: (i, 0)),
        scratch_shapes=[pltpu.VMEM((bs, BR_WIDTH), F32)],
        compiler_params=_params("parallel"),
        name="mem_decode",
    )(proj, proj, cache_k, cache_v, qn_w)


def _permute_columns(a):
    runs = sorted(COLUMN_RUNS, key=lambda r: r[2])
    pad = jnp.zeros(a.shape[:-1] + (N_PERM - C_SMALL - N_SMALL,), a.dtype)
    return jnp.concatenate([a[..., start:stop] for start, stop, _ in runs] + [pad], axis=-1)


def _block_diag(w):
    nblk, e, f = w.shape
    eye = jnp.eye(nblk, dtype=w.dtype)
    return (eye[:, None, :, None] * w[:, :, None, :]).reshape(nblk * e, nblk * f)


def _decay_lane_row(v):
    return jnp.zeros((1, LANES), F32).at[0, SM_DECAY:SM_DECAY + N_HEADS].set(v)


def kernel(x_prompt, x_sample, cache_fox_k, cache_fox_v, cache_fox_logf, cache_mem_k, cache_mem_v, state_dn, state_dn_conv, state_lru_h, state_lru_conv, page_table, mem_prompt, norm_w, w_in, b_in, dn_conv_w, dn_A_log, dn_dt_bias, dn_onorm_w, lru_conv_w, lru_conv_b, lru_w_r, lru_b_r, lru_w_i, lru_b_i, lru_lambda, fox_qn_w, fox_kn_w, mem_norm_w, w_mem_kv, mem_qn_w, mem_kn_w, w_branch, w_out):
    bp, s, d = x_prompt.shape
    bd = x_sample.shape[0]
    depth = w_in.shape[0]
    n_pool, page = cache_fox_k.shape[1], cache_fox_k.shape[2]
    mem_tokens = mem_prompt.shape[1]
    tp = bp * s

    yp = x_prompt.reshape(tp, d)
    ys = x_sample.reshape(bd, d)
    mem2 = mem_prompt.reshape(bp * mem_tokens, d)
    kv_rows = page * N_HEADS
    cache_k2 = cache_fox_k.reshape(depth * n_pool, kv_rows, HEAD_DIM)
    cache_v2 = cache_fox_v.reshape(depth * n_pool, kv_rows, HEAD_DIM)
    in_page, totals = _pool_bias(jnp.swapaxes(cache_fox_logf, 2, 3).reshape(depth * n_pool, N_HEADS, page))
    in_page = in_page.reshape(depth * n_pool, 1, kv_rows)
    totals = totals.reshape(depth * n_pool, 1, kv_rows)
    cmem_k2 = cache_mem_k.reshape(depth * bd, mem_tokens * N_HEADS, HEAD_DIM)
    cmem_v2 = cache_mem_v.reshape(depth * bd, mem_tokens * N_HEADS, HEAD_DIM)
    pt_flat = page_table.reshape(-1)

    row = lambda v: v.reshape(1, -1)
    acc = {n: [] for n in ("pk", "pv", "plf", "pmk", "pmv", "pdn", "pdc", "plh", "plc",
                           "sk", "sv", "slf", "sdn", "sdc", "slh", "slc")}
    w_p, w_g = _repack_w_in(w_in)
    for l in range(depth):
        b_p = _permute_columns(b_in[l]).reshape(1, N_PERM)
        b_g = b_in[l, GATE_RUN[0]:GATE_RUN[1]].reshape(1, N_GATE)
        nw = row(norm_w[l])
        al_row = _decay_lane_row(dn_A_log[l])
        dt_row = _decay_lane_row(dn_dt_bias[l])
        ow = row(dn_onorm_w[l])
        wr = _block_diag(lru_w_r[l]).astype(BF16)
        wi = _block_diag(lru_w_i[l]).astype(BF16)
        lru_args = (lru_conv_w[l], row(lru_conv_b[l]), wr, row(lru_b_r[l]), wi, row(lru_b_i[l]), row(lru_lambda[l]))
        wb = w_branch[l].astype(BF16)
        wo = w_out[l].astype(BF16)
        fqw, fkw, mqw = row(fox_qn_w[l]), row(fox_kn_w[l]), row(mem_qn_w[l])

        proj = _inproj(yp, nw, w_p, b_p, l)
        qb, kn, fv, kb, vb, lf, c, ct = _fox_prep(proj, fqw, fkw, bp, s, BF16)
        out_c = _fox_flash(qb, kb, vb, c, ct, proj, bp, s)
        mk, mv = _mem_kv(mem2, row(mem_norm_w[l]), w_mem_kv[l].astype(BF16), row(mem_kn_w[l]))
        out_m = _mem_attn(proj, mk, mv, mqw, bp, s)
        out_a, dn_s = _dn_prompt(proj, dn_conv_w[l], al_row, dt_row, ow, bp, s)
        out_b, lru_h = _lru_prompt(proj, *lru_args, bp, s)
        proj3 = proj.reshape(bp, s, N_PERM)
        acc["pk"].append(kn.reshape(bp, s, N_HEADS, HEAD_DIM))
        acc["pv"].append(fv.reshape(bp, s, N_HEADS, HEAD_DIM))
        acc["plf"].append(lf.reshape(bp, s, N_HEADS))
        acc["pmk"].append(mk.reshape(bp, mem_tokens, N_HEADS, HEAD_DIM))
        acc["pmv"].append(mv.reshape(bp, mem_tokens, N_HEADS, HEAD_DIM))
        acc["pdn"].append(dn_s)
        acc["pdc"].append(proj3[:, s - (CONV_W - 1):, C_DNQKV:C_DNQKV + 3 * BR_WIDTH])
        acc["plh"].append(lru_h.reshape(bp, BR_WIDTH))
        acc["plc"].append(proj3[:, s - (CONV_W - 1):, C_LRUX:C_LRUX + BR_WIDTH])
        yp = _merge(yp, nw, out_a, out_b, out_c, out_m, w_g, b_g, wb, wo, l)

        proj_s = _inproj(ys, nw, w_p, b_p, l)
        qn_s, kn_s, fv_s, _, _, lf_s, _, _ = _fox_prep(proj_s, fqw, fkw, 1, bd, F32)
        head_rows = lambda a: jnp.pad(a.reshape(bd, N_HEADS, HEAD_DIM), ((0, 0), (0, SUBLANES - N_HEADS), (0, 0)))
        lf_new = jnp.tile(lf_s, (1, page)).reshape(bd, 1, kv_rows)
        fg_s = proj_s[:, C_FG:C_FG + BR_WIDTH].reshape(bd, 1, BR_WIDTH)
        out_c_s = _fox_decode(head_rows(qn_s), head_rows(kn_s), head_rows(fv_s), lf_new, fg_s,
                              cache_k2, cache_v2, in_page, totals, pt_flat, l, n_pool).reshape(bd, BR_WIDTH)
        out_m_s = _mem_decode(proj_s, cmem_k2, cmem_v2, mqw, l)
        out_a_s, out_b_s, dn_s_s, dn_c_s, lru_h_s, lru_c_s = _sample_step(
            proj_s, l, state_dn, state_dn_conv, state_lru_h, state_lru_conv,
            dn_conv_w[l], al_row, dt_row, ow, *lru_args)
        acc["sk"].append(kn_s.reshape(bd, 1, N_HEADS, HEAD_DIM))
        acc["sv"].append(fv_s.reshape(bd, 1, N_HEADS, HEAD_DIM))
        acc["slf"].append(lf_s.reshape(bd, 1, N_HEADS))
        acc["sdn"].append(dn_s_s)
        acc["sdc"].append(dn_c_s)
        acc["slh"].append(lru_h_s)
        acc["slc"].append(lru_c_s)
        ys = _merge(ys, nw, out_a_s, out_b_s, out_c_s, out_m_s, w_g, b_g, wb, wo, l)

    st = lambda n: jnp.stack(acc[n])
    return (yp.reshape(bp, s, d), ys.reshape(bd, 1, d),
            st("pk"), st("pv"), st("plf"), st("pmk"), st("pmv"), st("pdn"), st("pdc"), st("plh"), st("plc"),
            st("sk"), st("sv"), st("slf"), st("sdn"), st("sdc"), st("slh"), st("slc"))
```

```python
import functools

import jax
import jax.numpy as jnp
from jax import lax
from jax.experimental import pallas as pl
from jax.experimental.pallas import tpu as pltpu

F32 = jnp.float32
BF16 = jnp.bfloat16
HIGHEST = lax.Precision.HIGHEST

EPS = 1e-6
LRU_C = 8.0
CONV_W = 4
N_HEADS = 4
HEAD_DIM = 128
BR_WIDTH = N_HEADS * HEAD_DIM
N_BRANCH = 4
LANES = 128
SUBLANES = 8
DN_CHUNK = 128
NEG_BIG = -1e30
VMEM_LIMIT_BYTES = 56 * 1024 * 1024

C_DNQKV = 0
C_DNZ = 1536
C_LRUX = 2048
C_LRUG = 2560
C_FQ = 3072
C_FK = 3584
C_FV = 4096
C_FG = 4608
C_MQ = 5120
C_MG = 5632
C_SMALL = 6144
N_PERM = 6400
INPROJ_TN = 1280
N_SMALL = 3 * N_HEADS
N_GATE = 4096
COLUMN_RUNS = ((0, 2048, 0), (2056, 4616, 2048), (4620, 6156, 4608), (2048, 2056, C_SMALL), (4616, 4620, C_SMALL + 8))
GATE_RUN = (6156, 10252)
SM_BETA = 0
SM_DECAY = 4
SM_FORGET = 8

NT_DIMS = (((1,), (1,)), ((), ()))


def _params(*sem):
    return pltpu.CompilerParams(dimension_semantics=sem, vmem_limit_bytes=VMEM_LIMIT_BYTES)


def _rms(x, w):
    return x * lax.rsqrt(jnp.mean(x * x, axis=-1, keepdims=True) + EPS) * w


def _l2norm(x):
    return x * lax.rsqrt(jnp.sum(x * x, axis=-1, keepdims=True) + EPS)


def _sigmoid(x):
    return 0.5 * jnp.tanh(0.5 * x) + 0.5


def _silu(x):
    return x * _sigmoid(x)


def _softplus(x):
    return jnp.maximum(x, 0.0) + jnp.log1p(jnp.exp(-jnp.abs(x)))


def _log_sigmoid(x):
    return -_softplus(-x)


def _dot(a, b):
    return jnp.dot(a.astype(BF16), b.astype(BF16), preferred_element_type=F32)


def _dot_nt(a, b):
    return lax.dot_general(a.astype(BF16), b.astype(BF16), NT_DIMS, preferred_element_type=F32)


def _dot_exact(a, b):
    return jnp.dot(a, b, precision=HIGHEST, preferred_element_type=F32)


def _iota2(shape, axis):
    return lax.broadcasted_iota(jnp.int32, shape, axis)


def _head(h):
    return slice(h * HEAD_DIM, (h + 1) * HEAD_DIM)


def _norm_kernel(x_ref, nw_ref, h_ref):
    h_ref[...] = _rms(x_ref[...], nw_ref[...]).astype(BF16)


def _norm(x, norm_w):
    t, d = x.shape
    tm = min(t, 1024)
    return pl.pallas_call(
        _norm_kernel,
        out_shape=jax.ShapeDtypeStruct((t, d), BF16),
        grid=(t // tm,),
        in_specs=[pl.BlockSpec((tm, d), lambda i: (i, 0)), pl.BlockSpec((1, d), lambda i: (0, 0))],
        out_specs=pl.BlockSpec((tm, d), lambda i: (i, 0)),
        compiler_params=_params("parallel"),
        name="norm",
    )(x, norm_w)


def _inproj_kernel(h_ref, w_ref, b_ref, o_ref):
    o_ref[...] = jnp.dot(h_ref[...], w_ref[...], preferred_element_type=F32) + b_ref[...]


def _repack_kernel(w_ref, o_ref, g_ref):
    def run(start, stop):
        lo = start // LANES * LANES
        hi = min(-(-stop // LANES) * LANES, w_ref.shape[-1])
        return w_ref[:, lo:hi][:, start - lo:stop - lo].astype(BF16)

    o_ref[...] = jnp.zeros_like(o_ref)
    for start, stop, dst in COLUMN_RUNS:
        o_ref[:, dst:dst + stop - start] = run(start, stop)
    g_ref[...] = run(*GATE_RUN)


def _repack_w_in(w_in):
    depth, d, n_in = w_in.shape
    tr = 64
    return pl.pallas_call(
        _repack_kernel,
        out_shape=(jax.ShapeDtypeStruct((depth, d, N_PERM), BF16),
                   jax.ShapeDtypeStruct((depth, d, N_GATE), BF16)),
        grid=(depth, d // tr),
        in_specs=[pl.BlockSpec((None, tr, n_in), lambda l, i: (l, i, 0))],
        out_specs=(pl.BlockSpec((None, tr, N_PERM), lambda l, i: (l, i, 0)),
                   pl.BlockSpec((None, tr, N_GATE), lambda l, i: (l, i, 0))),
        compiler_params=_params("parallel", "parallel"),
        name="repack_w_in",
    )(w_in)


def _inproj(hn, w_all, b, layer):
    t, d = hn.shape
    n = w_all.shape[2]
    tm = min(t, 1024)
    tn = INPROJ_TN
    return pl.pallas_call(
        _inproj_kernel,
        out_shape=jax.ShapeDtypeStruct((t, n), F32),
        grid=(t // tm, n // tn),
        in_specs=[pl.BlockSpec((tm, d), lambda i, j: (i, 0)),
                  pl.BlockSpec((None, d, tn), lambda i, j: (layer, 0, j)),
                  pl.BlockSpec((1, tn), lambda i, j: (0, j))],
        out_specs=pl.BlockSpec((tm, tn), lambda i, j: (i, j)),
        compiler_params=_params("parallel", "parallel"),
        name="inproj",
    )(hn, w_all, b)


def _fox_prep_kernel(h_ref, wq_ref, bq_ref, wg_ref, bg_ref, ws_ref, bs_ref, qw_ref, kw_ref,
                     qn_ref, kn_ref, vo_ref, kb_ref, vb_ref, gate_ref, lf_ref, c_ref, ct_ref, carry_ref, *, tm):
    @pl.when(pl.program_id(1) == 0)
    def _():
        carry_ref[...] = jnp.zeros_like(carry_ref)

    hn = h_ref[...]
    qkv = jnp.dot(hn, wq_ref[...], preferred_element_type=F32) + bq_ref[...]
    gate_ref[...] = _silu(jnp.dot(hn, wg_ref[...], preferred_element_type=F32) + bg_ref[...])
    sm = jnp.dot(hn, ws_ref[...], preferred_element_type=F32) + bs_ref[...]
    scale = HEAD_DIM ** -0.5
    for h in range(N_HEADS):
        qn_ref[:, _head(h)] = (_rms(qkv[:, _head(h)], qw_ref[...]) * scale).astype(qn_ref.dtype)
        kn = _rms(qkv[:, BR_WIDTH + h * HEAD_DIM:BR_WIDTH + (h + 1) * HEAD_DIM], kw_ref[...])
        kn_ref[:, _head(h)] = kn
        kb_ref[:, _head(h)] = kn.astype(BF16)
    v = qkv[:, 2 * BR_WIDTH:]
    vo_ref[...] = v
    vb_ref[...] = v.astype(BF16)
    lf = _log_sigmoid(sm)
    lf_ref[...] = lf[:, SM_FORGET:SM_FORGET + N_HEADS]
    tri = (_iota2((tm, tm), 0) >= _iota2((tm, tm), 1)).astype(F32)
    c = _dot_exact(tri, lf) + carry_ref[...]
    c_ref[...] = c
    ct_ref[...] = c.T
    carry_ref[...] = c[tm - 1:tm, :]


def _fox_prep(hn, w_p, b_p, layer, qn_w, kn_w, nb, s, q_dtype):
    t, d = hn.shape
    tm = min(s, 256)
    nj = s // tm
    row = lambda b, j: b * nj + j
    qkv_w = 3 * BR_WIDTH

    def cols(width, off):
        return (pl.BlockSpec((None, d, width), lambda b, j: (layer, 0, off // width)),
                pl.BlockSpec((1, width), lambda b, j: (0, off // width)))

    out_rows = pl.BlockSpec((tm, BR_WIDTH), lambda b, j: (row(b, j), 0))
    wide_out = lambda dt: jax.ShapeDtypeStruct((t, BR_WIDTH), dt)
    return pl.pallas_call(
        functools.partial(_fox_prep_kernel, tm=tm),
        out_shape=(wide_out(q_dtype), wide_out(F32), wide_out(F32), wide_out(BF16), wide_out(BF16), wide_out(F32),
                   jax.ShapeDtypeStruct((t, N_HEADS), F32),
                   jax.ShapeDtypeStruct((t, LANES), F32),
                   jax.ShapeDtypeStruct((nb, LANES, s), F32)),
        grid=(nb, nj),
        in_specs=[pl.BlockSpec((tm, d), lambda b, j: (row(b, j), 0)),
                  *cols(qkv_w, C_FQ), *cols(BR_WIDTH, C_FG), *cols(LANES, C_SMALL),
                  pl.BlockSpec((1, HEAD_DIM), lambda b, j: (0, 0)),
                  pl.BlockSpec((1, HEAD_DIM), lambda b, j: (0, 0))],
        out_specs=(out_rows, out_rows, out_rows, out_rows, out_rows, out_rows,
                   pl.BlockSpec((tm, N_HEADS), lambda b, j: (row(b, j), 0)),
                   pl.BlockSpec((tm, LANES), lambda b, j: (row(b, j), 0)),
                   pl.BlockSpec((None, LANES, tm), lambda b, j: (b, 0, j))),
        scratch_shapes=[pltpu.VMEM((1, LANES), F32)],
        compiler_params=_params("parallel", "arbitrary"),
        name="fox_prep",
    )(hn, w_p, b_p, w_p, b_p, w_p, b_p, qn_w, kn_w)


def _fox_flash_kernel(q_ref, k_ref, v_ref, cq_ref, ck_ref, g_ref, o_ref, m_ref, acc_ref, cqb_ref, *, tq):
    i = pl.program_id(1)
    j = pl.program_id(2)
    lane_tiles = tq // LANES

    @pl.when(j == 0)
    def _():
        m_ref[...] = jnp.full_like(m_ref, NEG_BIG)
        acc_ref[...] = jnp.zeros_like(acc_ref)
        for h in range(N_HEADS):
            cqb_ref[h] = jnp.broadcast_to(cq_ref[:, SM_FORGET + h:SM_FORGET + h + 1], (tq, LANES))

    def update(diagonal):
        ones = jnp.ones((tq, HEAD_DIM), BF16)
        for h in range(N_HEADS):
            s = _dot_nt(q_ref[:, _head(h)], k_ref[:, _head(h)]) - ck_ref[h:h + 1, :]
            if diagonal:
                s = jnp.where(_iota2((tq, tq), 0) >= _iota2((tq, tq), 1), s, NEG_BIG)
            cqb = cqb_ref[h]
            m_prev = m_ref[h]
            m_new = jnp.maximum(m_prev, jnp.max(s, axis=-1, keepdims=True) + cqb)
            shift = m_new - cqb
            p = jnp.exp((s - jnp.concatenate([shift] * lane_tiles, axis=1)).astype(BF16))
            alpha = jnp.exp(m_prev - m_new)
            v_ones = jnp.concatenate([v_ref[:, _head(h)], ones], axis=1)
            acc_ref[h] = (jnp.concatenate([alpha, alpha], axis=1) * acc_ref[h]
                          + jnp.dot(p, v_ones, preferred_element_type=F32))
            m_ref[h] = m_new

    @pl.when(j < i)
    def _():
        update(False)

    @pl.when(j == i)
    def _():
        update(True)
        for h in range(N_HEADS):
            acc = acc_ref[h]
            o = acc[:, :HEAD_DIM] / acc[:, HEAD_DIM:]
            o_ref[:, _head(h)] = (o * g_ref[:, _head(h)]).astype(BF16)


def _fox_flash(qn, kn, v, c, ct, gate, nb, s):
    t = qn.shape[0]
    tq = min(s, 512)
    nq = s // tq
    qrow = lambda b, i, j: (b * nq + i, 0)
    krow = lambda b, i, j: (b * nq + jnp.minimum(i, j), 0)
    stat = pltpu.VMEM((N_HEADS, tq, LANES), F32)
    return pl.pallas_call(
        functools.partial(_fox_flash_kernel, tq=tq),
        out_shape=jax.ShapeDtypeStruct((t, BR_WIDTH), BF16),
        grid=(nb, nq, nq),
        in_specs=[pl.BlockSpec((tq, BR_WIDTH), qrow),
                  pl.BlockSpec((tq, BR_WIDTH), krow),
                  pl.BlockSpec((tq, BR_WIDTH), krow),
                  pl.BlockSpec((tq, LANES), qrow),
                  pl.BlockSpec((None, SUBLANES, tq), lambda b, i, j: (b, SM_FORGET // SUBLANES, jnp.minimum(i, j))),
                  pl.BlockSpec((tq, BR_WIDTH), qrow)],
        out_specs=pl.BlockSpec((tq, BR_WIDTH), qrow),
        scratch_shapes=[stat, pltpu.VMEM((N_HEADS, tq, 2 * HEAD_DIM), F32), stat],
        compiler_params=_params("parallel", "parallel", "arbitrary"),
        name="fox_flash",
    )(qn, kn, v, c, ct, gate)


def _mem_kv_kernel(m_ref, nw_ref, w_ref, kw_ref, mk_ref, mv_ref):
    kv = _dot(_rms(m_ref[...], nw_ref[...]), w_ref[...])
    for h in range(N_HEADS):
        mk_ref[:, _head(h)] = _rms(kv[:, _head(h)], kw_ref[...])
    mv_ref[...] = kv[:, BR_WIDTH:]


def _mem_kv(mem, norm_w, w_kv, kn_w):
    t, d = mem.shape
    tm = min(t, 256)
    return pl.pallas_call(
        _mem_kv_kernel,
        out_shape=(jax.ShapeDtypeStruct((t, BR_WIDTH), F32), jax.ShapeDtypeStruct((t, BR_WIDTH), F32)),
        grid=(t // tm,),
        in_specs=[pl.BlockSpec((tm, d), lambda i: (i, 0)),
                  pl.BlockSpec((1, d), lambda i: (0, 0)),
                  pl.BlockSpec((d, 2 * BR_WIDTH), lambda i: (0, 0)),
                  pl.BlockSpec((1, HEAD_DIM), lambda i: (0, 0))],
        out_specs=(pl.BlockSpec((tm, BR_WIDTH), lambda i: (i, 0)),
                   pl.BlockSpec((tm, BR_WIDTH), lambda i: (i, 0))),
        compiler_params=_params("parallel"),
        name="mem_kv",
    )(mem, norm_w, w_kv, kn_w)


def _mem_attn_kernel(h_ref, wp_ref, bp_ref, k_ref, v_ref, qw_ref, o_ref):
    proj = jnp.dot(h_ref[...], wp_ref[...], preferred_element_type=F32) + bp_ref[...]
    scale = HEAD_DIM ** -0.5
    for h in range(N_HEADS):
        q = _rms(proj[:, _head(h)], qw_ref[...]) * scale
        s = _dot_nt(q, k_ref[:, _head(h)])
        p = jnp.exp(s - jnp.max(s, axis=-1, keepdims=True))
        o = _dot(p, v_ref[:, _head(h)]) / jnp.sum(p, axis=-1, keepdims=True)
        gate = _silu(proj[:, BR_WIDTH + h * HEAD_DIM:BR_WIDTH + (h + 1) * HEAD_DIM])
        o_ref[:, _head(h)] = (o * gate).astype(BF16)


def _mem_attn(hn, w_p, b_p, layer, mk, mv, qn_w, nb, s):
    t, d = hn.shape
    pair_w = 2 * BR_WIDTH
    m = mk.shape[0] // nb
    tq = min(s, 512)
    nq = s // tq
    return pl.pallas_call(
        _mem_attn_kernel,
        out_shape=jax.ShapeDtypeStruct((t, BR_WIDTH), BF16),
        grid=(nb, nq),
        in_specs=[pl.BlockSpec((tq, d), lambda b, i: (b * nq + i, 0)),
                  pl.BlockSpec((None, d, pair_w), lambda b, i: (layer, 0, C_MQ // pair_w)),
                  pl.BlockSpec((1, pair_w), lambda b, i: (0, C_MQ // pair_w)),
                  pl.BlockSpec((m, BR_WIDTH), lambda b, i: (b, 0)),
                  pl.BlockSpec((m, BR_WIDTH), lambda b, i: (b, 0)),
                  pl.BlockSpec((1, HEAD_DIM), lambda b, i: (0, 0))],
        out_specs=pl.BlockSpec((tq, BR_WIDTH), lambda b, i: (b * nq + i, 0)),
        compiler_params=_params("parallel", "parallel"),
        name="mem_attn",
    )(hn, w_p, b_p, mk, mv, qn_w)


def _causal_conv_tile(x, halo, w_ref):
    rows8 = _iota2(halo.shape, 0)
    y = x * w_ref[CONV_W - 1:CONV_W, :]
    for k in range(1, CONV_W):
        xr = pltpu.roll(x, k, axis=0)
        hr = pltpu.roll(halo, k, axis=0)
        top = jnp.where(rows8 < k, hr, xr[:SUBLANES])
        xs = jnp.concatenate([top, xr[SUBLANES:]], axis=0)
        y = y + xs * w_ref[CONV_W - 1 - k:CONV_W - k, :]
    return y


def _unit_lower_inverses(mats, row, col):
    n = mats[0].shape[0]

    def off_block(shift):
        return (((row >> (shift + 1)) == (col >> (shift + 1)))
                & (((row >> shift) & 1) == 1) & (((col >> shift) & 1) == 0))

    eye = (row == col).astype(F32)
    first = off_block(0)
    ds = [eye - jnp.where(first, a, 0.0) for a in mats]
    shift = 1
    while (1 << shift) < n:
        mask = off_block(shift)
        ts = [_dot(jnp.where(mask, a, 0.0), d) for a, d in zip(mats, ds)]
        ds = [d - _dot(d, t) for d, t in zip(ds, ts)]
        shift += 1
    return ds


def _dn_kernel(h_ref, wm_ref, bm_ref, ws_ref, bs_ref, cw_ref, al_ref, dt_ref, ow_ref,
               o_ref, st_ref, tail_ref, stt_ref, halo_ref, *, tl):
    j = pl.program_id(1)

    @pl.when(j == 0)
    def _():
        stt_ref[...] = jnp.zeros_like(stt_ref)
        halo_ref[...] = jnp.zeros_like(halo_ref)

    hn = h_ref[...]
    proj = jnp.dot(hn, wm_ref[...], preferred_element_type=F32) + bm_ref[...]
    sm = jnp.dot(hn, ws_ref[...], preferred_element_type=F32) + bs_ref[...]
    x = proj[:, :3 * BR_WIDTH]
    z = proj[:, 3 * BR_WIDTH:]
    qkv = _silu(_causal_conv_tile(x, halo_ref[...], cw_ref))
    halo_ref[...] = x[tl - SUBLANES:, :]
    tail_ref[...] = x[tl - SUBLANES:, :]
    beta_all = _sigmoid(sm)
    la_all = -jnp.exp(al_ref[...]) * _softplus(sm + dt_ref[...])

    cc = DN_CHUNK
    row = _iota2((cc, cc), 0)
    col = _iota2((cc, cc), 1)
    incl = row >= col
    strict = row > col
    tri = incl.astype(F32)
    kd = N_HEADS * HEAD_DIM

    systems = []
    for c in range(tl // cc):
        rows = slice(c * cc, (c + 1) * cc)
        g_all = _dot_exact(tri, la_all[rows])
        gt_all = g_all.T
        eg_all = jnp.exp(g_all)
        for h in range(N_HEADS):
            q = _l2norm(qkv[rows, h * HEAD_DIM:(h + 1) * HEAD_DIM]) * (HEAD_DIM ** -0.5)
            k = _l2norm(qkv[rows, kd + h * HEAD_DIM:kd + (h + 1) * HEAD_DIM])
            v = qkv[rows, 2 * kd + h * HEAD_DIM:2 * kd + (h + 1) * HEAD_DIM]
            beta = beta_all[rows, SM_BETA + h:SM_BETA + h + 1]
            gc = g_all[:, SM_DECAY + h:SM_DECAY + h + 1]
            gr = gt_all[SM_DECAY + h:SM_DECAY + h + 1, :]
            eg = eg_all[:, SM_DECAY + h:SM_DECAY + h + 1]
            g_last = g_all[cc - 1:cc, SM_DECAY + h:SM_DECAY + h + 1]
            decay = jnp.exp(jnp.where(incl, gc - gr, NEG_BIG))
            kb = k.astype(BF16)
            qb = q.astype(BF16)
            systems.append(dict(
                a=jnp.where(strict, beta * decay * _dot_nt(kb, kb), 0.0),
                qk=(decay * _dot_nt(qb, kb)).astype(BF16),
                rhs=jnp.concatenate([beta * v, (beta * eg) * k], axis=1).astype(BF16),
                kdec_t=(jnp.exp(g_last - gc) * k).T.astype(BF16),
                q=qb, eg=eg, g_end=jnp.exp(g_last)))

    ds = _unit_lower_inverses([sy["a"] for sy in systems], row, col)
    sols = [_dot(d, sy["rhs"]) for d, sy in zip(ds, systems)]

    heads = range(N_HEADS)
    stts = [stt_ref[h] for h in heads]
    for c in range(tl // cc):
        rows = slice(c * cc, (c + 1) * cc)
        sys_c = systems[c * N_HEADS:(c + 1) * N_HEADS]
        sol_c = sols[c * N_HEADS:(c + 1) * N_HEADS]
        us = [sol_c[h][:, :HEAD_DIM] - _dot(sol_c[h][:, HEAD_DIM:], stts[h]) for h in heads]
        os = [sys_c[h]["eg"] * _dot(sys_c[h]["q"], stts[h]) + _dot(sys_c[h]["qk"], us[h]) for h in heads]
        stts = [sys_c[h]["g_end"] * stts[h] + _dot(sys_c[h]["kdec_t"], us[h]) for h in heads]
        for h in heads:
            o = _rms(os[h], ow_ref[...]) * _silu(z[rows, _head(h)])
            o_ref[rows, _head(h)] = o.astype(BF16)
    for h in heads:
        stt_ref[h] = stts[h]

    @pl.when(j == pl.num_programs(1) - 1)
    def _():
        for h in range(N_HEADS):
            st_ref[h] = stt_ref[h].T


def _dn_prompt(hn, w_p, b_p, layer, conv_w, al_row, dt_row, onorm_w, nb, s):
    t, d = hn.shape
    tl = min(s, 512)
    nj = s // tl
    qkv_w = 3 * BR_WIDTH
    main_w = qkv_w + BR_WIDTH
    row = lambda b, j: b * nj + j
    return pl.pallas_call(
        functools.partial(_dn_kernel, tl=tl),
        out_shape=(jax.ShapeDtypeStruct((t, BR_WIDTH), BF16),
                   jax.ShapeDtypeStruct((nb, N_HEADS, HEAD_DIM, HEAD_DIM), F32),
                   jax.ShapeDtypeStruct((nb, SUBLANES, qkv_w), F32)),
        grid=(nb, nj),
        in_specs=[pl.BlockSpec((tl, d), lambda b, j: (row(b, j), 0)),
                  pl.BlockSpec((None, d, main_w), lambda b, j: (layer, 0, C_DNQKV // main_w)),
                  pl.BlockSpec((1, main_w), lambda b, j: (0, C_DNQKV // main_w)),
                  pl.BlockSpec((None, d, LANES), lambda b, j: (layer, 0, C_SMALL // LANES)),
                  pl.BlockSpec((1, LANES), lambda b, j: (0, C_SMALL // LANES)),
                  pl.BlockSpec((CONV_W, qkv_w), lambda b, j: (0, 0)),
                  pl.BlockSpec((1, LANES), lambda b, j: (0, 0)),
                  pl.BlockSpec((1, LANES), lambda b, j: (0, 0)),
                  pl.BlockSpec((1, HEAD_DIM), lambda b, j: (0, 0))],
        out_specs=(pl.BlockSpec((tl, BR_WIDTH), lambda b, j: (row(b, j), 0)),
                   pl.BlockSpec((None, N_HEADS, HEAD_DIM, HEAD_DIM), lambda b, j: (b, 0, 0, 0)),
                   pl.BlockSpec((None, SUBLANES, qkv_w), lambda b, j: (b, 0, 0))),
        scratch_shapes=[pltpu.VMEM((N_HEADS, HEAD_DIM, HEAD_DIM), F32),
                        pltpu.VMEM((SUBLANES, qkv_w), F32)],
        compiler_params=_params("parallel", "arbitrary"),
        name="dn_prompt",
    )(hn, w_p, b_p, w_p, b_p, conv_w, al_row, dt_row, onorm_w)


def _lru_gates(xc, wr_ref, br_ref, wi_ref, bi_ref, lam_ref):
    r = _sigmoid(_dot(xc, wr_ref[...]) + br_ref[...])
    i = _sigmoid(_dot(xc, wi_ref[...]) + bi_ref[...])
    log_a = -LRU_C * r * _softplus(-lam_ref[...])
    a = jnp.exp(log_a)
    one_minus_a2 = -jnp.tanh(log_a) * (a * a + 1.0)
    return a, jnp.sqrt(one_minus_a2) * (i * xc)


def _lru_kernel(h_ref, wp_ref, bp_ref, cw_ref, cb_ref, wr_ref, br_ref, wi_ref, bi_ref, lam_ref,
                o_ref, hl_ref, tail_ref, hc_ref, halo_ref, *, tl):
    @pl.when(pl.program_id(1) == 0)
    def _():
        hc_ref[...] = jnp.zeros_like(hc_ref)
        halo_ref[...] = jnp.zeros_like(halo_ref)

    proj = jnp.dot(h_ref[...], wp_ref[...], preferred_element_type=F32) + bp_ref[...]
    x = proj[:, :BR_WIDTH]
    xc = _causal_conv_tile(x, halo_ref[...], cw_ref) + cb_ref[...]
    halo_ref[...] = x[tl - SUBLANES:, :]
    tail_ref[...] = x[tl - SUBLANES:, :]
    a, b = _lru_gates(xc, wr_ref, br_ref, wi_ref, bi_ref, lam_ref)
    rows = _iota2(a.shape, 0)
    d = 1
    while d < tl:
        a_up = jnp.where(rows >= d, pltpu.roll(a, d, axis=0), 1.0)
        b_up = jnp.where(rows >= d, pltpu.roll(b, d, axis=0), 0.0)
        b = a * b_up + b
        a = a * a_up
        d *= 2
    hseq = a * hc_ref[...] + b
    hc_ref[...] = hseq[tl - 1:tl, :]
    hl_ref[...] = hseq[tl - 1:tl, :]
    o_ref[...] = (hseq * _silu(proj[:, BR_WIDTH:])).astype(BF16)


def _lru_prompt(hn, w_p, b_p, layer, conv_w, conv_b, w_r, b_r, w_i, b_i, lam, nb, s):
    t, d = hn.shape
    pair_w = 2 * BR_WIDTH
    tl = min(s, 256)
    nj = s // tl
    row = lambda b, j: b * nj + j
    vec = pl.BlockSpec((1, BR_WIDTH), lambda b, j: (0, 0))
    mat = pl.BlockSpec((BR_WIDTH, BR_WIDTH), lambda b, j: (0, 0))
    return pl.pallas_call(
        functools.partial(_lru_kernel, tl=tl),
        out_shape=(jax.ShapeDtypeStruct((t, BR_WIDTH), BF16),
                   jax.ShapeDtypeStruct((nb, 1, BR_WIDTH), F32),
                   jax.ShapeDtypeStruct((nb, SUBLANES, BR_WIDTH), F32)),
        grid=(nb, nj),
        in_specs=[pl.BlockSpec((tl, d), lambda b, j: (row(b, j), 0)),
                  pl.BlockSpec((None, d, pair_w), lambda b, j: (layer, 0, C_LRUX // pair_w)),
                  pl.BlockSpec((1, pair_w), lambda b, j: (0, C_LRUX // pair_w)),
                  pl.BlockSpec((CONV_W, BR_WIDTH), lambda b, j: (0, 0)),
                  vec, mat, vec, mat, vec, vec],
        out_specs=(pl.BlockSpec((tl, BR_WIDTH), lambda b, j: (row(b, j), 0)),
                   pl.BlockSpec((None, 1, BR_WIDTH), lambda b, j: (b, 0, 0)),
                   pl.BlockSpec((None, SUBLANES, BR_WIDTH), lambda b, j: (b, 0, 0))),
        scratch_shapes=[pltpu.VMEM((1, BR_WIDTH), F32), pltpu.VMEM((SUBLANES, BR_WIDTH), F32)],
        compiler_params=_params("parallel", "arbitrary"),
        name="lru_prompt",
    )(hn, w_p, b_p, conv_w, conv_b, w_r, b_r, w_i, b_i, lam)


def _merge_kernel(x_ref, h_ref, a_ref, b_ref, c_ref, m_ref, wg_ref, bg_ref, wb_ref, wo_ref, y_ref):
    x = x_ref[...]
    d = x.shape[1]
    h = h_ref[...]
    merged = None
    for n, br in enumerate((a_ref, b_ref, c_ref, m_ref)):
        cols = slice(n * d, (n + 1) * d)
        gate = jnp.dot(h, wg_ref[:, cols], preferred_element_type=F32) + bg_ref[:, cols]
        up = jnp.dot(br[...], wb_ref[n], preferred_element_type=F32)
        term = _sigmoid(gate) * up
        merged = term if merged is None else merged + term
    y_ref[...] = x + jnp.dot(merged.astype(BF16), wo_ref[...], preferred_element_type=F32)


def _merge(x, hn, out_a, out_b, out_c, out_m, w_gate, b_gate, w_branch, w_out, layer):
    t, d = x.shape
    tm = min(t, 512)
    rows = lambda w: pl.BlockSpec((tm, w), lambda i: (i, 0))
    return pl.pallas_call(
        _merge_kernel,
        out_shape=jax.ShapeDtypeStruct((t, d), F32),
        grid=(t // tm,),
        in_specs=[rows(d), rows(d),
                  rows(BR_WIDTH), rows(BR_WIDTH), rows(BR_WIDTH), rows(BR_WIDTH),
                  pl.BlockSpec((None, d, N_GATE), lambda i: (layer, 0, 0)),
                  pl.BlockSpec((1, N_GATE), lambda i: (0, 0)),
                  pl.BlockSpec((N_BRANCH, BR_WIDTH, d), lambda i: (0, 0, 0)),
                  pl.BlockSpec((d, d), lambda i: (0, 0))],
        out_specs=rows(d),
        compiler_params=_params("parallel"),
        name="merge",
    )(x, hn, out_a, out_b, out_c, out_m, w_gate, b_gate, w_branch, w_out)


def _sample_step_kernel(qkv_ref, z_ref, lx_ref, lg_ref, sm_ref, st_ref, dbuf_ref, lh_ref, lbuf_ref,
                        dcw_ref, al_ref, dt_ref, ow_ref, lcw_ref, lcb_ref, wr_ref, br_ref, wi_ref, bi_ref, lam_ref,
                        oa_ref, ob_ref, sto_ref, dbo_ref, lho_ref, lbo_ref, o_scr, *, bs):
    kd = N_HEADS * HEAD_DIM

    def step_conv(x, buf_ref, bufo_ref, w_ref):
        y = x * w_ref[CONV_W - 1:CONV_W, :]
        for k in range(CONV_W - 1):
            y = y + buf_ref[:, k, :] * w_ref[k:k + 1, :]
        for k in range(CONV_W - 2):
            bufo_ref[:, k, :] = buf_ref[:, k + 1, :]
        bufo_ref[:, CONV_W - 2, :] = x
        return y

    qkv = _silu(step_conv(qkv_ref[...], dbuf_ref, dbo_ref, dcw_ref))
    sm = sm_ref[...]
    beta_all = _sigmoid(sm)
    eg_all = jnp.exp(-jnp.exp(al_ref[...]) * _softplus(sm + dt_ref[...]))
    eye = _iota2((HEAD_DIM, HEAD_DIM), 0) == _iota2((HEAD_DIM, HEAD_DIM), 1)
    pad = jnp.zeros((SUBLANES - 2, HEAD_DIM), F32)
    items = []
    for h in range(N_HEADS):
        q = _l2norm(qkv[:, h * HEAD_DIM:(h + 1) * HEAD_DIM]) * (HEAD_DIM ** -0.5)
        k = _l2norm(qkv[:, kd + h * HEAD_DIM:kd + (h + 1) * HEAD_DIM])
        v = qkv[:, 2 * kd + h * HEAD_DIM:2 * kd + (h + 1) * HEAD_DIM]
        beta = beta_all[:, SM_BETA + h:SM_BETA + h + 1]
        eg = eg_all[:, SM_DECAY + h:SM_DECAY + h + 1]
        qk = jnp.sum(q * k, axis=-1, keepdims=True)
        for b in range(bs):
            one = slice(b, b + 1)
            items.append(dict(b=b, h=h, q=q[one], k=k[one], v=v[one], beta=beta[one], eg=eg[one], qk=qk[one]))
    sk_sqs = [_dot_nt(jnp.concatenate([it["k"], it["q"], pad], axis=0), st_ref[it["b"], it["h"]])
              for it in items]
    us = [it["beta"] * (it["v"] - it["eg"] * r[0:1]) for it, r in zip(items, sk_sqs)]
    for it, r, u in zip(items, sk_sqs, us):
        o_scr[it["b"]:it["b"] + 1, _head(it["h"])] = it["eg"] * r[1:2] + it["qk"] * u
    updates = [_dot(jnp.where(eye, jnp.broadcast_to(u, (HEAD_DIM, HEAD_DIM)), 0.0),
                    jnp.broadcast_to(it["k"], (HEAD_DIM, HEAD_DIM))) for it, u in zip(items, us)]
    for it, upd in zip(items, updates):
        sto_ref[it["b"], it["h"]] = it["eg"] * st_ref[it["b"], it["h"]] + upd
    o = o_scr[...]
    for h in range(N_HEADS):
        oa_ref[:, _head(h)] = (_rms(o[:, _head(h)], ow_ref[...]) * _silu(z_ref[:, _head(h)])).astype(BF16)

    xc = step_conv(lx_ref[...], lbuf_ref, lbo_ref, lcw_ref) + lcb_ref[...]
    a, bx = _lru_gates(xc, wr_ref, br_ref, wi_ref, bi_ref, lam_ref)
    hnew = a * lh_ref[...] + bx
    lho_ref[...] = hnew
    ob_ref[...] = (hnew * _silu(lg_ref[...])).astype(BF16)


def _sample_step(proj, layer, state_dn, state_dn_conv, state_lru_h, state_lru_conv,
                 dn_conv_w, al_row, dt_row, onorm_w, lru_conv_w, lru_conv_b, w_r, b_r, w_i, b_i, lam):
    nb = proj.shape[0]
    bs = SUBLANES
    qkv_w = 3 * BR_WIDTH
    wide = lambda off: pl.BlockSpec((bs, BR_WIDTH), lambda i: (i, off // BR_WIDTH))
    const = lambda shape: pl.BlockSpec(shape, lambda i: (0,) * len(shape))
    st_in = pl.BlockSpec((None, bs, N_HEADS, HEAD_DIM, HEAD_DIM), lambda i: (layer, i, 0, 0, 0))
    dbuf_in = pl.BlockSpec((None, bs, CONV_W - 1, qkv_w), lambda i: (layer, i, 0, 0))
    lh_in = pl.BlockSpec((None, bs, BR_WIDTH), lambda i: (layer, i, 0))
    lbuf_in = pl.BlockSpec((None, bs, CONV_W - 1, BR_WIDTH), lambda i: (layer, i, 0, 0))
    return pl.pallas_call(
        functools.partial(_sample_step_kernel, bs=bs),
        out_shape=(jax.ShapeDtypeStruct((nb, BR_WIDTH), BF16),
                   jax.ShapeDtypeStruct((nb, BR_WIDTH), BF16),
                   jax.ShapeDtypeStruct((nb, N_HEADS, HEAD_DIM, HEAD_DIM), F32),
                   jax.ShapeDtypeStruct((nb, CONV_W - 1, qkv_w), F32),
                   jax.ShapeDtypeStruct((nb, BR_WIDTH), F32),
                   jax.ShapeDtypeStruct((nb, CONV_W - 1, BR_WIDTH), F32)),
        grid=(nb // bs,),
        in_specs=[pl.BlockSpec((bs, qkv_w), lambda i: (i, 0)), wide(C_DNZ), wide(C_LRUX), wide(C_LRUG),
                  pl.BlockSpec((bs, LANES), lambda i: (i, C_SMALL // LANES)),
                  st_in, dbuf_in, lh_in, lbuf_in,
                  const((CONV_W, qkv_w)), const((1, LANES)), const((1, LANES)), const((1, HEAD_DIM)),
                  const((CONV_W, BR_WIDTH)), const((1, BR_WIDTH)),
                  const((BR_WIDTH, BR_WIDTH)), const((1, BR_WIDTH)),
                  const((BR_WIDTH, BR_WIDTH)), const((1, BR_WIDTH)), const((1, BR_WIDTH))],
        out_specs=(pl.BlockSpec((bs, BR_WIDTH), lambda i: (i, 0)),
                   pl.BlockSpec((bs, BR_WIDTH), lambda i: (i, 0)),
                   pl.BlockSpec((bs, N_HEADS, HEAD_DIM, HEAD_DIM), lambda i: (i, 0, 0, 0)),
                   pl.BlockSpec((bs, CONV_W - 1, qkv_w), lambda i: (i, 0, 0)),
                   pl.BlockSpec((bs, BR_WIDTH), lambda i: (i, 0)),
                   pl.BlockSpec((bs, CONV_W - 1, BR_WIDTH), lambda i: (i, 0, 0))),
        scratch_shapes=[pltpu.VMEM((bs, BR_WIDTH), F32)],
        compiler_params=_params("parallel"),
        name="sample_step",
    )(proj, proj, proj, proj, proj, state_dn, state_dn_conv, state_lru_h, state_lru_conv,
      dn_conv_w, al_row, dt_row, onorm_w, lru_conv_w, lru_conv_b, w_r, b_r, w_i, b_i, lam)


def _own_head_mask(n_rows_kv):
    shape = (SUBLANES, n_rows_kv)
    return (_iota2(shape, 1) % N_HEADS) == _iota2(shape, 0)


def _heads_to_row(o8):
    return jnp.concatenate([o8[h:h + 1] for h in range(N_HEADS)], axis=1)


def _split3(x):
    hi = x.astype(BF16)
    r1 = x - hi.astype(F32)
    mid = r1.astype(BF16)
    lo = (r1 - mid.astype(F32)).astype(BF16)
    return hi, mid, lo


def _pool_bias_kernel(lf_ref, after_ref, same_ref, inpage_ref, total_ref):
    in_page = None
    total = None
    for h in range(N_HEADS):
        for piece in _split3(lf_ref[:, h, :]):
            a = jnp.dot(piece, after_ref[h], preferred_element_type=F32)
            t = jnp.dot(piece, same_ref[h], preferred_element_type=F32)
            in_page = a if in_page is None else in_page + a
            total = t if total is None else total + t
    inpage_ref[...] = in_page
    total_ref[...] = total


def _pool_bias(cache_lf):
    n, nh, page = cache_lf.shape
    w = page * nh
    tp = min(n, 512)
    src_tok = jnp.arange(page)[None, :, None]
    src_head = jnp.arange(nh)[:, None, None]
    dst = jnp.arange(w)[None, None, :]
    same = (dst % nh) == src_head
    after = same & (src_tok > dst // nh)
    rows = pl.BlockSpec((tp, w), lambda i: (i, 0))
    const = pl.BlockSpec((nh, page, w), lambda i: (0, 0, 0))
    return pl.pallas_call(
        _pool_bias_kernel,
        out_shape=(jax.ShapeDtypeStruct((n, w), F32), jax.ShapeDtypeStruct((n, w), F32)),
        grid=(n // tp,),
        in_specs=[pl.BlockSpec((tp, nh, page), lambda i: (i, 0, 0)), const, const],
        out_specs=(rows, rows),
        compiler_params=_params("parallel"),
        name="pool_bias",
    )(cache_lf, after.astype(BF16), jnp.broadcast_to(same, after.shape).astype(BF16))


def _fox_decode_kernel(pt_ref, q_ref, kn_ref, vn_ref, lfn_ref, g_ref, *rest, n_pages):
    k_refs = rest[:n_pages]
    v_refs = rest[n_pages:2 * n_pages]
    inpage_refs = rest[2 * n_pages:3 * n_pages]
    total_refs = rest[3 * n_pages:4 * n_pages]
    o_ref = rest[4 * n_pages]
    del pt_ref
    q8 = q_ref[...]
    qb = q8.astype(BF16)
    own = _own_head_mask(k_refs[0].shape[0])

    later = lfn_ref[...]
    scores = [None] * n_pages
    for p in reversed(range(n_pages)):
        s = _dot_nt(qb, k_refs[p][...]) + (inpage_refs[p][...] + later)
        scores[p] = jnp.where(own, s, NEG_BIG)
        later = later + total_refs[p][...]
    s_self = jnp.sum(q8 * kn_ref[...], axis=-1, keepdims=True)
    m = s_self
    for s in scores:
        m = jnp.maximum(m, jnp.max(s, axis=-1, keepdims=True))
    p_self = jnp.exp(s_self - m)
    l = p_self
    acc = p_self * vn_ref[...]
    for p in range(n_pages):
        pr = jnp.exp(scores[p] - m)
        l = l + jnp.sum(pr, axis=-1, keepdims=True)
        acc = acc + _dot(pr, v_refs[p][...])
    o_ref[...] = (_heads_to_row(acc / l) * g_ref[...]).astype(BF16)


def _fox_decode(q8, k8, v8, lf_new, gate, cache_k, cache_v, in_page, totals, page_table, layer, n_pool):
    nb = q8.shape[0]
    n_pages = page_table.shape[0] // nb
    kv_rows = cache_k.shape[1]
    base = layer * n_pool
    row = pl.BlockSpec((None, 1, BR_WIDTH), lambda b, pt: (b, 0, 0))
    heads = pl.BlockSpec((None, SUBLANES, HEAD_DIM), lambda b, pt: (b, 0, 0))

    def paged(shape, p):
        return pl.BlockSpec((None,) + shape, lambda b, pt: (base + pt[b * n_pages + p], 0, 0))

    in_specs = [heads, heads, heads, pl.BlockSpec((None, 1, kv_rows), lambda b, pt: (b, 0, 0)), row]
    in_specs += [paged((kv_rows, HEAD_DIM), p) for p in range(n_pages)]
    in_specs += [paged((kv_rows, HEAD_DIM), p) for p in range(n_pages)]
    in_specs += [paged((1, kv_rows), p) for p in range(n_pages)]
    in_specs += [paged((1, kv_rows), p) for p in range(n_pages)]
    grid_spec = pltpu.PrefetchScalarGridSpec(
        num_scalar_prefetch=1,
        grid=(nb,),
        in_specs=in_specs,
        out_specs=row)
    return pl.pallas_call(
        functools.partial(_fox_decode_kernel, n_pages=n_pages),
        out_shape=jax.ShapeDtypeStruct((nb, 1, BR_WIDTH), BF16),
        grid_spec=grid_spec,
        compiler_params=_params("parallel"),
        name="fox_decode",
    )(page_table, q8, k8, v8, lf_new, gate,
      *([cache_k] * n_pages), *([cache_v] * n_pages), *([in_page] * n_pages), *([totals] * n_pages))


def _mem_decode_kernel(q_ref, g_ref, k_ref, v_ref, qw_ref, o_ref, o_scr, *, bs):
    own = _own_head_mask(k_ref.shape[1])
    scale = HEAD_DIM ** -0.5
    qs = [_rms(q_ref[:, _head(h)], qw_ref[...]) * scale for h in range(N_HEADS)]
    pad = jnp.zeros((SUBLANES - N_HEADS, HEAD_DIM), F32)
    for b in range(bs):
        q8 = jnp.concatenate([q[b:b + 1] for q in qs] + [pad], axis=0)
        s = jnp.where(own, _dot_nt(q8, k_ref[b]), NEG_BIG)
        p = jnp.exp(s - jnp.max(s, axis=-1, keepdims=True))
        o = _dot(p, v_ref[b]) / jnp.sum(p, axis=-1, keepdims=True)
        o_scr[b:b + 1, :] = _heads_to_row(o)
    o_ref[...] = (o_scr[...] * _silu(g_ref[...])).astype(BF16)


def _mem_decode(proj, cache_k, cache_v, qn_w, layer):
    nb = proj.shape[0]
    bs = SUBLANES
    m = cache_k.shape[1]
    base = layer * (nb // bs)
    kv = pl.BlockSpec((bs, m, HEAD_DIM), lambda i: (base + i, 0, 0))
    return pl.pallas_call(
        functools.partial(_mem_decode_kernel, bs=bs),
        out_shape=jax.ShapeDtypeStruct((nb, BR_WIDTH), BF16),
        grid=(nb // bs,),
        in_specs=[pl.BlockSpec((bs, BR_WIDTH), lambda i: (i, C_MQ // BR_WIDTH)),
                  pl.BlockSpec((bs, BR_WIDTH), lambda i: (i, C_MG // BR_WIDTH)),
                  kv, kv, pl.BlockSpec((1, HEAD_DIM), lambda i: (0, 0))],
        out_specs=pl.BlockSpec((bs, BR_WIDTH), lambda i: (i, 0)),
        scratch_shapes=[pltpu.VMEM((bs, BR_WIDTH), F32)],
        compiler_params=_params("parallel"),
        name="mem_decode",
    )(proj, proj, cache_k, cache_v, qn_w)


def _permute_columns(a):
    runs = sorted(COLUMN_RUNS, key=lambda r: r[2])
    pad = jnp.zeros(a.shape[:-1] + (N_PERM - C_SMALL - N_SMALL,), a.dtype)
    return jnp.concatenate([a[..., start:stop] for start, stop, _ in runs] + [pad], axis=-1)


def _block_diag(w):
    nblk, e, f = w.shape
    eye = jnp.eye(nblk, dtype=w.dtype)
    return (eye[:, None, :, None] * w[:, :, None, :]).reshape(nblk * e, nblk * f)


def _decay_lane_row(v):
    return jnp.zeros((1, LANES), F32).at[0, SM_DECAY:SM_DECAY + N_HEADS].set(v)


def kernel(x_prompt, x_sample, cache_fox_k, cache_fox_v, cache_fox_logf, cache_mem_k, cache_mem_v, state_dn, state_dn_conv, state_lru_h, state_lru_conv, page_table, mem_prompt, norm_w, w_in, b_in, dn_conv_w, dn_A_log, dn_dt_bias, dn_onorm_w, lru_conv_w, lru_conv_b, lru_w_r, lru_b_r, lru_w_i, lru_b_i, lru_lambda, fox_qn_w, fox_kn_w, mem_norm_w, w_mem_kv, mem_qn_w, mem_kn_w, w_branch, w_out):
    bp, s, d = x_prompt.shape
    bd = x_sample.shape[0]
    depth = w_in.shape[0]
    n_pool, page = cache_fox_k.shape[1], cache_fox_k.shape[2]
    mem_tokens = mem_prompt.shape[1]
    tp = bp * s

    yp = x_prompt.reshape(tp, d)
    ys = x_sample.reshape(bd, d)
    mem2 = mem_prompt.reshape(bp * mem_tokens, d)
    kv_rows = page * N_HEADS
    cache_k2 = cache_fox_k.reshape(depth * n_pool, kv_rows, HEAD_DIM)
    cache_v2 = cache_fox_v.reshape(depth * n_pool, kv_rows, HEAD_DIM)
    in_page, totals = _pool_bias(jnp.swapaxes(cache_fox_logf, 2, 3).reshape(depth * n_pool, N_HEADS, page))
    in_page = in_page.reshape(depth * n_pool, 1, kv_rows)
    totals = totals.reshape(depth * n_pool, 1, kv_rows)
    cmem_k2 = cache_mem_k.reshape(depth * bd, mem_tokens * N_HEADS, HEAD_DIM)
    cmem_v2 = cache_mem_v.reshape(depth * bd, mem_tokens * N_HEADS, HEAD_DIM)
    pt_flat = page_table.reshape(-1)

    row = lambda v: v.reshape(1, -1)
    acc = {n: [] for n in ("pk", "pv", "plf", "pmk", "pmv", "pdn", "pdc", "plh", "plc",
                           "sk", "sv", "slf", "sdn", "sdc", "slh", "slc")}
    w_p, w_g = _repack_w_in(w_in)
    for l in range(depth):
        b_p = _permute_columns(b_in[l]).reshape(1, N_PERM)
        b_g = b_in[l, GATE_RUN[0]:GATE_RUN[1]].reshape(1, N_GATE)
        nw = row(norm_w[l])
        al_row = _decay_lane_row(dn_A_log[l])
        dt_row = _decay_lane_row(dn_dt_bias[l])
        ow = row(dn_onorm_w[l])
        wr = _block_diag(lru_w_r[l]).astype(BF16)
        wi = _block_diag(lru_w_i[l]).astype(BF16)
        lru_args = (lru_conv_w[l], row(lru_conv_b[l]), wr, row(lru_b_r[l]), wi, row(lru_b_i[l]), row(lru_lambda[l]))
        wb = w_branch[l].astype(BF16)
        wo = w_out[l].astype(BF16)
        fqw, fkw, mqw = row(fox_qn_w[l]), row(fox_kn_w[l]), row(mem_qn_w[l])

        hp = _norm(yp, nw)
        qb, kn, fv, kb, vb, gate_c, lf, c, ct = _fox_prep(hp, w_p, b_p, l, fqw, fkw, bp, s, BF16)
        out_c = _fox_flash(qb, kb, vb, c, ct, gate_c, bp, s)
        mk, mv = _mem_kv(mem2, row(mem_norm_w[l]), w_mem_kv[l].astype(BF16), row(mem_kn_w[l]))
        out_m = _mem_attn(hp, w_p, b_p, l, mk, mv, mqw, bp, s)
        out_a, dn_s, dn_tail = _dn_prompt(hp, w_p, b_p, l, dn_conv_w[l], al_row, dt_row, ow, bp, s)
        out_b, lru_h, lru_tail = _lru_prompt(hp, w_p, b_p, l, *lru_args, bp, s)
        acc["pk"].append(kn.reshape(bp, s, N_HEADS, HEAD_DIM))
        acc["pv"].append(fv.reshape(bp, s, N_HEADS, HEAD_DIM))
        acc["plf"].append(lf.reshape(bp, s, N_HEADS))
        acc["pmk"].append(mk.reshape(bp, mem_tokens, N_HEADS, HEAD_DIM))
        acc["pmv"].append(mv.reshape(bp, mem_tokens, N_HEADS, HEAD_DIM))
        acc["pdn"].append(dn_s)
        acc["pdc"].append(dn_tail[:, SUBLANES - (CONV_W - 1):, :])
        acc["plh"].append(lru_h.reshape(bp, BR_WIDTH))
        acc["plc"].append(lru_tail[:, SUBLANES - (CONV_W - 1):, :])
        yp = _merge(yp, hp, out_a, out_b, out_c, out_m, w_g, b_g, wb, wo, l)

        hs = _norm(ys, nw)
        proj_s = _inproj(hs, w_p, b_p, l)
        qn_s, kn_s, fv_s, _, _, gate_s, lf_s, _, _ = _fox_prep(hs, w_p, b_p, l, fqw, fkw, 1, bd, F32)
        head_rows = lambda a: jnp.pad(a.reshape(bd, N_HEADS, HEAD_DIM), ((0, 0), (0, SUBLANES - N_HEADS), (0, 0)))
        lf_new = jnp.tile(lf_s, (1, page)).reshape(bd, 1, kv_rows)
        out_c_s = _fox_decode(head_rows(qn_s), head_rows(kn_s), head_rows(fv_s), lf_new,
                              gate_s.reshape(bd, 1, BR_WIDTH),
                              cache_k2, cache_v2, in_page, totals, pt_flat, l, n_pool).reshape(bd, BR_WIDTH)
        out_m_s = _mem_decode(proj_s, cmem_k2, cmem_v2, mqw, l)
        out_a_s, out_b_s, dn_s_s, dn_c_s, lru_h_s, lru_c_s = _sample_step(
            proj_s, l, state_dn, state_dn_conv, state_lru_h, state_lru_conv,
            dn_conv_w[l], al_row, dt_row, ow, *lru_args)
        acc["sk"].append(kn_s.reshape(bd, 1, N_HEADS, HEAD_DIM))
        acc["sv"].append(fv_s.reshape(bd, 1, N_HEADS, HEAD_DIM))
        acc["slf"].append(lf_s.reshape(bd, 1, N_HEADS))
        acc["sdn"].append(dn_s_s)
        acc["sdc"].append(dn_c_s)
        acc["slh"].append(lru_h_s)
        acc["slc"].append(lru_c_s)
        ys = _merge(ys, hs, out_a_s, out_b_s, out_c_s, out_m_s, w_g, b_g, wb, wo, l)

    st = lambda n: jnp.stack(acc[n])
    return (yp.reshape(bp, s, d), ys.reshape(bd, 1, d),
            st("pk"), st("pv"), st("plf"), st("pmk"), st("pmv"), st("pdn"), st("pdc"), st("plh"), st("plc"),
            st("sk"), st("sv"), st("slf"), st("sdn"), st("sdc"), st("slh"), st("slc"))
```

```python
import functools

import jax
import jax.numpy as jnp
from jax import lax
from jax.experimental import pallas as pl
from jax.experimental.pallas import tpu as pltpu

F32 = jnp.float32
BF16 = jnp.bfloat16
HIGHEST = lax.Precision.HIGHEST

EPS = 1e-6
LRU_C = 8.0
CONV_W = 4
N_HEADS = 4
HEAD_DIM = 128
BR_WIDTH = N_HEADS * HEAD_DIM
N_BRANCH = 4
LANES = 128
SUBLANES = 8
DN_CHUNK = 128
NEG_BIG = -1e30
VMEM_LIMIT_BYTES = 56 * 1024 * 1024

C_DNQKV = 0
C_DNZ = 1536
C_LRUX = 2048
C_LRUG = 2560
C_FQ = 3072
C_FK = 3584
C_FV = 4096
C_FG = 4608
C_MQ = 5120
C_MG = 5632
C_SMALL = 6144
N_PERM = 6400
INPROJ_TN = 1280
N_SMALL = 3 * N_HEADS
N_GATE = 4096
COLUMN_RUNS = ((0, 2048, 0), (2056, 4616, 2048), (4620, 6156, 4608), (2048, 2056, C_SMALL), (4616, 4620, C_SMALL + 8))
GATE_RUN = (6156, 10252)
SM_BETA = 0
SM_DECAY = 4
SM_FORGET = 8

NT_DIMS = (((1,), (1,)), ((), ()))


def _params(*sem):
    return pltpu.CompilerParams(dimension_semantics=sem, vmem_limit_bytes=VMEM_LIMIT_BYTES)


def _rms(x, w):
    return x * lax.rsqrt(jnp.mean(x * x, axis=-1, keepdims=True) + EPS) * w


def _l2norm(x):
    return x * lax.rsqrt(jnp.sum(x * x, axis=-1, keepdims=True) + EPS)


def _sigmoid(x):
    return 0.5 * jnp.tanh(0.5 * x) + 0.5


def _silu(x):
    return x * _sigmoid(x)


def _softplus(x):
    return jnp.maximum(x, 0.0) + jnp.log1p(jnp.exp(-jnp.abs(x)))


def _log_sigmoid(x):
    return -_softplus(-x)


def _dot(a, b):
    return jnp.dot(a.astype(BF16), b.astype(BF16), preferred_element_type=F32)


def _dot_nt(a, b):
    return lax.dot_general(a.astype(BF16), b.astype(BF16), NT_DIMS, preferred_element_type=F32)


def _dot_exact(a, b):
    return jnp.dot(a, b, precision=HIGHEST, preferred_element_type=F32)


def _iota2(shape, axis):
    return lax.broadcasted_iota(jnp.int32, shape, axis)


def _head(h):
    return slice(h * HEAD_DIM, (h + 1) * HEAD_DIM)


def _norm_kernel(x_ref, nw_ref, h_ref):
    h_ref[...] = _rms(x_ref[...], nw_ref[...]).astype(BF16)


def _norm(x, norm_w):
    t, d = x.shape
    tm = min(t, 1024)
    return pl.pallas_call(
        _norm_kernel,
        out_shape=jax.ShapeDtypeStruct((t, d), BF16),
        grid=(t // tm,),
        in_specs=[pl.BlockSpec((tm, d), lambda i: (i, 0)), pl.BlockSpec((1, d), lambda i: (0, 0))],
        out_specs=pl.BlockSpec((tm, d), lambda i: (i, 0)),
        compiler_params=_params("parallel"),
        name="norm",
    )(x, norm_w)


def _inproj_kernel(h_ref, w_ref, b_ref, o_ref):
    o_ref[...] = jnp.dot(h_ref[...], w_ref[...], preferred_element_type=F32) + b_ref[...]


def _repack_kernel(w_ref, o_ref, g_ref):
    def run(start, stop):
        lo = start // LANES * LANES
        hi = min(-(-stop // LANES) * LANES, w_ref.shape[-1])
        return w_ref[:, lo:hi][:, start - lo:stop - lo].astype(BF16)

    o_ref[...] = jnp.zeros_like(o_ref)
    for start, stop, dst in COLUMN_RUNS:
        o_ref[:, dst:dst + stop - start] = run(start, stop)
    g_ref[...] = run(*GATE_RUN)


def _repack_w_in(w_in):
    depth, d, n_in = w_in.shape
    tr = 64
    return pl.pallas_call(
        _repack_kernel,
        out_shape=(jax.ShapeDtypeStruct((depth, d, N_PERM), BF16),
                   jax.ShapeDtypeStruct((depth, d, N_GATE), BF16)),
        grid=(depth, d // tr),
        in_specs=[pl.BlockSpec((None, tr, n_in), lambda l, i: (l, i, 0))],
        out_specs=(pl.BlockSpec((None, tr, N_PERM), lambda l, i: (l, i, 0)),
                   pl.BlockSpec((None, tr, N_GATE), lambda l, i: (l, i, 0))),
        compiler_params=_params("parallel", "parallel"),
        name="repack_w_in",
    )(w_in)


def _inproj(hn, w_all, b, layer):
    t, d = hn.shape
    n = w_all.shape[2]
    tm = min(t, 1024)
    tn = INPROJ_TN
    return pl.pallas_call(
        _inproj_kernel,
        out_shape=jax.ShapeDtypeStruct((t, n), F32),
        grid=(t // tm, n // tn),
        in_specs=[pl.BlockSpec((tm, d), lambda i, j: (i, 0)),
                  pl.BlockSpec((None, d, tn), lambda i, j: (layer, 0, j)),
                  pl.BlockSpec((1, tn), lambda i, j: (0, j))],
        out_specs=pl.BlockSpec((tm, tn), lambda i, j: (i, j)),
        compiler_params=_params("parallel", "parallel"),
        name="inproj",
    )(hn, w_all, b)


def _fox_prep_kernel(h_ref, wq_ref, bq_ref, wg_ref, bg_ref, ws_ref, bs_ref, qw_ref, kw_ref,
                     qn_ref, kn_ref, vo_ref, kb_ref, vb_ref, gate_ref, lf_ref, c_ref, ct_ref, carry_ref, *, tm):
    @pl.when(pl.program_id(1) == 0)
    def _():
        carry_ref[...] = jnp.zeros_like(carry_ref)

    hn = h_ref[...]
    qkv = jnp.dot(hn, wq_ref[...], preferred_element_type=F32) + bq_ref[...]
    gate_ref[...] = _silu(jnp.dot(hn, wg_ref[...], preferred_element_type=F32) + bg_ref[...])
    sm = jnp.dot(hn, ws_ref[...], preferred_element_type=F32) + bs_ref[...]
    scale = HEAD_DIM ** -0.5
    for h in range(N_HEADS):
        qn_ref[:, _head(h)] = (_rms(qkv[:, _head(h)], qw_ref[...]) * scale).astype(qn_ref.dtype)
        kn = _rms(qkv[:, BR_WIDTH + h * HEAD_DIM:BR_WIDTH + (h + 1) * HEAD_DIM], kw_ref[...])
        v = qkv[:, 2 * BR_WIDTH + h * HEAD_DIM:2 * BR_WIDTH + (h + 1) * HEAD_DIM]
        kn_ref[pl.ds(h, tm, stride=N_HEADS), :] = kn
        vo_ref[pl.ds(h, tm, stride=N_HEADS), :] = v
        kb_ref[:, _head(h)] = kn.astype(BF16)
    vb_ref[...] = qkv[:, 2 * BR_WIDTH:].astype(BF16)
    lf = _log_sigmoid(sm)
    lf_ref[...] = lf[:, SM_FORGET:SM_FORGET + N_HEADS]
    tri = (_iota2((tm, tm), 0) >= _iota2((tm, tm), 1)).astype(F32)
    c = _dot_exact(tri, lf) + carry_ref[...]
    c_ref[...] = c
    ct_ref[...] = c.T
    carry_ref[...] = c[tm - 1:tm, :]


def _fox_prep(hn, w_p, b_p, layer, qn_w, kn_w, nb, s, q_dtype):
    t, d = hn.shape
    tm = min(s, 512)
    nj = s // tm
    row = lambda b, j: b * nj + j
    qkv_w = 3 * BR_WIDTH

    def cols(width, off):
        return (pl.BlockSpec((None, d, width), lambda b, j: (layer, 0, off // width)),
                pl.BlockSpec((1, width), lambda b, j: (0, off // width)))

    out_rows = pl.BlockSpec((tm, BR_WIDTH), lambda b, j: (row(b, j), 0))
    head_rows = pl.BlockSpec((tm * N_HEADS, HEAD_DIM), lambda b, j: (row(b, j), 0))
    wide_out = lambda dt: jax.ShapeDtypeStruct((t, BR_WIDTH), dt)
    head_out = jax.ShapeDtypeStruct((t * N_HEADS, HEAD_DIM), F32)
    return pl.pallas_call(
        functools.partial(_fox_prep_kernel, tm=tm),
        out_shape=(wide_out(q_dtype), head_out, head_out, wide_out(BF16), wide_out(BF16), wide_out(F32),
                   jax.ShapeDtypeStruct((t, N_HEADS), F32),
                   jax.ShapeDtypeStruct((t, LANES), F32),
                   jax.ShapeDtypeStruct((nb, LANES, s), F32)),
        grid=(nb, nj),
        in_specs=[pl.BlockSpec((tm, d), lambda b, j: (row(b, j), 0)),
                  *cols(qkv_w, C_FQ), *cols(BR_WIDTH, C_FG), *cols(LANES, C_SMALL),
                  pl.BlockSpec((1, HEAD_DIM), lambda b, j: (0, 0)),
                  pl.BlockSpec((1, HEAD_DIM), lambda b, j: (0, 0))],
        out_specs=(out_rows, head_rows, head_rows, out_rows, out_rows, out_rows,
                   pl.BlockSpec((tm, N_HEADS), lambda b, j: (row(b, j), 0)),
                   pl.BlockSpec((tm, LANES), lambda b, j: (row(b, j), 0)),
                   pl.BlockSpec((None, LANES, tm), lambda b, j: (b, 0, j))),
        scratch_shapes=[pltpu.VMEM((1, LANES), F32)],
        compiler_params=_params("parallel", "arbitrary"),
        name="fox_prep",
    )(hn, w_p, b_p, w_p, b_p, w_p, b_p, qn_w, kn_w)


def _fox_flash_kernel(q_ref, k_ref, v_ref, cq_ref, ck_ref, g_ref, o_ref, m_ref, acc_ref, cqb_ref, *, tq):
    i = pl.program_id(1)
    j = pl.program_id(2)
    lane_tiles = tq // LANES

    @pl.when(j == 0)
    def _():
        m_ref[...] = jnp.full_like(m_ref, NEG_BIG)
        acc_ref[...] = jnp.zeros_like(acc_ref)
        for h in range(N_HEADS):
            cqb_ref[h] = jnp.broadcast_to(cq_ref[:, SM_FORGET + h:SM_FORGET + h + 1], (tq, LANES))

    def update(diagonal):
        ones = jnp.ones((tq, HEAD_DIM), BF16)
        for h in range(N_HEADS):
            s = _dot_nt(q_ref[:, _head(h)], k_ref[:, _head(h)]) - ck_ref[h:h + 1, :]
            if diagonal:
                s = jnp.where(_iota2((tq, tq), 0) >= _iota2((tq, tq), 1), s, NEG_BIG)
            cqb = cqb_ref[h]
            m_prev = m_ref[h]
            m_new = jnp.maximum(m_prev, jnp.max(s, axis=-1, keepdims=True) + cqb)
            shift = m_new - cqb
            p = jnp.exp((s - jnp.concatenate([shift] * lane_tiles, axis=1)).astype(BF16))
            alpha = jnp.exp(m_prev - m_new)
            v_ones = jnp.concatenate([v_ref[:, _head(h)], ones], axis=1)
            acc_ref[h] = (jnp.concatenate([alpha, alpha], axis=1) * acc_ref[h]
                          + jnp.dot(p, v_ones, preferred_element_type=F32))
            m_ref[h] = m_new

    @pl.when(j < i)
    def _():
        update(False)

    @pl.when(j == i)
    def _():
        update(True)
        for h in range(N_HEADS):
            acc = acc_ref[h]
            o = acc[:, :HEAD_DIM] / acc[:, HEAD_DIM:]
            o_ref[:, _head(h)] = (o * g_ref[:, _head(h)]).astype(BF16)


def _fox_flash(qn, kn, v, c, ct, gate, nb, s):
    t = qn.shape[0]
    tq = min(s, 512)
    nq = s // tq
    qrow = lambda b, i, j: (b * nq + i, 0)
    krow = lambda b, i, j: (b * nq + jnp.minimum(i, j), 0)
    stat = pltpu.VMEM((N_HEADS, tq, LANES), F32)
    return pl.pallas_call(
        functools.partial(_fox_flash_kernel, tq=tq),
        out_shape=jax.ShapeDtypeStruct((t, BR_WIDTH), BF16),
        grid=(nb, nq, nq),
        in_specs=[pl.BlockSpec((tq, BR_WIDTH), qrow),
                  pl.BlockSpec((tq, BR_WIDTH), krow),
                  pl.BlockSpec((tq, BR_WIDTH), krow),
                  pl.BlockSpec((tq, LANES), qrow),
                  pl.BlockSpec((None, SUBLANES, tq), lambda b, i, j: (b, SM_FORGET // SUBLANES, jnp.minimum(i, j))),
                  pl.BlockSpec((tq, BR_WIDTH), qrow)],
        out_specs=pl.BlockSpec((tq, BR_WIDTH), qrow),
        scratch_shapes=[stat, pltpu.VMEM((N_HEADS, tq, 2 * HEAD_DIM), F32), stat],
        compiler_params=_params("parallel", "parallel", "arbitrary"),
        name="fox_flash",
    )(qn, kn, v, c, ct, gate)


def _mem_kv_kernel(m_ref, nw_ref, w_ref, kw_ref, mk_ref, mv_ref):
    kv = _dot(_rms(m_ref[...], nw_ref[...]), w_ref[...])
    for h in range(N_HEADS):
        mk_ref[:, _head(h)] = _rms(kv[:, _head(h)], kw_ref[...])
    mv_ref[...] = kv[:, BR_WIDTH:]


def _mem_kv(mem, norm_w, w_kv, kn_w):
    t, d = mem.shape
    tm = min(t, 256)
    return pl.pallas_call(
        _mem_kv_kernel,
        out_shape=(jax.ShapeDtypeStruct((t, BR_WIDTH), F32), jax.ShapeDtypeStruct((t, BR_WIDTH), F32)),
        grid=(t // tm,),
        in_specs=[pl.BlockSpec((tm, d), lambda i: (i, 0)),
                  pl.BlockSpec((1, d), lambda i: (0, 0)),
                  pl.BlockSpec((d, 2 * BR_WIDTH), lambda i: (0, 0)),
                  pl.BlockSpec((1, HEAD_DIM), lambda i: (0, 0))],
        out_specs=(pl.BlockSpec((tm, BR_WIDTH), lambda i: (i, 0)),
                   pl.BlockSpec((tm, BR_WIDTH), lambda i: (i, 0))),
        compiler_params=_params("parallel"),
        name="mem_kv",
    )(mem, norm_w, w_kv, kn_w)


def _mem_attn_kernel(h_ref, wp_ref, bp_ref, k_ref, v_ref, qw_ref, o_ref):
    proj = jnp.dot(h_ref[...], wp_ref[...], preferred_element_type=F32) + bp_ref[...]
    scale = HEAD_DIM ** -0.5
    for h in range(N_HEADS):
        q = _rms(proj[:, _head(h)], qw_ref[...]) * scale
        s = _dot_nt(q, k_ref[:, _head(h)])
        p = jnp.exp(s - jnp.max(s, axis=-1, keepdims=True))
        o = _dot(p, v_ref[:, _head(h)]) / jnp.sum(p, axis=-1, keepdims=True)
        gate = _silu(proj[:, BR_WIDTH + h * HEAD_DIM:BR_WIDTH + (h + 1) * HEAD_DIM])
        o_ref[:, _head(h)] = (o * gate).astype(BF16)


def _mem_attn(hn, w_p, b_p, layer, mk, mv, qn_w, nb, s):
    t, d = hn.shape
    pair_w = 2 * BR_WIDTH
    m = mk.shape[0] // nb
    tq = min(s, 512)
    nq = s // tq
    return pl.pallas_call(
        _mem_attn_kernel,
        out_shape=jax.ShapeDtypeStruct((t, BR_WIDTH), BF16),
        grid=(nb, nq),
        in_specs=[pl.BlockSpec((tq, d), lambda b, i: (b * nq + i, 0)),
                  pl.BlockSpec((None, d, pair_w), lambda b, i: (layer, 0, C_MQ // pair_w)),
                  pl.BlockSpec((1, pair_w), lambda b, i: (0, C_MQ // pair_w)),
                  pl.BlockSpec((m, BR_WIDTH), lambda b, i: (b, 0)),
                  pl.BlockSpec((m, BR_WIDTH), lambda b, i: (b, 0)),
                  pl.BlockSpec((1, HEAD_DIM), lambda b, i: (0, 0))],
        out_specs=pl.BlockSpec((tq, BR_WIDTH), lambda b, i: (b * nq + i, 0)),
        compiler_params=_params("parallel", "parallel"),
        name="mem_attn",
    )(hn, w_p, b_p, mk, mv, qn_w)


def _causal_conv_tile(x, halo, w_ref):
    rows8 = _iota2(halo.shape, 0)
    y = x * w_ref[CONV_W - 1:CONV_W, :]
    for k in range(1, CONV_W):
        xr = pltpu.roll(x, k, axis=0)
        hr = pltpu.roll(halo, k, axis=0)
        top = jnp.where(rows8 < k, hr, xr[:SUBLANES])
        xs = jnp.concatenate([top, xr[SUBLANES:]], axis=0)
        y = y + xs * w_ref[CONV_W - 1 - k:CONV_W - k, :]
    return y


def _unit_lower_inverses(mats, row, col):
    n = mats[0].shape[0]

    def off_block(shift):
        return (((row >> (shift + 1)) == (col >> (shift + 1)))
                & (((row >> shift) & 1) == 1) & (((col >> shift) & 1) == 0))

    eye = (row == col).astype(F32)
    first = off_block(0)
    ds = [eye - jnp.where(first, a, 0.0) for a in mats]
    shift = 1
    while (1 << shift) < n:
        mask = off_block(shift)
        ts = [_dot(jnp.where(mask, a, 0.0), d) for a, d in zip(mats, ds)]
        ds = [d - _dot(d, t) for d, t in zip(ds, ts)]
        shift += 1
    return ds


def _dn_kernel(h_ref, wm_ref, bm_ref, ws_ref, bs_ref, cw_ref, al_ref, dt_ref, ow_ref,
               o_ref, st_ref, tail_ref, stt_ref, halo_ref, *, tl):
    j = pl.program_id(1)

    @pl.when(j == 0)
    def _():
        stt_ref[...] = jnp.zeros_like(stt_ref)
        halo_ref[...] = jnp.zeros_like(halo_ref)

    hn = h_ref[...]
    proj = jnp.dot(hn, wm_ref[...], preferred_element_type=F32) + bm_ref[...]
    sm = jnp.dot(hn, ws_ref[...], preferred_element_type=F32) + bs_ref[...]
    x = proj[:, :3 * BR_WIDTH]
    z = proj[:, 3 * BR_WIDTH:]
    qkv = _silu(_causal_conv_tile(x, halo_ref[...], cw_ref))
    halo_ref[...] = x[tl - SUBLANES:, :]
    tail_ref[...] = x[tl - SUBLANES:, :]
    beta_all = _sigmoid(sm)
    la_all = -jnp.exp(al_ref[...]) * _softplus(sm + dt_ref[...])

    cc = DN_CHUNK
    row = _iota2((cc, cc), 0)
    col = _iota2((cc, cc), 1)
    incl = row >= col
    strict = row > col
    tri = incl.astype(F32)
    kd = N_HEADS * HEAD_DIM

    systems = []
    for c in range(tl // cc):
        rows = slice(c * cc, (c + 1) * cc)
        g_all = _dot_exact(tri, la_all[rows])
        gt_all = g_all.T
        eg_all = jnp.exp(g_all)
        for h in range(N_HEADS):
            q = _l2norm(qkv[rows, h * HEAD_DIM:(h + 1) * HEAD_DIM]) * (HEAD_DIM ** -0.5)
            k = _l2norm(qkv[rows, kd + h * HEAD_DIM:kd + (h + 1) * HEAD_DIM])
            v = qkv[rows, 2 * kd + h * HEAD_DIM:2 * kd + (h + 1) * HEAD_DIM]
            beta = beta_all[rows, SM_BETA + h:SM_BETA + h + 1]
            gc = g_all[:, SM_DECAY + h:SM_DECAY + h + 1]
            gr = gt_all[SM_DECAY + h:SM_DECAY + h + 1, :]
            eg = eg_all[:, SM_DECAY + h:SM_DECAY + h + 1]
            g_last = g_all[cc - 1:cc, SM_DECAY + h:SM_DECAY + h + 1]
            decay = jnp.exp(jnp.where(incl, gc - gr, NEG_BIG))
            kb = k.astype(BF16)
            qb = q.astype(BF16)
            systems.append(dict(
                a=jnp.where(strict, beta * decay * _dot_nt(kb, kb), 0.0),
                qk=(decay * _dot_nt(qb, kb)).astype(BF16),
                rhs=jnp.concatenate([beta * v, (beta * eg) * k], axis=1).astype(BF16),
                kdec_t=(jnp.exp(g_last - gc) * k).T.astype(BF16),
                q=qb, eg=eg, g_end=jnp.exp(g_last)))

    ds = _unit_lower_inverses([sy["a"] for sy in systems], row, col)
    sols = [_dot(d, sy["rhs"]) for d, sy in zip(ds, systems)]

    heads = range(N_HEADS)
    stts = [stt_ref[h] for h in heads]
    for c in range(tl // cc):
        rows = slice(c * cc, (c + 1) * cc)
        sys_c = systems[c * N_HEADS:(c + 1) * N_HEADS]
        sol_c = sols[c * N_HEADS:(c + 1) * N_HEADS]
        us = [sol_c[h][:, :HEAD_DIM] - _dot(sol_c[h][:, HEAD_DIM:], stts[h]) for h in heads]
        os = [sys_c[h]["eg"] * _dot(sys_c[h]["q"], stts[h]) + _dot(sys_c[h]["qk"], us[h]) for h in heads]
        stts = [sys_c[h]["g_end"] * stts[h] + _dot(sys_c[h]["kdec_t"], us[h]) for h in heads]
        for h in heads:
            o = _rms(os[h], ow_ref[...]) * _silu(z[rows, _head(h)])
            o_ref[rows, _head(h)] = o.astype(BF16)
    for h in heads:
        stt_ref[h] = stts[h]

    @pl.when(j == pl.num_programs(1) - 1)
    def _():
        for h in range(N_HEADS):
            st_ref[h] = stt_ref[h].T


def _dn_prompt(hn, w_p, b_p, layer, conv_w, al_row, dt_row, onorm_w, nb, s):
    t, d = hn.shape
    tl = min(s, 512)
    nj = s // tl
    qkv_w = 3 * BR_WIDTH
    main_w = qkv_w + BR_WIDTH
    row = lambda b, j: b * nj + j
    return pl.pallas_call(
        functools.partial(_dn_kernel, tl=tl),
        out_shape=(jax.ShapeDtypeStruct((t, BR_WIDTH), BF16),
                   jax.ShapeDtypeStruct((nb, N_HEADS, HEAD_DIM, HEAD_DIM), F32),
                   jax.ShapeDtypeStruct((nb, SUBLANES, qkv_w), F32)),
        grid=(nb, nj),
        in_specs=[pl.BlockSpec((tl, d), lambda b, j: (row(b, j), 0)),
                  pl.BlockSpec((None, d, main_w), lambda b, j: (layer, 0, C_DNQKV // main_w)),
                  pl.BlockSpec((1, main_w), lambda b, j: (0, C_DNQKV // main_w)),
                  pl.BlockSpec((None, d, LANES), lambda b, j: (layer, 0, C_SMALL // LANES)),
                  pl.BlockSpec((1, LANES), lambda b, j: (0, C_SMALL // LANES)),
                  pl.BlockSpec((CONV_W, qkv_w), lambda b, j: (0, 0)),
                  pl.BlockSpec((1, LANES), lambda b, j: (0, 0)),
                  pl.BlockSpec((1, LANES), lambda b, j: (0, 0)),
                  pl.BlockSpec((1, HEAD_DIM), lambda b, j: (0, 0))],
        out_specs=(pl.BlockSpec((tl, BR_WIDTH), lambda b, j: (row(b, j), 0)),
                   pl.BlockSpec((None, N_HEADS, HEAD_DIM, HEAD_DIM), lambda b, j: (b, 0, 0, 0)),
                   pl.BlockSpec((None, SUBLANES, qkv_w), lambda b, j: (b, 0, 0))),
        scratch_shapes=[pltpu.VMEM((N_HEADS, HEAD_DIM, HEAD_DIM), F32),
                        pltpu.VMEM((SUBLANES, qkv_w), F32)],
        compiler_params=_params("parallel", "arbitrary"),
        name="dn_prompt",
    )(hn, w_p, b_p, w_p, b_p, conv_w, al_row, dt_row, onorm_w)


def _lru_gates(xc, wr_ref, br_ref, wi_ref, bi_ref, lam_ref):
    r = _sigmoid(_dot(xc, wr_ref[...]) + br_ref[...])
    i = _sigmoid(_dot(xc, wi_ref[...]) + bi_ref[...])
    log_a = -LRU_C * r * _softplus(-lam_ref[...])
    a = jnp.exp(log_a)
    one_minus_a2 = -jnp.tanh(log_a) * (a * a + 1.0)
    return a, jnp.sqrt(one_minus_a2) * (i * xc)


def _lru_kernel(h_ref, wp_ref, bp_ref, cw_ref, cb_ref, wr_ref, br_ref, wi_ref, bi_ref, lam_ref,
                o_ref, hl_ref, tail_ref, hc_ref, halo_ref, *, tl):
    @pl.when(pl.program_id(1) == 0)
    def _():
        hc_ref[...] = jnp.zeros_like(hc_ref)
        halo_ref[...] = jnp.zeros_like(halo_ref)

    proj = jnp.dot(h_ref[...], wp_ref[...], preferred_element_type=F32) + bp_ref[...]
    x = proj[:, :BR_WIDTH]
    xc = _causal_conv_tile(x, halo_ref[...], cw_ref) + cb_ref[...]
    halo_ref[...] = x[tl - SUBLANES:, :]
    tail_ref[...] = x[tl - SUBLANES:, :]
    a, b = _lru_gates(xc, wr_ref, br_ref, wi_ref, bi_ref, lam_ref)
    rows = _iota2(a.shape, 0)
    d = 1
    while d < tl:
        a_up = jnp.where(rows >= d, pltpu.roll(a, d, axis=0), 1.0)
        b_up = jnp.where(rows >= d, pltpu.roll(b, d, axis=0), 0.0)
        b = a * b_up + b
        a = a * a_up
        d *= 2
    hseq = a * hc_ref[...] + b
    hc_ref[...] = hseq[tl - 1:tl, :]
    hl_ref[...] = hseq[tl - 1:tl, :]
    o_ref[...] = (hseq * _silu(proj[:, BR_WIDTH:])).astype(BF16)


def _lru_prompt(hn, w_p, b_p, layer, conv_w, conv_b, w_r, b_r, w_i, b_i, lam, nb, s):
    t, d = hn.shape
    pair_w = 2 * BR_WIDTH
    tl = min(s, 512)
    nj = s // tl
    row = lambda b, j: b * nj + j
    vec = pl.BlockSpec((1, BR_WIDTH), lambda b, j: (0, 0))
    mat = pl.BlockSpec((BR_WIDTH, BR_WIDTH), lambda b, j: (0, 0))
    return pl.pallas_call(
        functools.partial(_lru_kernel, tl=tl),
        out_shape=(jax.ShapeDtypeStruct((t, BR_WIDTH), BF16),
                   jax.ShapeDtypeStruct((nb, 1, BR_WIDTH), F32),
                   jax.ShapeDtypeStruct((nb, SUBLANES, BR_WIDTH), F32)),
        grid=(nb, nj),
        in_specs=[pl.BlockSpec((tl, d), lambda b, j: (row(b, j), 0)),
                  pl.BlockSpec((None, d, pair_w), lambda b, j: (layer, 0, C_LRUX // pair_w)),
                  pl.BlockSpec((1, pair_w), lambda b, j: (0, C_LRUX // pair_w)),
                  pl.BlockSpec((CONV_W, BR_WIDTH), lambda b, j: (0, 0)),
                  vec, mat, vec, mat, vec, vec],
        out_specs=(pl.BlockSpec((tl, BR_WIDTH), lambda b, j: (row(b, j), 0)),
                   pl.BlockSpec((None, 1, BR_WIDTH), lambda b, j: (b, 0, 0)),
                   pl.BlockSpec((None, SUBLANES, BR_WIDTH), lambda b, j: (b, 0, 0))),
        scratch_shapes=[pltpu.VMEM((1, BR_WIDTH), F32), pltpu.VMEM((SUBLANES, BR_WIDTH), F32)],
        compiler_params=_params("parallel", "arbitrary"),
        name="lru_prompt",
    )(hn, w_p, b_p, conv_w, conv_b, w_r, b_r, w_i, b_i, lam)


def _merge_kernel(x_ref, h_ref, a_ref, b_ref, c_ref, m_ref, wg_ref, bg_ref, wb_ref, wo_ref, y_ref):
    x = x_ref[...]
    d = x.shape[1]
    h = h_ref[...]
    merged = None
    for n, br in enumerate((a_ref, b_ref, c_ref, m_ref)):
        cols = slice(n * d, (n + 1) * d)
        gate = jnp.dot(h, wg_ref[:, cols], preferred_element_type=F32) + bg_ref[:, cols]
        up = jnp.dot(br[...], wb_ref[n], preferred_element_type=F32)
        term = _sigmoid(gate) * up
        merged = term if merged is None else merged + term
    y_ref[...] = x + jnp.dot(merged.astype(BF16), wo_ref[...], preferred_element_type=F32)


def _merge(x, hn, out_a, out_b, out_c, out_m, w_gate, b_gate, w_branch, w_out, layer):
    t, d = x.shape
    tm = min(t, 512)
    rows = lambda w: pl.BlockSpec((tm, w), lambda i: (i, 0))
    return pl.pallas_call(
        _merge_kernel,
        out_shape=jax.ShapeDtypeStruct((t, d), F32),
        grid=(t // tm,),
        in_specs=[rows(d), rows(d),
                  rows(BR_WIDTH), rows(BR_WIDTH), rows(BR_WIDTH), rows(BR_WIDTH),
                  pl.BlockSpec((None, d, N_GATE), lambda i: (layer, 0, 0)),
                  pl.BlockSpec((1, N_GATE), lambda i: (0, 0)),
                  pl.BlockSpec((N_BRANCH, BR_WIDTH, d), lambda i: (0, 0, 0)),
                  pl.BlockSpec((d, d), lambda i: (0, 0))],
        out_specs=rows(d),
        compiler_params=_params("parallel"),
        name="merge",
    )(x, hn, out_a, out_b, out_c, out_m, w_gate, b_gate, w_branch, w_out)


def _sample_step_kernel(qkv_ref, z_ref, lx_ref, lg_ref, sm_ref, st_ref, dbuf_ref, lh_ref, lbuf_ref,
                        dcw_ref, al_ref, dt_ref, ow_ref, lcw_ref, lcb_ref, wr_ref, br_ref, wi_ref, bi_ref, lam_ref,
                        oa_ref, ob_ref, sto_ref, dbo_ref, lho_ref, lbo_ref, o_scr, *, bs):
    kd = N_HEADS * HEAD_DIM

    def step_conv(x, buf_ref, bufo_ref, w_ref):
        y = x * w_ref[CONV_W - 1:CONV_W, :]
        for k in range(CONV_W - 1):
            y = y + buf_ref[:, k, :] * w_ref[k:k + 1, :]
        for k in range(CONV_W - 2):
            bufo_ref[:, k, :] = buf_ref[:, k + 1, :]
        bufo_ref[:, CONV_W - 2, :] = x
        return y

    qkv = _silu(step_conv(qkv_ref[...], dbuf_ref, dbo_ref, dcw_ref))
    sm = sm_ref[...]
    beta_all = _sigmoid(sm)
    eg_all = jnp.exp(-jnp.exp(al_ref[...]) * _softplus(sm + dt_ref[...]))
    eye = _iota2((HEAD_DIM, HEAD_DIM), 0) == _iota2((HEAD_DIM, HEAD_DIM), 1)
    pad = jnp.zeros((SUBLANES - 2, HEAD_DIM), F32)
    items = []
    for h in range(N_HEADS):
        q = _l2norm(qkv[:, h * HEAD_DIM:(h + 1) * HEAD_DIM]) * (HEAD_DIM ** -0.5)
        k = _l2norm(qkv[:, kd + h * HEAD_DIM:kd + (h + 1) * HEAD_DIM])
        v = qkv[:, 2 * kd + h * HEAD_DIM:2 * kd + (h + 1) * HEAD_DIM]
        beta = beta_all[:, SM_BETA + h:SM_BETA + h + 1]
        eg = eg_all[:, SM_DECAY + h:SM_DECAY + h + 1]
        qk = jnp.sum(q * k, axis=-1, keepdims=True)
        for b in range(bs):
            one = slice(b, b + 1)
            items.append(dict(b=b, h=h, q=q[one], k=k[one], v=v[one], beta=beta[one], eg=eg[one], qk=qk[one]))
    sk_sqs = [_dot_nt(jnp.concatenate([it["k"], it["q"], pad], axis=0), st_ref[it["b"], it["h"]])
              for it in items]
    us = [it["beta"] * (it["v"] - it["eg"] * r[0:1]) for it, r in zip(items, sk_sqs)]
    for it, r, u in zip(items, sk_sqs, us):
        o_scr[it["b"]:it["b"] + 1, _head(it["h"])] = it["eg"] * r[1:2] + it["qk"] * u
    updates = [_dot(jnp.where(eye, jnp.broadcast_to(u, (HEAD_DIM, HEAD_DIM)), 0.0),
                    jnp.broadcast_to(it["k"], (HEAD_DIM, HEAD_DIM))) for it, u in zip(items, us)]
    for it, upd in zip(items, updates):
        sto_ref[it["b"], it["h"]] = it["eg"] * st_ref[it["b"], it["h"]] + upd
    o = o_scr[...]
    for h in range(N_HEADS):
        oa_ref[:, _head(h)] = (_rms(o[:, _head(h)], ow_ref[...]) * _silu(z_ref[:, _head(h)])).astype(BF16)

    xc = step_conv(lx_ref[...], lbuf_ref, lbo_ref, lcw_ref) + lcb_ref[...]
    a, bx = _lru_gates(xc, wr_ref, br_ref, wi_ref, bi_ref, lam_ref)
    hnew = a * lh_ref[...] + bx
    lho_ref[...] = hnew
    ob_ref[...] = (hnew * _silu(lg_ref[...])).astype(BF16)


def _sample_step(proj, layer, state_dn, state_dn_conv, state_lru_h, state_lru_conv,
                 dn_conv_w, al_row, dt_row, onorm_w, lru_conv_w, lru_conv_b, w_r, b_r, w_i, b_i, lam):
    nb = proj.shape[0]
    bs = SUBLANES
    qkv_w = 3 * BR_WIDTH
    wide = lambda off: pl.BlockSpec((bs, BR_WIDTH), lambda i: (i, off // BR_WIDTH))
    const = lambda shape: pl.BlockSpec(shape, lambda i: (0,) * len(shape))
    st_in = pl.BlockSpec((None, bs, N_HEADS, HEAD_DIM, HEAD_DIM), lambda i: (layer, i, 0, 0, 0))
    dbuf_in = pl.BlockSpec((None, bs, CONV_W - 1, qkv_w), lambda i: (layer, i, 0, 0))
    lh_in = pl.BlockSpec((None, bs, BR_WIDTH), lambda i: (layer, i, 0))
    lbuf_in = pl.BlockSpec((None, bs, CONV_W - 1, BR_WIDTH), lambda i: (layer, i, 0, 0))
    return pl.pallas_call(
        functools.partial(_sample_step_kernel, bs=bs),
        out_shape=(jax.ShapeDtypeStruct((nb, BR_WIDTH), BF16),
                   jax.ShapeDtypeStruct((nb, BR_WIDTH), BF16),
                   jax.ShapeDtypeStruct((nb, N_HEADS, HEAD_DIM, HEAD_DIM), F32),
                   jax.ShapeDtypeStruct((nb, CONV_W - 1, qkv_w), F32),
                   jax.ShapeDtypeStruct((nb, BR_WIDTH), F32),
                   jax.ShapeDtypeStruct((nb, CONV_W - 1, BR_WIDTH), F32)),
        grid=(nb // bs,),
        in_specs=[pl.BlockSpec((bs, qkv_w), lambda i: (i, 0)), wide(C_DNZ), wide(C_LRUX), wide(C_LRUG),
                  pl.BlockSpec((bs, LANES), lambda i: (i, C_SMALL // LANES)),
                  st_in, dbuf_in, lh_in, lbuf_in,
                  const((CONV_W, qkv_w)), const((1, LANES)), const((1, LANES)), const((1, HEAD_DIM)),
                  const((CONV_W, BR_WIDTH)), const((1, BR_WIDTH)),
                  const((BR_WIDTH, BR_WIDTH)), const((1, BR_WIDTH)),
                  const((BR_WIDTH, BR_WIDTH)), const((1, BR_WIDTH)), const((1, BR_WIDTH))],
        out_specs=(pl.BlockSpec((bs, BR_WIDTH), lambda i: (i, 0)),
                   pl.BlockSpec((bs, BR_WIDTH), lambda i: (i, 0)),
                   pl.BlockSpec((bs, N_HEADS, HEAD_DIM, HEAD_DIM), lambda i: (i, 0, 0, 0)),
                   pl.BlockSpec((bs, CONV_W - 1, qkv_w), lambda i: (i, 0, 0)),
                   pl.BlockSpec((bs, BR_WIDTH), lambda i: (i, 0)),
                   pl.BlockSpec((bs, CONV_W - 1, BR_WIDTH), lambda i: (i, 0, 0))),
        scratch_shapes=[pltpu.VMEM((bs, BR_WIDTH), F32)],
        compiler_params=_params("parallel"),
        name="sample_step",
    )(proj, proj, proj, proj, proj, state_dn, state_dn_conv, state_lru_h, state_lru_conv,
      dn_conv_w, al_row, dt_row, onorm_w, lru_conv_w, lru_conv_b, w_r, b_r, w_i, b_i, lam)


def _own_head_mask(n_rows_kv):
    shape = (SUBLANES, n_rows_kv)
    return (_iota2(shape, 1) % N_HEADS) == _iota2(shape, 0)


def _heads_to_row(o8):
    return jnp.concatenate([o8[h:h + 1] for h in range(N_HEADS)], axis=1)


def _split3(x):
    hi = x.astype(BF16)
    r1 = x - hi.astype(F32)
    mid = r1.astype(BF16)
    lo = (r1 - mid.astype(F32)).astype(BF16)
    return hi, mid, lo


def _pool_bias_kernel(lf_ref, after_ref, same_ref, inpage_ref, total_ref):
    in_page = None
    total = None
    for h in range(N_HEADS):
        for piece in _split3(lf_ref[:, h, :]):
            a = jnp.dot(piece, after_ref[h], preferred_element_type=F32)
            t = jnp.dot(piece, same_ref[h], preferred_element_type=F32)
            in_page = a if in_page is None else in_page + a
            total = t if total is None else total + t
    inpage_ref[...] = in_page
    total_ref[...] = total


def _pool_bias(cache_lf):
    n, nh, page = cache_lf.shape
    w = page * nh
    tp = min(n, 512)
    src_tok = jnp.arange(page)[None, :, None]
    src_head = jnp.arange(nh)[:, None, None]
    dst = jnp.arange(w)[None, None, :]
    same = (dst % nh) == src_head
    after = same & (src_tok > dst // nh)
    rows = pl.BlockSpec((tp, w), lambda i: (i, 0))
    const = pl.BlockSpec((nh, page, w), lambda i: (0, 0, 0))
    return pl.pallas_call(
        _pool_bias_kernel,
        out_shape=(jax.ShapeDtypeStruct((n, w), F32), jax.ShapeDtypeStruct((n, w), F32)),
        grid=(n // tp,),
        in_specs=[pl.BlockSpec((tp, nh, page), lambda i: (i, 0, 0)), const, const],
        out_specs=(rows, rows),
        compiler_params=_params("parallel"),
        name="pool_bias",
    )(cache_lf, after.astype(BF16), jnp.broadcast_to(same, after.shape).astype(BF16))


def _fox_decode_kernel(pt_ref, q_ref, kn_ref, vn_ref, lfn_ref, g_ref, *rest, n_pages):
    k_refs = rest[:n_pages]
    v_refs = rest[n_pages:2 * n_pages]
    inpage_refs = rest[2 * n_pages:3 * n_pages]
    total_refs = rest[3 * n_pages:4 * n_pages]
    o_ref = rest[4 * n_pages]
    del pt_ref
    q8 = q_ref[...]
    qb = q8.astype(BF16)
    own = _own_head_mask(k_refs[0].shape[0])

    later = lfn_ref[...]
    scores = [None] * n_pages
    for p in reversed(range(n_pages)):
        s = _dot_nt(qb, k_refs[p][...]) + (inpage_refs[p][...] + later)
        scores[p] = jnp.where(own, s, NEG_BIG)
        later = later + total_refs[p][...]
    s_self = jnp.sum(q8 * kn_ref[...], axis=-1, keepdims=True)
    m = s_self
    for s in scores:
        m = jnp.maximum(m, jnp.max(s, axis=-1, keepdims=True))
    p_self = jnp.exp(s_self - m)
    l = p_self
    acc = p_self * vn_ref[...]
    for p in range(n_pages):
        pr = jnp.exp(scores[p] - m)
        l = l + jnp.sum(pr, axis=-1, keepdims=True)
        acc = acc + _dot(pr, v_refs[p][...])
    o_ref[...] = (_heads_to_row(acc / l) * g_ref[...]).astype(BF16)


def _fox_decode(q8, k8, v8, lf_new, gate, cache_k, cache_v, in_page, totals, page_table, layer, n_pool):
    nb = q8.shape[0]
    n_pages = page_table.shape[0] // nb
    kv_rows = cache_k.shape[1]
    base = layer * n_pool
    row = pl.BlockSpec((None, 1, BR_WIDTH), lambda b, pt: (b, 0, 0))
    heads = pl.BlockSpec((None, SUBLANES, HEAD_DIM), lambda b, pt: (b, 0, 0))

    def paged(shape, p):
        return pl.BlockSpec((None,) + shape, lambda b, pt: (base + pt[b * n_pages + p], 0, 0))

    in_specs = [heads, heads, heads, pl.BlockSpec((None, 1, kv_rows), lambda b, pt: (b, 0, 0)), row]
    in_specs += [paged((kv_rows, HEAD_DIM), p) for p in range(n_pages)]
    in_specs += [paged((kv_rows, HEAD_DIM), p) for p in range(n_pages)]
    in_specs += [paged((1, kv_rows), p) for p in range(n_pages)]
    in_specs += [paged((1, kv_rows), p) for p in range(n_pages)]
    grid_spec = pltpu.PrefetchScalarGridSpec(
        num_scalar_prefetch=1,
        grid=(nb,),
        in_specs=in_specs,
        out_specs=row)
    return pl.pallas_call(
        functools.partial(_fox_decode_kernel, n_pages=n_pages),
        out_shape=jax.ShapeDtypeStruct((nb, 1, BR_WIDTH), BF16),
        grid_spec=grid_spec,
        compiler_params=_params("parallel"),
        name="fox_decode",
    )(page_table, q8, k8, v8, lf_new, gate,
      *([cache_k] * n_pages), *([cache_v] * n_pages), *([in_page] * n_pages), *([totals] * n_pages))


def _mem_decode_kernel(q_ref, g_ref, k_ref, v_ref, qw_ref, o_ref, o_scr, *, bs):
    own = _own_head_mask(k_ref.shape[1])
    scale = HEAD_DIM ** -0.5
    qs = [_rms(q_ref[:, _head(h)], qw_ref[...]) * scale for h in range(N_HEADS)]
    pad = jnp.zeros((SUBLANES - N_HEADS, HEAD_DIM), F32)
    for b in range(bs):
        q8 = jnp.concatenate([q[b:b + 1] for q in qs] + [pad], axis=0)
        s = jnp.where(own, _dot_nt(q8, k_ref[b]), NEG_BIG)
        p = jnp.exp(s - jnp.max(s, axis=-1, keepdims=True))
        o = _dot(p, v_ref[b]) / jnp.sum(p, axis=-1, keepdims=True)
        o_scr[b:b + 1, :] = _heads_to_row(o)
    o_ref[...] = (o_scr[...] * _silu(g_ref[...])).astype(BF16)


def _mem_decode(proj, cache_k, cache_v, qn_w, layer):
    nb = proj.shape[0]
    bs = SUBLANES
    m = cache_k.shape[1]
    base = layer * (nb // bs)
    kv = pl.BlockSpec((bs, m, HEAD_DIM), lambda i: (base + i, 0, 0))
    return pl.pallas_call(
        functools.partial(_mem_decode_kernel, bs=bs),
        out_shape=jax.ShapeDtypeStruct((nb, BR_WIDTH), BF16),
        grid=(nb // bs,),
        in_specs=[pl.BlockSpec((bs, BR_WIDTH), lambda i: (i, C_MQ // BR_WIDTH)),
                  pl.BlockSpec((bs, BR_WIDTH), lambda i: (i, C_MG // BR_WIDTH)),
                  kv, kv, pl.BlockSpec((1, HEAD_DIM), lambda i: (0, 0))],
        out_specs=pl.BlockSpec((bs, BR_WIDTH), lambda i: (i, 0)),
        scratch_shapes=[pltpu.VMEM((bs, BR_WIDTH), F32)],
        compiler_params=_params("parallel"),
        name="mem_decode",
    )(proj, proj, cache_k, cache_v, qn_w)


def _permute_columns(a):
    runs = sorted(COLUMN_RUNS, key=lambda r: r[2])
    pad = jnp.zeros(a.shape[:-1] + (N_PERM - C_SMALL - N_SMALL,), a.dtype)
    return jnp.concatenate([a[..., start:stop] for start, stop, _ in runs] + [pad], axis=-1)


def _block_diag(w):
    nblk, e, f = w.shape
    eye = jnp.eye(nblk, dtype=w.dtype)
    return (eye[:, None, :, None] * w[:, :, None, :]).reshape(nblk * e, nblk * f)


def _decay_lane_row(v):
    return jnp.zeros((1, LANES), F32).at[0, SM_DECAY:SM_DECAY + N_HEADS].set(v)


def kernel(x_prompt, x_sample, cache_fox_k, cache_fox_v, cache_fox_logf, cache_mem_k, cache_mem_v, state_dn, state_dn_conv, state_lru_h, state_lru_conv, page_table, mem_prompt, norm_w, w_in, b_in, dn_conv_w, dn_A_log, dn_dt_bias, dn_onorm_w, lru_conv_w, lru_conv_b, lru_w_r, lru_b_r, lru_w_i, lru_b_i, lru_lambda, fox_qn_w, fox_kn_w, mem_norm_w, w_mem_kv, mem_qn_w, mem_kn_w, w_branch, w_out):
    bp, s, d = x_prompt.shape
    bd = x_sample.shape[0]
    depth = w_in.shape[0]
    n_pool, page = cache_fox_k.shape[1], cache_fox_k.shape[2]
    mem_tokens = mem_prompt.shape[1]
    tp = bp * s

    yp = x_prompt.reshape(tp, d)
    ys = x_sample.reshape(bd, d)
    mem2 = mem_prompt.reshape(bp * mem_tokens, d)
    kv_rows = page * N_HEADS
    cache_k2 = cache_fox_k.reshape(depth * n_pool, kv_rows, HEAD_DIM)
    cache_v2 = cache_fox_v.reshape(depth * n_pool, kv_rows, HEAD_DIM)
    in_page, totals = _pool_bias(jnp.swapaxes(cache_fox_logf, 2, 3).reshape(depth * n_pool, N_HEADS, page))
    in_page = in_page.reshape(depth * n_pool, 1, kv_rows)
    totals = totals.reshape(depth * n_pool, 1, kv_rows)
    cmem_k2 = cache_mem_k.reshape(depth * bd, mem_tokens * N_HEADS, HEAD_DIM)
    cmem_v2 = cache_mem_v.reshape(depth * bd, mem_tokens * N_HEADS, HEAD_DIM)
    pt_flat = page_table.reshape(-1)

    row = lambda v: v.reshape(1, -1)
    acc = {n: [] for n in ("pk", "pv", "plf", "pmk", "pmv", "pdn", "pdc", "plh", "plc",
                           "sk", "sv", "slf", "sdn", "sdc", "slh", "slc")}
    w_p, w_g = _repack_w_in(w_in)
    for l in range(depth):
        b_p = _permute_columns(b_in[l]).reshape(1, N_PERM)
        b_g = b_in[l, GATE_RUN[0]:GATE_RUN[1]].reshape(1, N_GATE)
        nw = row(norm_w[l])
        al_row = _decay_lane_row(dn_A_log[l])
        dt_row = _decay_lane_row(dn_dt_bias[l])
        ow = row(dn_onorm_w[l])
        wr = _block_diag(lru_w_r[l]).astype(BF16)
        wi = _block_diag(lru_w_i[l]).astype(BF16)
        lru_args = (lru_conv_w[l], row(lru_conv_b[l]), wr, row(lru_b_r[l]), wi, row(lru_b_i[l]), row(lru_lambda[l]))
        wb = w_branch[l].astype(BF16)
        wo = w_out[l].astype(BF16)
        fqw, fkw, mqw = row(fox_qn_w[l]), row(fox_kn_w[l]), row(mem_qn_w[l])

        hp = _norm(yp, nw)
        qb, kn, fv, kb, vb, gate_c, lf, c, ct = _fox_prep(hp, w_p, b_p, l, fqw, fkw, bp, s, BF16)
        out_c = _fox_flash(qb, kb, vb, c, ct, gate_c, bp, s)
        mk, mv = _mem_kv(mem2, row(mem_norm_w[l]), w_mem_kv[l].astype(BF16), row(mem_kn_w[l]))
        out_m = _mem_attn(hp, w_p, b_p, l, mk, mv, mqw, bp, s)
        out_a, dn_s, dn_tail = _dn_prompt(hp, w_p, b_p, l, dn_conv_w[l], al_row, dt_row, ow, bp, s)
        out_b, lru_h, lru_tail = _lru_prompt(hp, w_p, b_p, l, *lru_args, bp, s)
        acc["pk"].append(kn.reshape(bp, s, N_HEADS, HEAD_DIM))
        acc["pv"].append(fv.reshape(bp, s, N_HEADS, HEAD_DIM))
        acc["plf"].append(lf.reshape(bp, s, N_HEADS))
        acc["pmk"].append(mk.reshape(bp, mem_tokens, N_HEADS, HEAD_DIM))
        acc["pmv"].append(mv.reshape(bp, mem_tokens, N_HEADS, HEAD_DIM))
        acc["pdn"].append(dn_s)
        acc["pdc"].append(dn_tail[:, SUBLANES - (CONV_W - 1):, :])
        acc["plh"].append(lru_h.reshape(bp, BR_WIDTH))
        acc["plc"].append(lru_tail[:, SUBLANES - (CONV_W - 1):, :])
        yp = _merge(yp, hp, out_a, out_b, out_c, out_m, w_g, b_g, wb, wo, l)

        hs = _norm(ys, nw)
        proj_s = _inproj(hs, w_p, b_p, l)
        qn_s, kn_s, fv_s, _, _, gate_s, lf_s, _, _ = _fox_prep(hs, w_p, b_p, l, fqw, fkw, 1, bd, F32)
        head_rows = lambda a: jnp.pad(a.reshape(bd, N_HEADS, HEAD_DIM), ((0, 0), (0, SUBLANES - N_HEADS), (0, 0)))
        lf_new = jnp.tile(lf_s, (1, page)).reshape(bd, 1, kv_rows)
        out_c_s = _fox_decode(head_rows(qn_s), head_rows(kn_s), head_rows(fv_s), lf_new,
                              gate_s.reshape(bd, 1, BR_WIDTH),
                              cache_k2, cache_v2, in_page, totals, pt_flat, l, n_pool).reshape(bd, BR_WIDTH)
        out_m_s = _mem_decode(proj_s, cmem_k2, cmem_v2, mqw, l)
        out_a_s, out_b_s, dn_s_s, dn_c_s, lru_h_s, lru_c_s = _sample_step(
            proj_s, l, state_dn, state_dn_conv, state_lru_h, state_lru_conv,
            dn_conv_w[l], al_row, dt_row, ow, *lru_args)
        acc["sk"].append(kn_s.reshape(bd, 1, N_HEADS, HEAD_DIM))
        acc["sv"].append(fv_s.reshape(bd, 1, N_HEADS, HEAD_DIM))
        acc["slf"].append(lf_s.reshape(bd, 1, N_HEADS))
        acc["sdn"].append(dn_s_s)
        acc["sdc"].append(dn_c_s)
        acc["slh"].append(lru_h_s)
        acc["slc"].append(lru_c_s)
        ys = _merge(ys, hs, out_a_s, out_b_s, out_c_s, out_m_s, w_g, b_g, wb, wo, l)

    st = lambda n: jnp.stack(acc[n])
    return (yp.reshape(bp, s, d), ys.reshape(bd, 1, d),
            st("pk"), st("pv"), st("plf"), st("pmk"), st("pmv"), st("pdn"), st("pdc"), st("plh"), st("plc"),
            st("sk"), st("sv"), st("slf"), st("sdn"), st("sdc"), st("slh"), st("slc"))
```

```python
import functools

import jax
import jax.numpy as jnp
from jax import lax
from jax.experimental import pallas as pl
from jax.experimental.pallas import tpu as pltpu

F32 = jnp.float32
BF16 = jnp.bfloat16

EPS = 1e-6
LRU_C = 8.0
CONV_W = 4
N_HEADS = 4
HEAD_DIM = 128
BR_WIDTH = N_HEADS * HEAD_DIM
N_BRANCH = 4
LANES = 128
SUBLANES = 8
DN_CHUNK = 128
DECODE_SEQS = 2
NEG_BIG = -1e30
VMEM_LIMIT_BYTES = 56 * 1024 * 1024

C_DNQKV = 0
C_DNZ = 1536
C_LRUX = 2048
C_LRUG = 2560
C_FQ = 3072
C_FK = 3584
C_FV = 4096
C_FG = 4608
C_MQ = 5120
C_MG = 5632
C_SMALL = 6144
N_PERM = 6400
INPROJ_TN = 1280
N_SMALL = 3 * N_HEADS
N_GATE = 4096
COLUMN_RUNS = ((0, 2048, 0), (2056, 4616, 2048), (4620, 6156, 4608), (2048, 2056, C_SMALL), (4616, 4620, C_SMALL + 8))
GATE_RUN = (6156, 10252)
SM_BETA = 0
SM_DECAY = 4
SM_FORGET = 8

NT_DIMS = (((1,), (1,)), ((), ()))


def _params(*sem):
    return pltpu.CompilerParams(dimension_semantics=sem, vmem_limit_bytes=VMEM_LIMIT_BYTES)


def _rms(x, w):
    return x * lax.rsqrt(jnp.mean(x * x, axis=-1, keepdims=True) + EPS) * w


def _l2norm(x):
    return x * lax.rsqrt(jnp.sum(x * x, axis=-1, keepdims=True) + EPS)


def _sigmoid(x):
    return 0.5 * jnp.tanh(0.5 * x) + 0.5


def _silu(x):
    return x * _sigmoid(x)


def _softplus(x):
    return jnp.maximum(x, 0.0) + jnp.log1p(jnp.exp(-jnp.abs(x)))


def _log_sigmoid(x):
    return -_softplus(-x)


def _dot(a, b):
    return jnp.dot(a.astype(BF16), b.astype(BF16), preferred_element_type=F32)


def _dot_nt(a, b):
    return lax.dot_general(a.astype(BF16), b.astype(BF16), NT_DIMS, preferred_element_type=F32)


def _split3(x):
    hi = x.astype(BF16)
    r1 = x - hi.astype(F32)
    mid = r1.astype(BF16)
    lo = (r1 - mid.astype(F32)).astype(BF16)
    return hi, mid, lo


def _dot_select(sel, x):
    return sum(jnp.dot(sel, piece, preferred_element_type=F32) for piece in _split3(x))


def _iota2(shape, axis):
    return lax.broadcasted_iota(jnp.int32, shape, axis)


def _head(h):
    return slice(h * HEAD_DIM, (h + 1) * HEAD_DIM)


def _norm_kernel(x_ref, nw_ref, h_ref):
    h_ref[...] = _rms(x_ref[...], nw_ref[...]).astype(BF16)


def _norm(x, norm_w):
    t, d = x.shape
    tm = min(t, 1024)
    return pl.pallas_call(
        _norm_kernel,
        out_shape=jax.ShapeDtypeStruct((t, d), BF16),
        grid=(t // tm,),
        in_specs=[pl.BlockSpec((tm, d), lambda i: (i, 0)), pl.BlockSpec((1, d), lambda i: (0, 0))],
        out_specs=pl.BlockSpec((tm, d), lambda i: (i, 0)),
        compiler_params=_params("parallel"),
        name="norm",
    )(x, norm_w)


def _inproj_kernel(h_ref, w_ref, b_ref, o_ref):
    o_ref[...] = jnp.dot(h_ref[...], w_ref[...], preferred_element_type=F32) + b_ref[...]


def _repack_kernel(w_ref, o_ref, g_ref):
    def run(start, stop):
        lo = start // LANES * LANES
        hi = min(-(-stop // LANES) * LANES, w_ref.shape[-1])
        return w_ref[:, lo:hi][:, start - lo:stop - lo].astype(BF16)

    o_ref[...] = jnp.zeros_like(o_ref)
    for start, stop, dst in COLUMN_RUNS:
        o_ref[:, dst:dst + stop - start] = run(start, stop)
    g_ref[...] = run(*GATE_RUN)


def _repack_w_in(w_in):
    depth, d, n_in = w_in.shape
    tr = 64
    return pl.pallas_call(
        _repack_kernel,
        out_shape=(jax.ShapeDtypeStruct((depth, d, N_PERM), BF16),
                   jax.ShapeDtypeStruct((depth, d, N_GATE), BF16)),
        grid=(depth, d // tr),
        in_specs=[pl.BlockSpec((None, tr, n_in), lambda l, i: (l, i, 0))],
        out_specs=(pl.BlockSpec((None, tr, N_PERM), lambda l, i: (l, i, 0)),
                   pl.BlockSpec((None, tr, N_GATE), lambda l, i: (l, i, 0))),
        compiler_params=_params("parallel", "parallel"),
        name="repack_w_in",
    )(w_in)


def _inproj(hn, w_all, b, layer):
    t, d = hn.shape
    n = w_all.shape[2]
    tm = min(t, 1024)
    tn = INPROJ_TN
    return pl.pallas_call(
        _inproj_kernel,
        out_shape=jax.ShapeDtypeStruct((t, n), F32),
        grid=(t // tm, n // tn),
        in_specs=[pl.BlockSpec((tm, d), lambda i, j: (i, 0)),
                  pl.BlockSpec((None, d, tn), lambda i, j: (layer, 0, j)),
                  pl.BlockSpec((1, tn), lambda i, j: (0, j))],
        out_specs=pl.BlockSpec((tm, tn), lambda i, j: (i, j)),
        compiler_params=_params("parallel", "parallel"),
        name="inproj",
    )(hn, w_all, b)


def _fox_prep_kernel(h_ref, wq_ref, bq_ref, wg_ref, bg_ref, ws_ref, bs_ref, qw_ref, kw_ref,
                     qn_ref, kn_ref, vo_ref, kb_ref, vb_ref, gate_ref, lf_ref, c_ref, ct_ref, carry_ref, *, tm):
    @pl.when(pl.program_id(1) == 0)
    def _():
        carry_ref[...] = jnp.zeros_like(carry_ref)

    hn = h_ref[...]
    qkv = jnp.dot(hn, wq_ref[...], preferred_element_type=F32) + bq_ref[...]
    gate_ref[...] = _silu(jnp.dot(hn, wg_ref[...], preferred_element_type=F32) + bg_ref[...])
    sm = jnp.dot(hn, ws_ref[...], preferred_element_type=F32) + bs_ref[...]
    scale = HEAD_DIM ** -0.5
    for h in range(N_HEADS):
        qn_ref[:, _head(h)] = (_rms(qkv[:, _head(h)], qw_ref[...]) * scale).astype(qn_ref.dtype)
        kn = _rms(qkv[:, BR_WIDTH + h * HEAD_DIM:BR_WIDTH + (h + 1) * HEAD_DIM], kw_ref[...])
        v = qkv[:, 2 * BR_WIDTH + h * HEAD_DIM:2 * BR_WIDTH + (h + 1) * HEAD_DIM]
        kn_ref[pl.ds(h, tm, stride=N_HEADS), :] = kn
        vo_ref[pl.ds(h, tm, stride=N_HEADS), :] = v
        kb_ref[:, _head(h)] = kn.astype(BF16)
    vb_ref[...] = qkv[:, 2 * BR_WIDTH:].astype(BF16)
    lf = _log_sigmoid(sm)
    lf_ref[...] = lf[:, SM_FORGET:SM_FORGET + N_HEADS]
    tri = (_iota2((tm, tm), 0) >= _iota2((tm, tm), 1)).astype(BF16)
    c = _dot_select(tri, lf) + carry_ref[...]
    c_ref[...] = c
    ct_ref[...] = c.T
    carry_ref[...] = c[tm - 1:tm, :]


def _fox_prep(hn, w_p, b_p, layer, qn_w, kn_w, nb, s, q_dtype):
    t, d = hn.shape
    tm = min(s, 512)
    nj = s // tm
    row = lambda b, j: b * nj + j
    qkv_w = 3 * BR_WIDTH

    def cols(width, off):
        return (pl.BlockSpec((None, d, width), lambda b, j: (layer, 0, off // width)),
                pl.BlockSpec((1, width), lambda b, j: (0, off // width)))

    out_rows = pl.BlockSpec((tm, BR_WIDTH), lambda b, j: (row(b, j), 0))
    head_rows = pl.BlockSpec((tm * N_HEADS, HEAD_DIM), lambda b, j: (row(b, j), 0))
    wide_out = lambda dt: jax.ShapeDtypeStruct((t, BR_WIDTH), dt)
    head_out = jax.ShapeDtypeStruct((t * N_HEADS, HEAD_DIM), F32)
    return pl.pallas_call(
        functools.partial(_fox_prep_kernel, tm=tm),
        out_shape=(wide_out(q_dtype), head_out, head_out, wide_out(BF16), wide_out(BF16), wide_out(F32),
                   jax.ShapeDtypeStruct((t, N_HEADS), F32),
                   jax.ShapeDtypeStruct((t, LANES), F32),
                   jax.ShapeDtypeStruct((nb, LANES, s), F32)),
        grid=(nb, nj),
        in_specs=[pl.BlockSpec((tm, d), lambda b, j: (row(b, j), 0)),
                  *cols(qkv_w, C_FQ), *cols(BR_WIDTH, C_FG), *cols(LANES, C_SMALL),
                  pl.BlockSpec((1, HEAD_DIM), lambda b, j: (0, 0)),
                  pl.BlockSpec((1, HEAD_DIM), lambda b, j: (0, 0))],
        out_specs=(out_rows, head_rows, head_rows, out_rows, out_rows, out_rows,
                   pl.BlockSpec((tm, N_HEADS), lambda b, j: (row(b, j), 0)),
                   pl.BlockSpec((tm, LANES), lambda b, j: (row(b, j), 0)),
                   pl.BlockSpec((None, LANES, tm), lambda b, j: (b, 0, j))),
        scratch_shapes=[pltpu.VMEM((1, LANES), F32)],
        compiler_params=_params("parallel", "arbitrary"),
        name="fox_prep",
    )(hn, w_p, b_p, w_p, b_p, w_p, b_p, qn_w, kn_w)


def _fox_flash_kernel(q_ref, k_ref, v_ref, cq_ref, ck_ref, g_ref, o_ref, m_ref, acc_ref, cqb_ref, *, tq):
    i = pl.program_id(1)
    j = pl.program_id(2)
    lane_tiles = tq // LANES

    @pl.when(j == 0)
    def _():
        m_ref[...] = jnp.full_like(m_ref, NEG_BIG)
        acc_ref[...] = jnp.zeros_like(acc_ref)
        for h in range(N_HEADS):
            cqb_ref[h] = jnp.broadcast_to(cq_ref[:, SM_FORGET + h:SM_FORGET + h + 1], (tq, LANES))

    def update(diagonal):
        ones = jnp.ones((tq, HEAD_DIM), BF16)
        for h in range(N_HEADS):
            s = _dot_nt(q_ref[:, _head(h)], k_ref[:, _head(h)]) - ck_ref[h:h + 1, :]
            if diagonal:
                s = jnp.where(_iota2((tq, tq), 0) >= _iota2((tq, tq), 1), s, NEG_BIG)
            cqb = cqb_ref[h]
            m_prev = m_ref[h]
            m_new = jnp.maximum(m_prev, jnp.max(s, axis=-1, keepdims=True) + cqb)
            shift = m_new - cqb
            p = jnp.exp((s - jnp.concatenate([shift] * lane_tiles, axis=1)).astype(BF16))
            alpha = jnp.exp(m_prev - m_new)
            v_ones = jnp.concatenate([v_ref[:, _head(h)], ones], axis=1)
            acc_ref[h] = (jnp.concatenate([alpha, alpha], axis=1) * acc_ref[h]
                          + jnp.dot(p, v_ones, preferred_element_type=F32))
            m_ref[h] = m_new

    @pl.when(j < i)
    def _():
        update(False)

    @pl.when(j == i)
    def _():
        update(True)
        for h in range(N_HEADS):
            acc = acc_ref[h]
            o = acc[:, :HEAD_DIM] / acc[:, HEAD_DIM:]
            o_ref[:, _head(h)] = (o * g_ref[:, _head(h)]).astype(BF16)


def _fox_flash(qn, kn, v, c, ct, gate, nb, s):
    t = qn.shape[0]
    tq = min(s, 512)
    nq = s // tq
    qrow = lambda b, i, j: (b * nq + i, 0)
    krow = lambda b, i, j: (b * nq + jnp.minimum(i, j), 0)
    stat = pltpu.VMEM((N_HEADS, tq, LANES), F32)
    return pl.pallas_call(
        functools.partial(_fox_flash_kernel, tq=tq),
        out_shape=jax.ShapeDtypeStruct((t, BR_WIDTH), BF16),
        grid=(nb, nq, nq),
        in_specs=[pl.BlockSpec((tq, BR_WIDTH), qrow),
                  pl.BlockSpec((tq, BR_WIDTH), krow),
                  pl.BlockSpec((tq, BR_WIDTH), krow),
                  pl.BlockSpec((tq, LANES), qrow),
                  pl.BlockSpec((None, SUBLANES, tq), lambda b, i, j: (b, SM_FORGET // SUBLANES, jnp.minimum(i, j))),
                  pl.BlockSpec((tq, BR_WIDTH), qrow)],
        out_specs=pl.BlockSpec((tq, BR_WIDTH), qrow),
        scratch_shapes=[stat, pltpu.VMEM((N_HEADS, tq, 2 * HEAD_DIM), F32), stat],
        compiler_params=_params("parallel", "parallel", "arbitrary"),
        name="fox_flash",
    )(qn, kn, v, c, ct, gate)


def _mem_kv_kernel(m_ref, nw_ref, w_ref, kw_ref, mk_ref, mv_ref):
    kv = _dot(_rms(m_ref[...], nw_ref[...]), w_ref[...])
    for h in range(N_HEADS):
        mk_ref[:, _head(h)] = _rms(kv[:, _head(h)], kw_ref[...])
    mv_ref[...] = kv[:, BR_WIDTH:]


def _mem_kv(mem, norm_w, w_kv, kn_w):
    t, d = mem.shape
    tm = min(t, 256)
    return pl.pallas_call(
        _mem_kv_kernel,
        out_shape=(jax.ShapeDtypeStruct((t, BR_WIDTH), F32), jax.ShapeDtypeStruct((t, BR_WIDTH), F32)),
        grid=(t // tm,),
        in_specs=[pl.BlockSpec((tm, d), lambda i: (i, 0)),
                  pl.BlockSpec((1, d), lambda i: (0, 0)),
                  pl.BlockSpec((d, 2 * BR_WIDTH), lambda i: (0, 0)),
                  pl.BlockSpec((1, HEAD_DIM), lambda i: (0, 0))],
        out_specs=(pl.BlockSpec((tm, BR_WIDTH), lambda i: (i, 0)),
                   pl.BlockSpec((tm, BR_WIDTH), lambda i: (i, 0))),
        compiler_params=_params("parallel"),
        name="mem_kv",
    )(mem, norm_w, w_kv, kn_w)


def _mem_attn_kernel(h_ref, wp_ref, bp_ref, k_ref, v_ref, qw_ref, o_ref):
    proj = jnp.dot(h_ref[...], wp_ref[...], preferred_element_type=F32) + bp_ref[...]
    scale = HEAD_DIM ** -0.5
    for h in range(N_HEADS):
        q = _rms(proj[:, _head(h)], qw_ref[...]) * scale
        s = _dot_nt(q, k_ref[:, _head(h)])
        p = jnp.exp(s - jnp.max(s, axis=-1, keepdims=True))
        o = _dot(p, v_ref[:, _head(h)]) / jnp.sum(p, axis=-1, keepdims=True)
        gate = _silu(proj[:, BR_WIDTH + h * HEAD_DIM:BR_WIDTH + (h + 1) * HEAD_DIM])
        o_ref[:, _head(h)] = (o * gate).astype(BF16)


def _mem_attn(hn, w_p, b_p, layer, mk, mv, qn_w, nb, s):
    t, d = hn.shape
    pair_w = 2 * BR_WIDTH
    m = mk.shape[0] // nb
    tq = min(s, 512)
    nq = s // tq
    return pl.pallas_call(
        _mem_attn_kernel,
        out_shape=jax.ShapeDtypeStruct((t, BR_WIDTH), BF16),
        grid=(nb, nq),
        in_specs=[pl.BlockSpec((tq, d), lambda b, i: (b * nq + i, 0)),
                  pl.BlockSpec((None, d, pair_w), lambda b, i: (layer, 0, C_MQ // pair_w)),
                  pl.BlockSpec((1, pair_w), lambda b, i: (0, C_MQ // pair_w)),
                  pl.BlockSpec((m, BR_WIDTH), lambda b, i: (b, 0)),
                  pl.BlockSpec((m, BR_WIDTH), lambda b, i: (b, 0)),
                  pl.BlockSpec((1, HEAD_DIM), lambda b, i: (0, 0))],
        out_specs=pl.BlockSpec((tq, BR_WIDTH), lambda b, i: (b * nq + i, 0)),
        compiler_params=_params("parallel", "parallel"),
        name="mem_attn",
    )(hn, w_p, b_p, mk, mv, qn_w)


def _causal_conv_tile(x, halo, w_ref):
    rows8 = _iota2(halo.shape, 0)
    y = x * w_ref[CONV_W - 1:CONV_W, :]
    for k in range(1, CONV_W):
        xr = pltpu.roll(x, k, axis=0)
        hr = pltpu.roll(halo, k, axis=0)
        top = jnp.where(rows8 < k, hr, xr[:SUBLANES])
        xs = jnp.concatenate([top, xr[SUBLANES:]], axis=0)
        y = y + xs * w_ref[CONV_W - 1 - k:CONV_W - k, :]
    return y


def _unit_lower_inverses(mats, row, col):
    n = mats[0].shape[0]

    def off_block(shift):
        return (((row >> (shift + 1)) == (col >> (shift + 1)))
                & (((row >> shift) & 1) == 1) & (((col >> shift) & 1) == 0))

    eye = (row == col).astype(F32)
    first = off_block(0)
    ds = [eye - jnp.where(first, a, 0.0) for a in mats]
    shift = 1
    while (1 << shift) < n:
        mask = off_block(shift)
        ts = [_dot(jnp.where(mask, a, 0.0), d) for a, d in zip(mats, ds)]
        ds = [d - _dot(d, t) for d, t in zip(ds, ts)]
        shift += 1
    return ds


def _dn_kernel(h_ref, wm_ref, bm_ref, ws_ref, bs_ref, cw_ref, al_ref, dt_ref, ow_ref,
               o_ref, st_ref, tail_ref, stt_ref, halo_ref, *, tl):
    j = pl.program_id(1)

    @pl.when(j == 0)
    def _():
        stt_ref[...] = jnp.zeros_like(stt_ref)
        halo_ref[...] = jnp.zeros_like(halo_ref)

    hn = h_ref[...]
    proj = jnp.dot(hn, wm_ref[...], preferred_element_type=F32) + bm_ref[...]
    sm = jnp.dot(hn, ws_ref[...], preferred_element_type=F32) + bs_ref[...]
    x = proj[:, :3 * BR_WIDTH]
    z = proj[:, 3 * BR_WIDTH:]
    qkv = _silu(_causal_conv_tile(x, halo_ref[...], cw_ref))
    halo_ref[...] = x[tl - SUBLANES:, :]
    tail_ref[...] = x[tl - SUBLANES:, :]
    beta_all = _sigmoid(sm)
    la_all = -jnp.exp(al_ref[...]) * _softplus(sm + dt_ref[...])

    cc = DN_CHUNK
    row = _iota2((cc, cc), 0)
    col = _iota2((cc, cc), 1)
    incl = row >= col
    strict = row > col
    tri = incl.astype(BF16)
    kd = N_HEADS * HEAD_DIM

    systems = []
    for c in range(tl // cc):
        rows = slice(c * cc, (c + 1) * cc)
        g_all = _dot_select(tri, la_all[rows])
        gt_all = g_all.T
        eg_all = jnp.exp(g_all)
        for h in range(N_HEADS):
            q = _l2norm(qkv[rows, h * HEAD_DIM:(h + 1) * HEAD_DIM]) * (HEAD_DIM ** -0.5)
            k = _l2norm(qkv[rows, kd + h * HEAD_DIM:kd + (h + 1) * HEAD_DIM])
            v = qkv[rows, 2 * kd + h * HEAD_DIM:2 * kd + (h + 1) * HEAD_DIM]
            beta = beta_all[rows, SM_BETA + h:SM_BETA + h + 1]
            gc = g_all[:, SM_DECAY + h:SM_DECAY + h + 1]
            gr = gt_all[SM_DECAY + h:SM_DECAY + h + 1, :]
            eg = eg_all[:, SM_DECAY + h:SM_DECAY + h + 1]
            g_last = g_all[cc - 1:cc, SM_DECAY + h:SM_DECAY + h + 1]
            decay = jnp.exp(jnp.where(incl, gc - gr, NEG_BIG))
            kb = k.astype(BF16)
            qb = q.astype(BF16)
            systems.append(dict(
                a=jnp.where(strict, beta * decay * _dot_nt(kb, kb), 0.0),
                qk=(decay * _dot_nt(qb, kb)).astype(BF16),
                rhs=jnp.concatenate([beta * v, (beta * eg) * k], axis=1).astype(BF16),
                kdec_t=(jnp.exp(g_last - gc) * k).T.astype(BF16),
                q=qb, eg=eg, g_end=jnp.exp(g_last)))

    ds = _unit_lower_inverses([sy["a"] for sy in systems], row, col)
    sols = [_dot(d, sy["rhs"]) for d, sy in zip(ds, systems)]

    heads = range(N_HEADS)
    stts = [stt_ref[h] for h in heads]
    for c in range(tl // cc):
        rows = slice(c * cc, (c + 1) * cc)
        sys_c = systems[c * N_HEADS:(c + 1) * N_HEADS]
        sol_c = sols[c * N_HEADS:(c + 1) * N_HEADS]
        us = [sol_c[h][:, :HEAD_DIM] - _dot(sol_c[h][:, HEAD_DIM:], stts[h]) for h in heads]
        os = [sys_c[h]["eg"] * _dot(sys_c[h]["q"], stts[h]) + _dot(sys_c[h]["qk"], us[h]) for h in heads]
        stts = [sys_c[h]["g_end"] * stts[h] + _dot(sys_c[h]["kdec_t"], us[h]) for h in heads]
        for h in heads:
            o = _rms(os[h], ow_ref[...]) * _silu(z[rows, _head(h)])
            o_ref[rows, _head(h)] = o.astype(BF16)
    for h in heads:
        stt_ref[h] = stts[h]

    @pl.when(j == pl.num_programs(1) - 1)
    def _():
        for h in range(N_HEADS):
            st_ref[h] = stt_ref[h].T


def _dn_prompt(hn, w_p, b_p, layer, conv_w, al_row, dt_row, onorm_w, nb, s):
    t, d = hn.shape
    tl = min(s, 512)
    nj = s // tl
    qkv_w = 3 * BR_WIDTH
    main_w = qkv_w + BR_WIDTH
    row = lambda b, j: b * nj + j
    return pl.pallas_call(
        functools.partial(_dn_kernel, tl=tl),
        out_shape=(jax.ShapeDtypeStruct((t, BR_WIDTH), BF16),
                   jax.ShapeDtypeStruct((nb, N_HEADS, HEAD_DIM, HEAD_DIM), F32),
                   jax.ShapeDtypeStruct((nb, SUBLANES, qkv_w), F32)),
        grid=(nb, nj),
        in_specs=[pl.BlockSpec((tl, d), lambda b, j: (row(b, j), 0)),
                  pl.BlockSpec((None, d, main_w), lambda b, j: (layer, 0, C_DNQKV // main_w)),
                  pl.BlockSpec((1, main_w), lambda b, j: (0, C_DNQKV // main_w)),
                  pl.BlockSpec((None, d, LANES), lambda b, j: (layer, 0, C_SMALL // LANES)),
                  pl.BlockSpec((1, LANES), lambda b, j: (0, C_SMALL // LANES)),
                  pl.BlockSpec((CONV_W, qkv_w), lambda b, j: (0, 0)),
                  pl.BlockSpec((1, LANES), lambda b, j: (0, 0)),
                  pl.BlockSpec((1, LANES), lambda b, j: (0, 0)),
                  pl.BlockSpec((1, HEAD_DIM), lambda b, j: (0, 0))],
        out_specs=(pl.BlockSpec((tl, BR_WIDTH), lambda b, j: (row(b, j), 0)),
                   pl.BlockSpec((None, N_HEADS, HEAD_DIM, HEAD_DIM), lambda b, j: (b, 0, 0, 0)),
                   pl.BlockSpec((None, SUBLANES, qkv_w), lambda b, j: (b, 0, 0))),
        scratch_shapes=[pltpu.VMEM((N_HEADS, HEAD_DIM, HEAD_DIM), F32),
                        pltpu.VMEM((SUBLANES, qkv_w), F32)],
        compiler_params=_params("parallel", "arbitrary"),
        name="dn_prompt",
    )(hn, w_p, b_p, w_p, b_p, conv_w, al_row, dt_row, onorm_w)


def _lru_gates(xc, wr_ref, br_ref, wi_ref, bi_ref, lam_ref):
    r = _sigmoid(_dot(xc, wr_ref[...]) + br_ref[...])
    i = _sigmoid(_dot(xc, wi_ref[...]) + bi_ref[...])
    log_a = -LRU_C * r * _softplus(-lam_ref[...])
    a = jnp.exp(log_a)
    one_minus_a2 = -jnp.tanh(log_a) * (a * a + 1.0)
    return a, jnp.sqrt(one_minus_a2) * (i * xc)


def _lru_kernel(h_ref, wp_ref, bp_ref, cw_ref, cb_ref, wr_ref, br_ref, wi_ref, bi_ref, lam_ref,
                o_ref, hl_ref, tail_ref, hc_ref, halo_ref, *, tl):
    @pl.when(pl.program_id(1) == 0)
    def _():
        hc_ref[...] = jnp.zeros_like(hc_ref)
        halo_ref[...] = jnp.zeros_like(halo_ref)

    proj = jnp.dot(h_ref[...], wp_ref[...], preferred_element_type=F32) + bp_ref[...]
    x = proj[:, :BR_WIDTH]
    xc = _causal_conv_tile(x, halo_ref[...], cw_ref) + cb_ref[...]
    halo_ref[...] = x[tl - SUBLANES:, :]
    tail_ref[...] = x[tl - SUBLANES:, :]
    a, b = _lru_gates(xc, wr_ref, br_ref, wi_ref, bi_ref, lam_ref)
    rows = _iota2(a.shape, 0)
    d = 1
    while d < tl:
        a_up = jnp.where(rows >= d, pltpu.roll(a, d, axis=0), 1.0)
        b_up = jnp.where(rows >= d, pltpu.roll(b, d, axis=0), 0.0)
        b = a * b_up + b
        a = a * a_up
        d *= 2
    hseq = a * hc_ref[...] + b
    hc_ref[...] = hseq[tl - 1:tl, :]
    hl_ref[...] = hseq[tl - 1:tl, :]
    o_ref[...] = (hseq * _silu(proj[:, BR_WIDTH:])).astype(BF16)


def _lru_prompt(hn, w_p, b_p, layer, conv_w, conv_b, w_r, b_r, w_i, b_i, lam, nb, s):
    t, d = hn.shape
    pair_w = 2 * BR_WIDTH
    tl = min(s, 512)
    nj = s // tl
    row = lambda b, j: b * nj + j
    vec = pl.BlockSpec((1, BR_WIDTH), lambda b, j: (0, 0))
    mat = pl.BlockSpec((BR_WIDTH, BR_WIDTH), lambda b, j: (0, 0))
    return pl.pallas_call(
        functools.partial(_lru_kernel, tl=tl),
        out_shape=(jax.ShapeDtypeStruct((t, BR_WIDTH), BF16),
                   jax.ShapeDtypeStruct((nb, 1, BR_WIDTH), F32),
                   jax.ShapeDtypeStruct((nb, SUBLANES, BR_WIDTH), F32)),
        grid=(nb, nj),
        in_specs=[pl.BlockSpec((tl, d), lambda b, j: (row(b, j), 0)),
                  pl.BlockSpec((None, d, pair_w), lambda b, j: (layer, 0, C_LRUX // pair_w)),
                  pl.BlockSpec((1, pair_w), lambda b, j: (0, C_LRUX // pair_w)),
                  pl.BlockSpec((CONV_W, BR_WIDTH), lambda b, j: (0, 0)),
                  vec, mat, vec, mat, vec, vec],
        out_specs=(pl.BlockSpec((tl, BR_WIDTH), lambda b, j: (row(b, j), 0)),
                   pl.BlockSpec((None, 1, BR_WIDTH), lambda b, j: (b, 0, 0)),
                   pl.BlockSpec((None, SUBLANES, BR_WIDTH), lambda b, j: (b, 0, 0))),
        scratch_shapes=[pltpu.VMEM((1, BR_WIDTH), F32), pltpu.VMEM((SUBLANES, BR_WIDTH), F32)],
        compiler_params=_params("parallel", "arbitrary"),
        name="lru_prompt",
    )(hn, w_p, b_p, conv_w, conv_b, w_r, b_r, w_i, b_i, lam)


def _merge_kernel(x_ref, h_ref, a_ref, b_ref, c_ref, m_ref, wg_ref, bg_ref, wb_ref, wo_ref, y_ref):
    x = x_ref[...]
    d = x.shape[1]
    h = h_ref[...]
    merged = None
    for n, br in enumerate((a_ref, b_ref, c_ref, m_ref)):
        cols = slice(n * d, (n + 1) * d)
        gate = jnp.dot(h, wg_ref[:, cols], preferred_element_type=F32) + bg_ref[:, cols]
        up = jnp.dot(br[...], wb_ref[n], preferred_element_type=F32)
        term = _sigmoid(gate) * up
        merged = term if merged is None else merged + term
    y_ref[...] = x + jnp.dot(merged.astype(BF16), wo_ref[...], preferred_element_type=F32)


def _merge(x, hn, out_a, out_b, out_c, out_m, w_gate, b_gate, w_branch, w_out, layer):
    t, d = x.shape
    tm = min(t, 512)
    rows = lambda w: pl.BlockSpec((tm, w), lambda i: (i, 0))
    return pl.pallas_call(
        _merge_kernel,
        out_shape=jax.ShapeDtypeStruct((t, d), F32),
        grid=(t // tm,),
        in_specs=[rows(d), rows(d),
                  rows(BR_WIDTH), rows(BR_WIDTH), rows(BR_WIDTH), rows(BR_WIDTH),
                  pl.BlockSpec((None, d, N_GATE), lambda i: (layer, 0, 0)),
                  pl.BlockSpec((1, N_GATE), lambda i: (0, 0)),
                  pl.BlockSpec((N_BRANCH, BR_WIDTH, d), lambda i: (0, 0, 0)),
                  pl.BlockSpec((d, d), lambda i: (0, 0))],
        out_specs=rows(d),
        compiler_params=_params("parallel"),
        name="merge",
    )(x, hn, out_a, out_b, out_c, out_m, w_gate, b_gate, w_branch, w_out)


def _sample_step_kernel(qkv_ref, z_ref, lx_ref, lg_ref, sm_ref, st_ref, dbuf_ref, lh_ref, lbuf_ref,
                        dcw_ref, al_ref, dt_ref, ow_ref, lcw_ref, lcb_ref, wr_ref, br_ref, wi_ref, bi_ref, lam_ref,
                        oa_ref, ob_ref, sto_ref, dbo_ref, lho_ref, lbo_ref, o_scr, *, bs):
    kd = N_HEADS * HEAD_DIM

    def step_conv(x, buf_ref, bufo_ref, w_ref):
        y = x * w_ref[CONV_W - 1:CONV_W, :]
        for k in range(CONV_W - 1):
            y = y + buf_ref[:, k, :] * w_ref[k:k + 1, :]
        for k in range(CONV_W - 2):
            bufo_ref[:, k, :] = buf_ref[:, k + 1, :]
        bufo_ref[:, CONV_W - 2, :] = x
        return y

    qkv = _silu(step_conv(qkv_ref[...], dbuf_ref, dbo_ref, dcw_ref))
    sm = sm_ref[...]
    beta_all = _sigmoid(sm)
    eg_all = jnp.exp(-jnp.exp(al_ref[...]) * _softplus(sm + dt_ref[...]))
    eye = _iota2((HEAD_DIM, HEAD_DIM), 0) == _iota2((HEAD_DIM, HEAD_DIM), 1)
    pad = jnp.zeros((SUBLANES - 2, HEAD_DIM), F32)
    items = []
    for h in range(N_HEADS):
        q = _l2norm(qkv[:, h * HEAD_DIM:(h + 1) * HEAD_DIM]) * (HEAD_DIM ** -0.5)
        k = _l2norm(qkv[:, kd + h * HEAD_DIM:kd + (h + 1) * HEAD_DIM])
        v = qkv[:, 2 * kd + h * HEAD_DIM:2 * kd + (h + 1) * HEAD_DIM]
        beta = beta_all[:, SM_BETA + h:SM_BETA + h + 1]
        eg = eg_all[:, SM_DECAY + h:SM_DECAY + h + 1]
        qk = jnp.sum(q * k, axis=-1, keepdims=True)
        for b in range(bs):
            one = slice(b, b + 1)
            items.append(dict(b=b, h=h, q=q[one], k=k[one], v=v[one], beta=beta[one], eg=eg[one], qk=qk[one]))
    sk_sqs = [_dot_nt(jnp.concatenate([it["k"], it["q"], pad], axis=0), st_ref[it["b"], it["h"]])
              for it in items]
    us = [it["beta"] * (it["v"] - it["eg"] * r[0:1]) for it, r in zip(items, sk_sqs)]
    for it, r, u in zip(items, sk_sqs, us):
        o_scr[it["b"]:it["b"] + 1, _head(it["h"])] = it["eg"] * r[1:2] + it["qk"] * u
    updates = [_dot(jnp.where(eye, jnp.broadcast_to(u, (HEAD_DIM, HEAD_DIM)), 0.0),
                    jnp.broadcast_to(it["k"], (HEAD_DIM, HEAD_DIM))) for it, u in zip(items, us)]
    for it, upd in zip(items, updates):
        sto_ref[it["b"], it["h"]] = it["eg"] * st_ref[it["b"], it["h"]] + upd
    o = o_scr[...]
    for h in range(N_HEADS):
        oa_ref[:, _head(h)] = (_rms(o[:, _head(h)], ow_ref[...]) * _silu(z_ref[:, _head(h)])).astype(BF16)

    xc = step_conv(lx_ref[...], lbuf_ref, lbo_ref, lcw_ref) + lcb_ref[...]
    a, bx = _lru_gates(xc, wr_ref, br_ref, wi_ref, bi_ref, lam_ref)
    hnew = a * lh_ref[...] + bx
    lho_ref[...] = hnew
    ob_ref[...] = (hnew * _silu(lg_ref[...])).astype(BF16)


def _sample_step(proj, layer, state_dn, state_dn_conv, state_lru_h, state_lru_conv,
                 dn_conv_w, al_row, dt_row, onorm_w, lru_conv_w, lru_conv_b, w_r, b_r, w_i, b_i, lam):
    nb = proj.shape[0]
    bs = SUBLANES
    qkv_w = 3 * BR_WIDTH
    wide = lambda off: pl.BlockSpec((bs, BR_WIDTH), lambda i: (i, off // BR_WIDTH))
    const = lambda shape: pl.BlockSpec(shape, lambda i: (0,) * len(shape))
    st_in = pl.BlockSpec((None, bs, N_HEADS, HEAD_DIM, HEAD_DIM), lambda i: (layer, i, 0, 0, 0))
    dbuf_in = pl.BlockSpec((None, bs, CONV_W - 1, qkv_w), lambda i: (layer, i, 0, 0))
    lh_in = pl.BlockSpec((None, bs, BR_WIDTH), lambda i: (layer, i, 0))
    lbuf_in = pl.BlockSpec((None, bs, CONV_W - 1, BR_WIDTH), lambda i: (layer, i, 0, 0))
    return pl.pallas_call(
        functools.partial(_sample_step_kernel, bs=bs),
        out_shape=(jax.ShapeDtypeStruct((nb, BR_WIDTH), BF16),
                   jax.ShapeDtypeStruct((nb, BR_WIDTH), BF16),
                   jax.ShapeDtypeStruct((nb, N_HEADS, HEAD_DIM, HEAD_DIM), F32),
                   jax.ShapeDtypeStruct((nb, CONV_W - 1, qkv_w), F32),
                   jax.ShapeDtypeStruct((nb, BR_WIDTH), F32),
                   jax.ShapeDtypeStruct((nb, CONV_W - 1, BR_WIDTH), F32)),
        grid=(nb // bs,),
        in_specs=[pl.BlockSpec((bs, qkv_w), lambda i: (i, 0)), wide(C_DNZ), wide(C_LRUX), wide(C_LRUG),
                  pl.BlockSpec((bs, LANES), lambda i: (i, C_SMALL // LANES)),
                  st_in, dbuf_in, lh_in, lbuf_in,
                  const((CONV_W, qkv_w)), const((1, LANES)), const((1, LANES)), const((1, HEAD_DIM)),
                  const((CONV_W, BR_WIDTH)), const((1, BR_WIDTH)),
                  const((BR_WIDTH, BR_WIDTH)), const((1, BR_WIDTH)),
                  const((BR_WIDTH, BR_WIDTH)), const((1, BR_WIDTH)), const((1, BR_WIDTH))],
        out_specs=(pl.BlockSpec((bs, BR_WIDTH), lambda i: (i, 0)),
                   pl.BlockSpec((bs, BR_WIDTH), lambda i: (i, 0)),
                   pl.BlockSpec((bs, N_HEADS, HEAD_DIM, HEAD_DIM), lambda i: (i, 0, 0, 0)),
                   pl.BlockSpec((bs, CONV_W - 1, qkv_w), lambda i: (i, 0, 0)),
                   pl.BlockSpec((bs, BR_WIDTH), lambda i: (i, 0)),
                   pl.BlockSpec((bs, CONV_W - 1, BR_WIDTH), lambda i: (i, 0, 0))),
        scratch_shapes=[pltpu.VMEM((bs, BR_WIDTH), F32)],
        compiler_params=_params("parallel"),
        name="sample_step",
    )(proj, proj, proj, proj, proj, state_dn, state_dn_conv, state_lru_h, state_lru_conv,
      dn_conv_w, al_row, dt_row, onorm_w, lru_conv_w, lru_conv_b, w_r, b_r, w_i, b_i, lam)


def _own_head_mask(n_rows_kv):
    shape = (SUBLANES, n_rows_kv)
    return (_iota2(shape, 1) % N_HEADS) == _iota2(shape, 0)


def _heads_to_row(o8):
    return jnp.concatenate([o8[h:h + 1] for h in range(N_HEADS)], axis=1)


def _pool_bias_kernel(lf_ref, after_ref, same_ref, bias_ref):
    in_page = None
    total = None
    for h in range(N_HEADS):
        for piece in _split3(lf_ref[:, h, :]):
            a = jnp.dot(piece, after_ref[h], preferred_element_type=F32)
            t = jnp.dot(piece, same_ref[h], preferred_element_type=F32)
            in_page = a if in_page is None else in_page + a
            total = t if total is None else total + t
    bias_ref[:, 0, :] = in_page
    bias_ref[:, 1, :] = total


def _pool_bias(cache_lf):
    n, nh, page = cache_lf.shape
    w = page * nh
    tp = min(n, 512)
    src_tok = jnp.arange(page)[None, :, None]
    src_head = jnp.arange(nh)[:, None, None]
    dst = jnp.arange(w)[None, None, :]
    same = (dst % nh) == src_head
    after = same & (src_tok > dst // nh)
    const = pl.BlockSpec((nh, page, w), lambda i: (0, 0, 0))
    return pl.pallas_call(
        _pool_bias_kernel,
        out_shape=jax.ShapeDtypeStruct((n, 2, w), F32),
        grid=(n // tp,),
        in_specs=[pl.BlockSpec((tp, nh, page), lambda i: (i, 0, 0)), const, const],
        out_specs=pl.BlockSpec((tp, 2, w), lambda i: (i, 0, 0)),
        compiler_params=_params("parallel"),
        name="pool_bias",
    )(cache_lf, after.astype(BF16), jnp.broadcast_to(same, after.shape).astype(BF16))


def _fox_decode_kernel(pt_ref, q_ref, kn_ref, vn_ref, lfn_ref, g_ref, *rest, n_pages, n_seq):
    n_blk = n_seq * n_pages
    k_refs, v_refs, bias_refs = rest[:n_blk], rest[n_blk:2 * n_blk], rest[2 * n_blk:3 * n_blk]
    o_ref = rest[3 * n_blk]
    del pt_ref
    own = _own_head_mask(k_refs[0].shape[0])
    for i in range(n_seq):
        pages = range(i * n_pages, (i + 1) * n_pages)
        q8 = q_ref[i]
        qb = q8.astype(BF16)

        later = lfn_ref[i]
        scores = {}
        for p in reversed(pages):
            s = _dot_nt(qb, k_refs[p][...]) + (bias_refs[p][0:1, :] + later)
            scores[p] = jnp.where(own, s, NEG_BIG)
            later = later + bias_refs[p][1:2, :]
        s_self = jnp.sum(q8 * kn_ref[i], axis=-1, keepdims=True)
        m = s_self
        for p in pages:
            m = jnp.maximum(m, jnp.max(scores[p], axis=-1, keepdims=True))
        p_self = jnp.exp(s_self - m)
        l = p_self
        acc = p_self * vn_ref[i]
        for p in pages:
            pr = jnp.exp(scores[p] - m)
            l = l + jnp.sum(pr, axis=-1, keepdims=True)
            acc = acc + _dot(pr, v_refs[p][...])
        o_ref[i] = (_heads_to_row(acc / l) * g_ref[i]).astype(BF16)


def _fox_decode(q8, k8, v8, lf_new, gate, cache_k, cache_v, page_bias, page_table, layer, n_pool):
    nb = q8.shape[0]
    n_pages = page_table.shape[0] // nb
    n_seq = DECODE_SEQS if nb % DECODE_SEQS == 0 else 1
    kv_rows = cache_k.shape[1]
    base = layer * n_pool
    row = pl.BlockSpec((n_seq, 1, BR_WIDTH), lambda b, pt: (b, 0, 0))
    heads = pl.BlockSpec((n_seq, SUBLANES, HEAD_DIM), lambda b, pt: (b, 0, 0))

    def paged(shape, i, p):
        return pl.BlockSpec((None,) + shape, lambda b, pt: (base + pt[(b * n_seq + i) * n_pages + p], 0, 0))

    blocks = [(i, p) for i in range(n_seq) for p in range(n_pages)]
    in_specs = [heads, heads, heads, pl.BlockSpec((n_seq, 1, kv_rows), lambda b, pt: (b, 0, 0)), row]
    in_specs += [paged((kv_rows, HEAD_DIM), i, p) for i, p in blocks]
    in_specs += [paged((kv_rows, HEAD_DIM), i, p) for i, p in blocks]
    in_specs += [paged((2, kv_rows), i, p) for i, p in blocks]
    grid_spec = pltpu.PrefetchScalarGridSpec(
        num_scalar_prefetch=1,
        grid=(nb // n_seq,),
        in_specs=in_specs,
        out_specs=row)
    n_blk = len(blocks)
    return pl.pallas_call(
        functools.partial(_fox_decode_kernel, n_pages=n_pages, n_seq=n_seq),
        out_shape=jax.ShapeDtypeStruct((nb, 1, BR_WIDTH), BF16),
        grid_spec=grid_spec,
        compiler_params=_params("parallel"),
        name="fox_decode",
    )(page_table, q8, k8, v8, lf_new, gate, *([cache_k] * n_blk), *([cache_v] * n_blk), *([page_bias] * n_blk))


def _mem_decode_kernel(q_ref, g_ref, k_ref, v_ref, qw_ref, o_ref, o_scr, *, bs):
    own = _own_head_mask(k_ref.shape[1])
    scale = HEAD_DIM ** -0.5
    qs = [_rms(q_ref[:, _head(h)], qw_ref[...]) * scale for h in range(N_HEADS)]
    pad = jnp.zeros((SUBLANES - N_HEADS, HEAD_DIM), F32)
    for b in range(bs):
        q8 = jnp.concatenate([q[b:b + 1] for q in qs] + [pad], axis=0)
        s = jnp.where(own, _dot_nt(q8, k_ref[b]), NEG_BIG)
        p = jnp.exp(s - jnp.max(s, axis=-1, keepdims=True))
        o = _dot(p, v_ref[b]) / jnp.sum(p, axis=-1, keepdims=True)
        o_scr[b:b + 1, :] = _heads_to_row(o)
    o_ref[...] = (o_scr[...] * _silu(g_ref[...])).astype(BF16)


def _mem_decode(proj, cache_k, cache_v, qn_w, layer):
    nb = proj.shape[0]
    bs = SUBLANES
    m = cache_k.shape[1]
    base = layer * (nb // bs)
    kv = pl.BlockSpec((bs, m, HEAD_DIM), lambda i: (base + i, 0, 0))
    return pl.pallas_call(
        functools.partial(_mem_decode_kernel, bs=bs),
        out_shape=jax.ShapeDtypeStruct((nb, BR_WIDTH), BF16),
        grid=(nb // bs,),
        in_specs=[pl.BlockSpec((bs, BR_WIDTH), lambda i: (i, C_MQ // BR_WIDTH)),
                  pl.BlockSpec((bs, BR_WIDTH), lambda i: (i, C_MG // BR_WIDTH)),
                  kv, kv, pl.BlockSpec((1, HEAD_DIM), lambda i: (0, 0))],
        out_specs=pl.BlockSpec((bs, BR_WIDTH), lambda i: (i, 0)),
        scratch_shapes=[pltpu.VMEM((bs, BR_WIDTH), F32)],
        compiler_params=_params("parallel"),
        name="mem_decode",
    )(proj, proj, cache_k, cache_v, qn_w)


def _permute_columns(a):
    runs = sorted(COLUMN_RUNS, key=lambda r: r[2])
    pad = jnp.zeros(a.shape[:-1] + (N_PERM - C_SMALL - N_SMALL,), a.dtype)
    return jnp.concatenate([a[..., start:stop] for start, stop, _ in runs] + [pad], axis=-1)


def _block_diag(w):
    nblk, e, f = w.shape
    eye = jnp.eye(nblk, dtype=w.dtype)
    return (eye[:, None, :, None] * w[:, :, None, :]).reshape(nblk * e, nblk * f)


def _decay_lane_row(v):
    return jnp.zeros((1, LANES), F32).at[0, SM_DECAY:SM_DECAY + N_HEADS].set(v)


def kernel(x_prompt, x_sample, cache_fox_k, cache_fox_v, cache_fox_logf, cache_mem_k, cache_mem_v, state_dn, state_dn_conv, state_lru_h, state_lru_conv, page_table, mem_prompt, norm_w, w_in, b_in, dn_conv_w, dn_A_log, dn_dt_bias, dn_onorm_w, lru_conv_w, lru_conv_b, lru_w_r, lru_b_r, lru_w_i, lru_b_i, lru_lambda, fox_qn_w, fox_kn_w, mem_norm_w, w_mem_kv, mem_qn_w, mem_kn_w, w_branch, w_out):
    bp, s, d = x_prompt.shape
    bd = x_sample.shape[0]
    depth = w_in.shape[0]
    n_pool, page = cache_fox_k.shape[1], cache_fox_k.shape[2]
    mem_tokens = mem_prompt.shape[1]
    tp = bp * s

    yp = x_prompt.reshape(tp, d)
    ys = x_sample.reshape(bd, d)
    mem2 = mem_prompt.reshape(bp * mem_tokens, d)
    kv_rows = page * N_HEADS
    cache_k2 = cache_fox_k.reshape(depth * n_pool, kv_rows, HEAD_DIM)
    cache_v2 = cache_fox_v.reshape(depth * n_pool, kv_rows, HEAD_DIM)
    page_bias = _pool_bias(jnp.swapaxes(cache_fox_logf, 2, 3).reshape(depth * n_pool, N_HEADS, page))
    cmem_k2 = cache_mem_k.reshape(depth * bd, mem_tokens * N_HEADS, HEAD_DIM)
    cmem_v2 = cache_mem_v.reshape(depth * bd, mem_tokens * N_HEADS, HEAD_DIM)
    pt_flat = page_table.reshape(-1)

    row = lambda v: v.reshape(1, -1)
    acc = {n: [] for n in ("pk", "pv", "plf", "pmk", "pmv", "pdn", "pdc", "plh", "plc",
                           "sk", "sv", "slf", "sdn", "sdc", "slh", "slc")}
    w_p, w_g = _repack_w_in(w_in)
    for l in range(depth):
        b_p = _permute_columns(b_in[l]).reshape(1, N_PERM)
        b_g = b_in[l, GATE_RUN[0]:GATE_RUN[1]].reshape(1, N_GATE)
        nw = row(norm_w[l])
        al_row = _decay_lane_row(dn_A_log[l])
        dt_row = _decay_lane_row(dn_dt_bias[l])
        ow = row(dn_onorm_w[l])
        wr = _block_diag(lru_w_r[l]).astype(BF16)
        wi = _block_diag(lru_w_i[l]).astype(BF16)
        lru_args = (lru_conv_w[l], row(lru_conv_b[l]), wr, row(lru_b_r[l]), wi, row(lru_b_i[l]), row(lru_lambda[l]))
        wb = w_branch[l].astype(BF16)
        wo = w_out[l].astype(BF16)
        fqw, fkw, mqw = row(fox_qn_w[l]), row(fox_kn_w[l]), row(mem_qn_w[l])

        hp = _norm(yp, nw)
        qb, kn, fv, kb, vb, gate_c, lf, c, ct = _fox_prep(hp, w_p, b_p, l, fqw, fkw, bp, s, BF16)
        out_c = _fox_flash(qb, kb, vb, c, ct, gate_c, bp, s)
        mk, mv = _mem_kv(mem2, row(mem_norm_w[l]), w_mem_kv[l].astype(BF16), row(mem_kn_w[l]))
        out_m = _mem_attn(hp, w_p, b_p, l, mk, mv, mqw, bp, s)
        out_a, dn_s, dn_tail = _dn_prompt(hp, w_p, b_p, l, dn_conv_w[l], al_row, dt_row, ow, bp, s)
        out_b, lru_h, lru_tail = _lru_prompt(hp, w_p, b_p, l, *lru_args, bp, s)
        acc["pk"].append(kn.reshape(bp, s, N_HEADS, HEAD_DIM))
        acc["pv"].append(fv.reshape(bp, s, N_HEADS, HEAD_DIM))
        acc["plf"].append(lf.reshape(bp, s, N_HEADS))
        acc["pmk"].append(mk.reshape(bp, mem_tokens, N_HEADS, HEAD_DIM))
        acc["pmv"].append(mv.reshape(bp, mem_tokens, N_HEADS, HEAD_DIM))
        acc["pdn"].append(dn_s)
        acc["pdc"].append(dn_tail[:, SUBLANES - (CONV_W - 1):, :])
        acc["plh"].append(lru_h.reshape(bp, BR_WIDTH))
        acc["plc"].append(lru_tail[:, SUBLANES - (CONV_W - 1):, :])
        yp = _merge(yp, hp, out_a, out_b, out_c, out_m, w_g, b_g, wb, wo, l)

        hs = _norm(ys, nw)
        proj_s = _inproj(hs, w_p, b_p, l)
        qn_s, kn_s, fv_s, _, _, gate_s, lf_s, _, _ = _fox_prep(hs, w_p, b_p, l, fqw, fkw, 1, bd, F32)
        head_rows = lambda a: jnp.pad(a.reshape(bd, N_HEADS, HEAD_DIM), ((0, 0), (0, SUBLANES - N_HEADS), (0, 0)))
        lf_new = jnp.tile(lf_s, (1, page)).reshape(bd, 1, kv_rows)
        out_c_s = _fox_decode(head_rows(qn_s), head_rows(kn_s), head_rows(fv_s), lf_new,
                              gate_s.reshape(bd, 1, BR_WIDTH),
                              cache_k2, cache_v2, page_bias, pt_flat, l, n_pool).reshape(bd, BR_WIDTH)
        out_m_s = _mem_decode(proj_s, cmem_k2, cmem_v2, mqw, l)
        out_a_s, out_b_s, dn_s_s, dn_c_s, lru_h_s, lru_c_s = _sample_step(
            proj_s, l, state_dn, state_dn_conv, state_lru_h, state_lru_conv,
            dn_conv_w[l], al_row, dt_row, ow, *lru_args)
        acc["sk"].append(kn_s.reshape(bd, 1, N_HEADS, HEAD_DIM))
        acc["sv"].append(fv_s.reshape(bd, 1, N_HEADS, HEAD_DIM))
        acc["slf"].append(lf_s.reshape(bd, 1, N_HEADS))
        acc["sdn"].append(dn_s_s)
        acc["sdc"].append(dn_c_s)
        acc["slh"].append(lru_h_s)
        acc["slc"].append(lru_c_s)
        ys = _merge(ys, hs, out_a_s, out_b_s, out_c_s, out_m_s, w_g, b_g, wb, wo, l)

    st = lambda n: jnp.stack(acc[n])
    return (yp.reshape(bp, s, d), ys.reshape(bd, 1, d),
            st("pk"), st("pv"), st("plf"), st("pmk"), st("pmv"), st("pdn"), st("pdc"), st("plh"), st("plc"),
            st("sk"), st("sv"), st("slf"), st("sdn"), st("sdc"), st("slh"), st("slc"))
```

```python
import functools

import jax
import jax.numpy as jnp
from jax import lax
from jax.experimental import pallas as pl
from jax.experimental.pallas import tpu as pltpu

F32 = jnp.float32
BF16 = jnp.bfloat16

EPS = 1e-6
LRU_C = 8.0
CONV_W = 4
N_HEADS = 4
HEAD_DIM = 128
BR_WIDTH = N_HEADS * HEAD_DIM
N_BRANCH = 4
LANES = 128
SUBLANES = 8
DN_CHUNK = 128
DECODE_SEQS = 2
NEG_BIG = -1e30
VMEM_LIMIT_BYTES = 56 * 1024 * 1024

C_DNQKV = 0
C_DNZ = 1536
C_LRUX = 2048
C_LRUG = 2560
C_FQ = 3072
C_FK = 3584
C_FV = 4096
C_FG = 4608
C_MQ = 5120
C_MG = 5632
C_SMALL = 6144
N_PERM = 6400
INPROJ_TN = 1280
N_SMALL = 3 * N_HEADS
N_GATE = 4096
COLUMN_RUNS = ((0, 2048, 0), (2056, 4616, 2048), (4620, 6156, 4608), (2048, 2056, C_SMALL), (4616, 4620, C_SMALL + 8))
GATE_RUN = (6156, 10252)
SM_BETA = 0
SM_DECAY = 4
SM_FORGET = 8

NT_DIMS = (((1,), (1,)), ((), ()))


def _params(*sem):
    return pltpu.CompilerParams(dimension_semantics=sem, vmem_limit_bytes=VMEM_LIMIT_BYTES)


def _rms(x, w):
    return x * lax.rsqrt(jnp.mean(x * x, axis=-1, keepdims=True) + EPS) * w


def _l2norm(x):
    return x * lax.rsqrt(jnp.sum(x * x, axis=-1, keepdims=True) + EPS)


def _sigmoid(x):
    return 0.5 * jnp.tanh(0.5 * x) + 0.5


def _silu(x):
    return x * _sigmoid(x)


def _softplus(x):
    return jnp.maximum(x, 0.0) + jnp.log1p(jnp.exp(-jnp.abs(x)))


def _log_sigmoid(x):
    return -_softplus(-x)


def _dot(a, b):
    return jnp.dot(a.astype(BF16), b.astype(BF16), preferred_element_type=F32)


def _dot_nt(a, b):
    return lax.dot_general(a.astype(BF16), b.astype(BF16), NT_DIMS, preferred_element_type=F32)


def _split3(x):
    hi = x.astype(BF16)
    r1 = x - hi.astype(F32)
    mid = r1.astype(BF16)
    lo = (r1 - mid.astype(F32)).astype(BF16)
    return hi, mid, lo


def _dot_select(sel, x):
    return sum(jnp.dot(sel, piece, preferred_element_type=F32) for piece in _split3(x))


def _iota2(shape, axis):
    return lax.broadcasted_iota(jnp.int32, shape, axis)


def _head(h):
    return slice(h * HEAD_DIM, (h + 1) * HEAD_DIM)


def _norm_kernel(x_ref, nw_ref, h_ref):
    h_ref[...] = _rms(x_ref[...], nw_ref[...]).astype(BF16)


def _norm(x, norm_w):
    t, d = x.shape
    tm = min(t, 1024)
    return pl.pallas_call(
        _norm_kernel,
        out_shape=jax.ShapeDtypeStruct((t, d), BF16),
        grid=(t // tm,),
        in_specs=[pl.BlockSpec((tm, d), lambda i: (i, 0)), pl.BlockSpec((1, d), lambda i: (0, 0))],
        out_specs=pl.BlockSpec((tm, d), lambda i: (i, 0)),
        compiler_params=_params("parallel"),
        name="norm",
    )(x, norm_w)


def _inproj_kernel(h_ref, w_ref, b_ref, o_ref):
    o_ref[...] = jnp.dot(h_ref[...], w_ref[...], preferred_element_type=F32) + b_ref[...]


def _repack_kernel(w_ref, o_ref, g_ref):
    def run(start, stop):
        lo = start // LANES * LANES
        hi = min(-(-stop // LANES) * LANES, w_ref.shape[-1])
        return w_ref[:, lo:hi][:, start - lo:stop - lo].astype(BF16)

    o_ref[...] = jnp.zeros_like(o_ref)
    for start, stop, dst in COLUMN_RUNS:
        o_ref[:, dst:dst + stop - start] = run(start, stop)
    g_ref[...] = run(*GATE_RUN)


def _repack_w_in(w_in):
    depth, d, n_in = w_in.shape
    tr = 64
    return pl.pallas_call(
        _repack_kernel,
        out_shape=(jax.ShapeDtypeStruct((depth, d, N_PERM), BF16),
                   jax.ShapeDtypeStruct((depth, d, N_GATE), BF16)),
        grid=(depth, d // tr),
        in_specs=[pl.BlockSpec((None, tr, n_in), lambda l, i: (l, i, 0))],
        out_specs=(pl.BlockSpec((None, tr, N_PERM), lambda l, i: (l, i, 0)),
                   pl.BlockSpec((None, tr, N_GATE), lambda l, i: (l, i, 0))),
        compiler_params=_params("parallel", "parallel"),
        name="repack_w_in",
    )(w_in)


def _inproj(hn, w_all, b, layer):
    t, d = hn.shape
    n = w_all.shape[2]
    tm = min(t, 1024)
    tn = INPROJ_TN
    return pl.pallas_call(
        _inproj_kernel,
        out_shape=jax.ShapeDtypeStruct((t, n), F32),
        grid=(t // tm, n // tn),
        in_specs=[pl.BlockSpec((tm, d), lambda i, j: (i, 0)),
                  pl.BlockSpec((None, d, tn), lambda i, j: (layer, 0, j)),
                  pl.BlockSpec((1, tn), lambda i, j: (0, j))],
        out_specs=pl.BlockSpec((tm, tn), lambda i, j: (i, j)),
        compiler_params=_params("parallel", "parallel"),
        name="inproj",
    )(hn, w_all, b)


def _fox_prep_kernel(h_ref, wq_ref, bq_ref, wg_ref, bg_ref, ws_ref, bs_ref, qw_ref, kw_ref,
                     qn_ref, kn_ref, vo_ref, kb_ref, vb_ref, gate_ref, lf_ref, c_ref, ct_ref, carry_ref, *, tm):
    @pl.when(pl.program_id(1) == 0)
    def _():
        carry_ref[...] = jnp.zeros_like(carry_ref)

    hn = h_ref[...]
    qkv = jnp.dot(hn, wq_ref[...], preferred_element_type=F32) + bq_ref[...]
    gate_ref[...] = _silu(jnp.dot(hn, wg_ref[...], preferred_element_type=F32) + bg_ref[...])
    sm = jnp.dot(hn, ws_ref[...], preferred_element_type=F32) + bs_ref[...]
    scale = HEAD_DIM ** -0.5
    for h in range(N_HEADS):
        qn_ref[:, _head(h)] = (_rms(qkv[:, _head(h)], qw_ref[...]) * scale).astype(qn_ref.dtype)
        kn = _rms(qkv[:, BR_WIDTH + h * HEAD_DIM:BR_WIDTH + (h + 1) * HEAD_DIM], kw_ref[...])
        v = qkv[:, 2 * BR_WIDTH + h * HEAD_DIM:2 * BR_WIDTH + (h + 1) * HEAD_DIM]
        kn_ref[pl.ds(h, tm, stride=N_HEADS), :] = kn
        vo_ref[pl.ds(h, tm, stride=N_HEADS), :] = v
        kb_ref[:, _head(h)] = kn.astype(BF16)
    vb_ref[...] = qkv[:, 2 * BR_WIDTH:].astype(BF16)
    lf = _log_sigmoid(sm)
    lf_ref[...] = lf[:, SM_FORGET:SM_FORGET + N_HEADS]
    tri = (_iota2((tm, tm), 0) >= _iota2((tm, tm), 1)).astype(BF16)
    c = _dot_select(tri, lf) + carry_ref[...]
    c_ref[...] = c
    ct_ref[...] = c.T
    carry_ref[...] = c[tm - 1:tm, :]


def _fox_prep(hn, w_p, b_p, layer, qn_w, kn_w, nb, s, q_dtype):
    t, d = hn.shape
    tm = min(s, 512)
    nj = s // tm
    row = lambda b, j: b * nj + j
    qkv_w = 3 * BR_WIDTH

    def cols(width, off):
        return (pl.BlockSpec((None, d, width), lambda b, j: (layer, 0, off // width)),
                pl.BlockSpec((1, width), lambda b, j: (0, off // width)))

    out_rows = pl.BlockSpec((tm, BR_WIDTH), lambda b, j: (row(b, j), 0))
    head_rows = pl.BlockSpec((tm * N_HEADS, HEAD_DIM), lambda b, j: (row(b, j), 0))
    wide_out = lambda dt: jax.ShapeDtypeStruct((t, BR_WIDTH), dt)
    head_out = jax.ShapeDtypeStruct((t * N_HEADS, HEAD_DIM), F32)
    return pl.pallas_call(
        functools.partial(_fox_prep_kernel, tm=tm),
        out_shape=(wide_out(q_dtype), head_out, head_out, wide_out(BF16), wide_out(BF16), wide_out(F32),
                   jax.ShapeDtypeStruct((t, N_HEADS), F32),
                   jax.ShapeDtypeStruct((t, LANES), F32),
                   jax.ShapeDtypeStruct((nb, LANES, s), F32)),
        grid=(nb, nj),
        in_specs=[pl.BlockSpec((tm, d), lambda b, j: (row(b, j), 0)),
                  *cols(qkv_w, C_FQ), *cols(BR_WIDTH, C_FG), *cols(LANES, C_SMALL),
                  pl.BlockSpec((1, HEAD_DIM), lambda b, j: (0, 0)),
                  pl.BlockSpec((1, HEAD_DIM), lambda b, j: (0, 0))],
        out_specs=(out_rows, head_rows, head_rows, out_rows, out_rows, out_rows,
                   pl.BlockSpec((tm, N_HEADS), lambda b, j: (row(b, j), 0)),
                   pl.BlockSpec((tm, LANES), lambda b, j: (row(b, j), 0)),
                   pl.BlockSpec((None, LANES, tm), lambda b, j: (b, 0, j))),
        scratch_shapes=[pltpu.VMEM((1, LANES), F32)],
        compiler_params=_params("parallel", "arbitrary"),
        name="fox_prep",
    )(hn, w_p, b_p, w_p, b_p, w_p, b_p, qn_w, kn_w)


def _fox_flash_kernel(q_ref, k_ref, v_ref, cq_ref, ck_ref, g_ref, o_ref, m_ref, acc_ref, cqb_ref, *, tq):
    i = pl.program_id(1)
    j = pl.program_id(2)
    lane_tiles = tq // LANES

    @pl.when(j == 0)
    def _():
        m_ref[...] = jnp.full_like(m_ref, NEG_BIG)
        acc_ref[...] = jnp.zeros_like(acc_ref)
        for h in range(N_HEADS):
            cqb_ref[h] = jnp.broadcast_to(cq_ref[:, SM_FORGET + h:SM_FORGET + h + 1], (tq, LANES))

    def update(diagonal):
        ones = jnp.ones((tq, HEAD_DIM), BF16)
        for h in range(N_HEADS):
            s = _dot_nt(q_ref[:, _head(h)], k_ref[:, _head(h)]) - ck_ref[h:h + 1, :]
            if diagonal:
                s = jnp.where(_iota2((tq, tq), 0) >= _iota2((tq, tq), 1), s, NEG_BIG)
            cqb = cqb_ref[h]
            m_prev = m_ref[h]
            m_new = jnp.maximum(m_prev, jnp.max(s, axis=-1, keepdims=True) + cqb)
            shift = m_new - cqb
            p = jnp.exp((s - jnp.concatenate([shift] * lane_tiles, axis=1)).astype(BF16))
            alpha = jnp.exp(m_prev - m_new)
            v_ones = jnp.concatenate([v_ref[:, _head(h)], ones], axis=1)
            acc_ref[h] = (jnp.concatenate([alpha, alpha], axis=1) * acc_ref[h]
                          + jnp.dot(p, v_ones, preferred_element_type=F32))
            m_ref[h] = m_new

    @pl.when(j < i)
    def _():
        update(False)

    @pl.when(j == i)
    def _():
        update(True)
        for h in range(N_HEADS):
            acc = acc_ref[h]
            o = acc[:, :HEAD_DIM] / acc[:, HEAD_DIM:]
            o_ref[:, _head(h)] = (o * g_ref[:, _head(h)]).astype(BF16)


def _fox_flash(qn, kn, v, c, ct, gate, nb, s):
    t = qn.shape[0]
    tq = min(s, 1024)
    nq = s // tq
    qrow = lambda b, i, j: (b * nq + i, 0)
    krow = lambda b, i, j: (b * nq + jnp.minimum(i, j), 0)
    stat = pltpu.VMEM((N_HEADS, tq, LANES), F32)
    return pl.pallas_call(
        functools.partial(_fox_flash_kernel, tq=tq),
        out_shape=jax.ShapeDtypeStruct((t, BR_WIDTH), BF16),
        grid=(nb, nq, nq),
        in_specs=[pl.BlockSpec((tq, BR_WIDTH), qrow),
                  pl.BlockSpec((tq, BR_WIDTH), krow),
                  pl.BlockSpec((tq, BR_WIDTH), krow),
                  pl.BlockSpec((tq, LANES), qrow),
                  pl.BlockSpec((None, SUBLANES, tq), lambda b, i, j: (b, SM_FORGET // SUBLANES, jnp.minimum(i, j))),
                  pl.BlockSpec((tq, BR_WIDTH), qrow)],
        out_specs=pl.BlockSpec((tq, BR_WIDTH), qrow),
        scratch_shapes=[stat, pltpu.VMEM((N_HEADS, tq, 2 * HEAD_DIM), F32), stat],
        compiler_params=_params("parallel", "parallel", "arbitrary"),
        name="fox_flash",
    )(qn, kn, v, c, ct, gate)


def _mem_kv_kernel(m_ref, nw_ref, w_ref, kw_ref, mk_ref, mv_ref):
    kv = _dot(_rms(m_ref[...], nw_ref[...]), w_ref[...])
    for h in range(N_HEADS):
        mk_ref[:, _head(h)] = _rms(kv[:, _head(h)], kw_ref[...])
    mv_ref[...] = kv[:, BR_WIDTH:]


def _mem_kv(mem, norm_w, w_kv, kn_w):
    t, d = mem.shape
    tm = min(t, 256)
    return pl.pallas_call(
        _mem_kv_kernel,
        out_shape=(jax.ShapeDtypeStruct((t, BR_WIDTH), F32), jax.ShapeDtypeStruct((t, BR_WIDTH), F32)),
        grid=(t // tm,),
        in_specs=[pl.BlockSpec((tm, d), lambda i: (i, 0)),
                  pl.BlockSpec((1, d), lambda i: (0, 0)),
                  pl.BlockSpec((d, 2 * BR_WIDTH), lambda i: (0, 0)),
                  pl.BlockSpec((1, HEAD_DIM), lambda i: (0, 0))],
        out_specs=(pl.BlockSpec((tm, BR_WIDTH), lambda i: (i, 0)),
                   pl.BlockSpec((tm, BR_WIDTH), lambda i: (i, 0))),
        compiler_params=_params("parallel"),
        name="mem_kv",
    )(mem, norm_w, w_kv, kn_w)


def _mem_attn_kernel(h_ref, wp_ref, bp_ref, k_ref, v_ref, qw_ref, o_ref):
    proj = jnp.dot(h_ref[...], wp_ref[...], preferred_element_type=F32) + bp_ref[...]
    scale = HEAD_DIM ** -0.5
    for h in range(N_HEADS):
        q = _rms(proj[:, _head(h)], qw_ref[...]) * scale
        s = _dot_nt(q, k_ref[:, _head(h)])
        p = jnp.exp(s - jnp.max(s, axis=-1, keepdims=True))
        o = _dot(p, v_ref[:, _head(h)]) / jnp.sum(p, axis=-1, keepdims=True)
        gate = _silu(proj[:, BR_WIDTH + h * HEAD_DIM:BR_WIDTH + (h + 1) * HEAD_DIM])
        o_ref[:, _head(h)] = (o * gate).astype(BF16)


def _mem_attn(hn, w_p, b_p, layer, mk, mv, qn_w, nb, s):
    t, d = hn.shape
    pair_w = 2 * BR_WIDTH
    m = mk.shape[0] // nb
    tq = min(s, 512)
    nq = s // tq
    return pl.pallas_call(
        _mem_attn_kernel,
        out_shape=jax.ShapeDtypeStruct((t, BR_WIDTH), BF16),
        grid=(nb, nq),
        in_specs=[pl.BlockSpec((tq, d), lambda b, i: (b * nq + i, 0)),
                  pl.BlockSpec((None, d, pair_w), lambda b, i: (layer, 0, C_MQ // pair_w)),
                  pl.BlockSpec((1, pair_w), lambda b, i: (0, C_MQ // pair_w)),
                  pl.BlockSpec((m, BR_WIDTH), lambda b, i: (b, 0)),
                  pl.BlockSpec((m, BR_WIDTH), lambda b, i: (b, 0)),
                  pl.BlockSpec((1, HEAD_DIM), lambda b, i: (0, 0))],
        out_specs=pl.BlockSpec((tq, BR_WIDTH), lambda b, i: (b * nq + i, 0)),
        compiler_params=_params("parallel", "parallel"),
        name="mem_attn",
    )(hn, w_p, b_p, mk, mv, qn_w)


def _causal_conv_tile(x, halo, w_ref):
    rows8 = _iota2(halo.shape, 0)
    y = x * w_ref[CONV_W - 1:CONV_W, :]
    for k in range(1, CONV_W):
        xr = pltpu.roll(x, k, axis=0)
        hr = pltpu.roll(halo, k, axis=0)
        top = jnp.where(rows8 < k, hr, xr[:SUBLANES])
        xs = jnp.concatenate([top, xr[SUBLANES:]], axis=0)
        y = y + xs * w_ref[CONV_W - 1 - k:CONV_W - k, :]
    return y


def _unit_lower_inverses(mats, row, col):
    n = mats[0].shape[0]

    def off_block(shift):
        return (((row >> (shift + 1)) == (col >> (shift + 1)))
                & (((row >> shift) & 1) == 1) & (((col >> shift) & 1) == 0))

    eye = (row == col).astype(F32)
    first = off_block(0)
    ds = [eye - jnp.where(first, a, 0.0) for a in mats]
    shift = 1
    while (1 << shift) < n:
        mask = off_block(shift)
        ts = [_dot(jnp.where(mask, a, 0.0), d) for a, d in zip(mats, ds)]
        ds = [d - _dot(d, t) for d, t in zip(ds, ts)]
        shift += 1
    return ds


def _dn_kernel(h_ref, wm_ref, bm_ref, ws_ref, bs_ref, cw_ref, al_ref, dt_ref, ow_ref,
               o_ref, st_ref, tail_ref, stt_ref, halo_ref, *, tl):
    j = pl.program_id(1)

    @pl.when(j == 0)
    def _():
        stt_ref[...] = jnp.zeros_like(stt_ref)
        halo_ref[...] = jnp.zeros_like(halo_ref)

    hn = h_ref[...]
    proj = jnp.dot(hn, wm_ref[...], preferred_element_type=F32) + bm_ref[...]
    sm = jnp.dot(hn, ws_ref[...], preferred_element_type=F32) + bs_ref[...]
    x = proj[:, :3 * BR_WIDTH]
    z = proj[:, 3 * BR_WIDTH:]
    qkv = _silu(_causal_conv_tile(x, halo_ref[...], cw_ref))
    halo_ref[...] = x[tl - SUBLANES:, :]
    tail_ref[...] = x[tl - SUBLANES:, :]
    beta_all = _sigmoid(sm)
    la_all = -jnp.exp(al_ref[...]) * _softplus(sm + dt_ref[...])

    cc = DN_CHUNK
    row = _iota2((cc, cc), 0)
    col = _iota2((cc, cc), 1)
    incl = row >= col
    strict = row > col
    tri = incl.astype(BF16)
    kd = N_HEADS * HEAD_DIM

    systems = []
    for c in range(tl // cc):
        rows = slice(c * cc, (c + 1) * cc)
        g_all = _dot_select(tri, la_all[rows])
        gt_all = g_all.T
        eg_all = jnp.exp(g_all)
        for h in range(N_HEADS):
            q = _l2norm(qkv[rows, h * HEAD_DIM:(h + 1) * HEAD_DIM]) * (HEAD_DIM ** -0.5)
            k = _l2norm(qkv[rows, kd + h * HEAD_DIM:kd + (h + 1) * HEAD_DIM])
            v = qkv[rows, 2 * kd + h * HEAD_DIM:2 * kd + (h + 1) * HEAD_DIM]
            beta = beta_all[rows, SM_BETA + h:SM_BETA + h + 1]
            gc = g_all[:, SM_DECAY + h:SM_DECAY + h + 1]
            gr = gt_all[SM_DECAY + h:SM_DECAY + h + 1, :]
            eg = eg_all[:, SM_DECAY + h:SM_DECAY + h + 1]
            g_last = g_all[cc - 1:cc, SM_DECAY + h:SM_DECAY + h + 1]
            decay = jnp.exp(jnp.where(incl, gc - gr, NEG_BIG))
            kb = k.astype(BF16)
            qb = q.astype(BF16)
            systems.append(dict(
                a=jnp.where(strict, beta * decay * _dot_nt(kb, kb), 0.0),
                qk=(decay * _dot_nt(qb, kb)).astype(BF16),
                rhs=jnp.concatenate([beta * v, (beta * eg) * k], axis=1).astype(BF16),
                kdec_t=(jnp.exp(g_last - gc) * k).T.astype(BF16),
                q=qb, eg=eg, g_end=jnp.exp(g_last)))

    ds = _unit_lower_inverses([sy["a"] for sy in systems], row, col)
    sols = [_dot(d, sy["rhs"]) for d, sy in zip(ds, systems)]

    heads = range(N_HEADS)
    stts = [stt_ref[h] for h in heads]
    for c in range(tl // cc):
        rows = slice(c * cc, (c + 1) * cc)
        sys_c = systems[c * N_HEADS:(c + 1) * N_HEADS]
        sol_c = sols[c * N_HEADS:(c + 1) * N_HEADS]
        us = [sol_c[h][:, :HEAD_DIM] - _dot(sol_c[h][:, HEAD_DIM:], stts[h]) for h in heads]
        os = [sys_c[h]["eg"] * _dot(sys_c[h]["q"], stts[h]) + _dot(sys_c[h]["qk"], us[h]) for h in heads]
        stts = [sys_c[h]["g_end"] * stts[h] + _dot(sys_c[h]["kdec_t"], us[h]) for h in heads]
        for h in heads:
            o = _rms(os[h], ow_ref[...]) * _silu(z[rows, _head(h)])
            o_ref[rows, _head(h)] = o.astype(BF16)
    for h in heads:
        stt_ref[h] = stts[h]

    @pl.when(j == pl.num_programs(1) - 1)
    def _():
        for h in range(N_HEADS):
            st_ref[h] = stt_ref[h].T


def _dn_prompt(hn, w_p, b_p, layer, conv_w, al_row, dt_row, onorm_w, nb, s):
    t, d = hn.shape
    tl = min(s, 512)
    nj = s // tl
    qkv_w = 3 * BR_WIDTH
    main_w = qkv_w + BR_WIDTH
    row = lambda b, j: b * nj + j
    return pl.pallas_call(
        functools.partial(_dn_kernel, tl=tl),
        out_shape=(jax.ShapeDtypeStruct((t, BR_WIDTH), BF16),
                   jax.ShapeDtypeStruct((nb, N_HEADS, HEAD_DIM, HEAD_DIM), F32),
                   jax.ShapeDtypeStruct((nb, SUBLANES, qkv_w), F32)),
        grid=(nb, nj),
        in_specs=[pl.BlockSpec((tl, d), lambda b, j: (row(b, j), 0)),
                  pl.BlockSpec((None, d, main_w), lambda b, j: (layer, 0, C_DNQKV // main_w)),
                  pl.BlockSpec((1, main_w), lambda b, j: (0, C_DNQKV // main_w)),
                  pl.BlockSpec((None, d, LANES), lambda b, j: (layer, 0, C_SMALL // LANES)),
                  pl.BlockSpec((1, LANES), lambda b, j: (0, C_SMALL // LANES)),
                  pl.BlockSpec((CONV_W, qkv_w), lambda b, j: (0, 0)),
                  pl.BlockSpec((1, LANES), lambda b, j: (0, 0)),
                  pl.BlockSpec((1, LANES), lambda b, j: (0, 0)),
                  pl.BlockSpec((1, HEAD_DIM), lambda b, j: (0, 0))],
        out_specs=(pl.BlockSpec((tl, BR_WIDTH), lambda b, j: (row(b, j), 0)),
                   pl.BlockSpec((None, N_HEADS, HEAD_DIM, HEAD_DIM), lambda b, j: (b, 0, 0, 0)),
                   pl.BlockSpec((None, SUBLANES, qkv_w), lambda b, j: (b, 0, 0))),
        scratch_shapes=[pltpu.VMEM((N_HEADS, HEAD_DIM, HEAD_DIM), F32),
                        pltpu.VMEM((SUBLANES, qkv_w), F32)],
        compiler_params=_params("parallel", "arbitrary"),
        name="dn_prompt",
    )(hn, w_p, b_p, w_p, b_p, conv_w, al_row, dt_row, onorm_w)


def _lru_gates(xc, wr_ref, br_ref, wi_ref, bi_ref, lam_ref):
    r = _sigmoid(_dot(xc, wr_ref[...]) + br_ref[...])
    i = _sigmoid(_dot(xc, wi_ref[...]) + bi_ref[...])
    log_a = -LRU_C * r * _softplus(-lam_ref[...])
    a = jnp.exp(log_a)
    one_minus_a2 = -jnp.tanh(log_a) * (a * a + 1.0)
    return a, jnp.sqrt(one_minus_a2) * (i * xc)


def _lru_kernel(h_ref, wp_ref, bp_ref, cw_ref, cb_ref, wr_ref, br_ref, wi_ref, bi_ref, lam_ref,
                o_ref, hl_ref, tail_ref, hc_ref, halo_ref, *, tl):
    @pl.when(pl.program_id(1) == 0)
    def _():
        hc_ref[...] = jnp.zeros_like(hc_ref)
        halo_ref[...] = jnp.zeros_like(halo_ref)

    proj = jnp.dot(h_ref[...], wp_ref[...], preferred_element_type=F32) + bp_ref[...]
    x = proj[:, :BR_WIDTH]
    xc = _causal_conv_tile(x, halo_ref[...], cw_ref) + cb_ref[...]
    halo_ref[...] = x[tl - SUBLANES:, :]
    tail_ref[...] = x[tl - SUBLANES:, :]
    a, b = _lru_gates(xc, wr_ref, br_ref, wi_ref, bi_ref, lam_ref)
    rows = _iota2(a.shape, 0)
    d = 1
    while d < tl:
        a_up = jnp.where(rows >= d, pltpu.roll(a, d, axis=0), 1.0)
        b_up = jnp.where(rows >= d, pltpu.roll(b, d, axis=0), 0.0)
        b = a * b_up + b
        a = a * a_up
        d *= 2
    hseq = a * hc_ref[...] + b
    hc_ref[...] = hseq[tl - 1:tl, :]
    hl_ref[...] = hseq[tl - 1:tl, :]
    o_ref[...] = (hseq * _silu(proj[:, BR_WIDTH:])).astype(BF16)


def _lru_prompt(hn, w_p, b_p, layer, conv_w, conv_b, w_r, b_r, w_i, b_i, lam, nb, s):
    t, d = hn.shape
    pair_w = 2 * BR_WIDTH
    tl = min(s, 512)
    nj = s // tl
    row = lambda b, j: b * nj + j
    vec = pl.BlockSpec((1, BR_WIDTH), lambda b, j: (0, 0))
    mat = pl.BlockSpec((BR_WIDTH, BR_WIDTH), lambda b, j: (0, 0))
    return pl.pallas_call(
        functools.partial(_lru_kernel, tl=tl),
        out_shape=(jax.ShapeDtypeStruct((t, BR_WIDTH), BF16),
                   jax.ShapeDtypeStruct((nb, 1, BR_WIDTH), F32),
                   jax.ShapeDtypeStruct((nb, SUBLANES, BR_WIDTH), F32)),
        grid=(nb, nj),
        in_specs=[pl.BlockSpec((tl, d), lambda b, j: (row(b, j), 0)),
                  pl.BlockSpec((None, d, pair_w), lambda b, j: (layer, 0, C_LRUX // pair_w)),
                  pl.BlockSpec((1, pair_w), lambda b, j: (0, C_LRUX // pair_w)),
                  pl.BlockSpec((CONV_W, BR_WIDTH), lambda b, j: (0, 0)),
                  vec, mat, vec, mat, vec, vec],
        out_specs=(pl.BlockSpec((tl, BR_WIDTH), lambda b, j: (row(b, j), 0)),
                   pl.BlockSpec((None, 1, BR_WIDTH), lambda b, j: (b, 0, 0)),
                   pl.BlockSpec((None, SUBLANES, BR_WIDTH), lambda b, j: (b, 0, 0))),
        scratch_shapes=[pltpu.VMEM((1, BR_WIDTH), F32), pltpu.VMEM((SUBLANES, BR_WIDTH), F32)],
        compiler_params=_params("parallel", "arbitrary"),
        name="lru_prompt",
    )(hn, w_p, b_p, conv_w, conv_b, w_r, b_r, w_i, b_i, lam)


def _merge_kernel(x_ref, h_ref, a_ref, b_ref, c_ref, m_ref, wg_ref, bg_ref, wb_ref, wo_ref, y_ref):
    x = x_ref[...]
    d = x.shape[1]
    h = h_ref[...]
    merged = None
    for n, br in enumerate((a_ref, b_ref, c_ref, m_ref)):
        cols = slice(n * d, (n + 1) * d)
        gate = jnp.dot(h, wg_ref[:, cols], preferred_element_type=F32) + bg_ref[:, cols]
        up = jnp.dot(br[...], wb_ref[n], preferred_element_type=F32)
        term = _sigmoid(gate) * up
        merged = term if merged is None else merged + term
    y_ref[...] = x + jnp.dot(merged.astype(BF16), wo_ref[...], preferred_element_type=F32)


def _merge(x, hn, out_a, out_b, out_c, out_m, w_gate, b_gate, w_branch, w_out, layer):
    t, d = x.shape
    tm = min(t, 512)
    rows = lambda w: pl.BlockSpec((tm, w), lambda i: (i, 0))
    return pl.pallas_call(
        _merge_kernel,
        out_shape=jax.ShapeDtypeStruct((t, d), F32),
        grid=(t // tm,),
        in_specs=[rows(d), rows(d),
                  rows(BR_WIDTH), rows(BR_WIDTH), rows(BR_WIDTH), rows(BR_WIDTH),
                  pl.BlockSpec((None, d, N_GATE), lambda i: (layer, 0, 0)),
                  pl.BlockSpec((1, N_GATE), lambda i: (0, 0)),
                  pl.BlockSpec((N_BRANCH, BR_WIDTH, d), lambda i: (0, 0, 0)),
                  pl.BlockSpec((d, d), lambda i: (0, 0))],
        out_specs=rows(d),
        compiler_params=_params("parallel"),
        name="merge",
    )(x, hn, out_a, out_b, out_c, out_m, w_gate, b_gate, w_branch, w_out)


def _sample_step_kernel(qkv_ref, z_ref, lx_ref, lg_ref, sm_ref, st_ref, dbuf_ref, lh_ref, lbuf_ref,
                        dcw_ref, al_ref, dt_ref, ow_ref, lcw_ref, lcb_ref, wr_ref, br_ref, wi_ref, bi_ref, lam_ref,
                        oa_ref, ob_ref, sto_ref, dbo_ref, lho_ref, lbo_ref, o_scr, *, bs):
    kd = N_HEADS * HEAD_DIM

    def step_conv(x, buf_ref, bufo_ref, w_ref):
        y = x * w_ref[CONV_W - 1:CONV_W, :]
        for k in range(CONV_W - 1):
            y = y + buf_ref[:, k, :] * w_ref[k:k + 1, :]
        for k in range(CONV_W - 2):
            bufo_ref[:, k, :] = buf_ref[:, k + 1, :]
        bufo_ref[:, CONV_W - 2, :] = x
        return y

    qkv = _silu(step_conv(qkv_ref[...], dbuf_ref, dbo_ref, dcw_ref))
    sm = sm_ref[...]
    beta_all = _sigmoid(sm)
    eg_all = jnp.exp(-jnp.exp(al_ref[...]) * _softplus(sm + dt_ref[...]))
    eye = _iota2((HEAD_DIM, HEAD_DIM), 0) == _iota2((HEAD_DIM, HEAD_DIM), 1)
    pad = jnp.zeros((SUBLANES - 2, HEAD_DIM), F32)
    items = []
    for h in range(N_HEADS):
        q = _l2norm(qkv[:, h * HEAD_DIM:(h + 1) * HEAD_DIM]) * (HEAD_DIM ** -0.5)
        k = _l2norm(qkv[:, kd + h * HEAD_DIM:kd + (h + 1) * HEAD_DIM])
        v = qkv[:, 2 * kd + h * HEAD_DIM:2 * kd + (h + 1) * HEAD_DIM]
        beta = beta_all[:, SM_BETA + h:SM_BETA + h + 1]
        eg = eg_all[:, SM_DECAY + h:SM_DECAY + h + 1]
        qk = jnp.sum(q * k, axis=-1, keepdims=True)
        for b in range(bs):
            one = slice(b, b + 1)
            items.append(dict(b=b, h=h, q=q[one], k=k[one], v=v[one], beta=beta[one], eg=eg[one], qk=qk[one]))
    sk_sqs = [_dot_nt(jnp.concatenate([it["k"], it["q"], pad], axis=0), st_ref[it["b"], it["h"]])
              for it in items]
    us = [it["beta"] * (it["v"] - it["eg"] * r[0:1]) for it, r in zip(items, sk_sqs)]
    for it, r, u in zip(items, sk_sqs, us):
        o_scr[it["b"]:it["b"] + 1, _head(it["h"])] = it["eg"] * r[1:2] + it["qk"] * u
    updates = [_dot(jnp.where(eye, jnp.broadcast_to(u, (HEAD_DIM, HEAD_DIM)), 0.0),
                    jnp.broadcast_to(it["k"], (HEAD_DIM, HEAD_DIM))) for it, u in zip(items, us)]
    for it, upd in zip(items, updates):
        sto_ref[it["b"], it["h"]] = it["eg"] * st_ref[it["b"], it["h"]] + upd
    o = o_scr[...]
    for h in range(N_HEADS):
        oa_ref[:, _head(h)] = (_rms(o[:, _head(h)], ow_ref[...]) * _silu(z_ref[:, _head(h)])).astype(BF16)

    xc = step_conv(lx_ref[...], lbuf_ref, lbo_ref, lcw_ref) + lcb_ref[...]
    a, bx = _lru_gates(xc, wr_ref, br_ref, wi_ref, bi_ref, lam_ref)
    hnew = a * lh_ref[...] + bx
    lho_ref[...] = hnew
    ob_ref[...] = (hnew * _silu(lg_ref[...])).astype(BF16)


def _sample_step(proj, layer, state_dn, state_dn_conv, state_lru_h, state_lru_conv,
                 dn_conv_w, al_row, dt_row, onorm_w, lru_conv_w, lru_conv_b, w_r, b_r, w_i, b_i, lam):
    nb = proj.shape[0]
    bs = SUBLANES
    qkv_w = 3 * BR_WIDTH
    wide = lambda off: pl.BlockSpec((bs, BR_WIDTH), lambda i: (i, off // BR_WIDTH))
    const = lambda shape: pl.BlockSpec(shape, lambda i: (0,) * len(shape))
    st_in = pl.BlockSpec((None, bs, N_HEADS, HEAD_DIM, HEAD_DIM), lambda i: (layer, i, 0, 0, 0))
    dbuf_in = pl.BlockSpec((None, bs, CONV_W - 1, qkv_w), lambda i: (layer, i, 0, 0))
    lh_in = pl.BlockSpec((None, bs, BR_WIDTH), lambda i: (layer, i, 0))
    lbuf_in = pl.BlockSpec((None, bs, CONV_W - 1, BR_WIDTH), lambda i: (layer, i, 0, 0))
    return pl.pallas_call(
        functools.partial(_sample_step_kernel, bs=bs),
        out_shape=(jax.ShapeDtypeStruct((nb, BR_WIDTH), BF16),
                   jax.ShapeDtypeStruct((nb, BR_WIDTH), BF16),
                   jax.ShapeDtypeStruct((nb, N_HEADS, HEAD_DIM, HEAD_DIM), F32),
                   jax.ShapeDtypeStruct((nb, CONV_W - 1, qkv_w), F32),
                   jax.ShapeDtypeStruct((nb, BR_WIDTH), F32),
                   jax.ShapeDtypeStruct((nb, CONV_W - 1, BR_WIDTH), F32)),
        grid=(nb // bs,),
        in_specs=[pl.BlockSpec((bs, qkv_w), lambda i: (i, 0)), wide(C_DNZ), wide(C_LRUX), wide(C_LRUG),
                  pl.BlockSpec((bs, LANES), lambda i: (i, C_SMALL // LANES)),
                  st_in, dbuf_in, lh_in, lbuf_in,
                  const((CONV_W, qkv_w)), const((1, LANES)), const((1, LANES)), const((1, HEAD_DIM)),
                  const((CONV_W, BR_WIDTH)), const((1, BR_WIDTH)),
                  const((BR_WIDTH, BR_WIDTH)), const((1, BR_WIDTH)),
                  const((BR_WIDTH, BR_WIDTH)), const((1, BR_WIDTH)), const((1, BR_WIDTH))],
        out_specs=(pl.BlockSpec((bs, BR_WIDTH), lambda i: (i, 0)),
                   pl.BlockSpec((bs, BR_WIDTH), lambda i: (i, 0)),
                   pl.BlockSpec((bs, N_HEADS, HEAD_DIM, HEAD_DIM), lambda i: (i, 0, 0, 0)),
                   pl.BlockSpec((bs, CONV_W - 1, qkv_w), lambda i: (i, 0, 0)),
                   pl.BlockSpec((bs, BR_WIDTH), lambda i: (i, 0)),
                   pl.BlockSpec((bs, CONV_W - 1, BR_WIDTH), lambda i: (i, 0, 0))),
        scratch_shapes=[pltpu.VMEM((bs, BR_WIDTH), F32)],
        compiler_params=_params("parallel"),
        name="sample_step",
    )(proj, proj, proj, proj, proj, state_dn, state_dn_conv, state_lru_h, state_lru_conv,
      dn_conv_w, al_row, dt_row, onorm_w, lru_conv_w, lru_conv_b, w_r, b_r, w_i, b_i, lam)


def _own_head_mask(n_rows_kv):
    shape = (SUBLANES, n_rows_kv)
    return (_iota2(shape, 1) % N_HEADS) == _iota2(shape, 0)


def _heads_to_row(o8):
    return jnp.concatenate([o8[h:h + 1] for h in range(N_HEADS)], axis=1)


def _pool_bias_kernel(lf_ref, after_ref, same_ref, bias_ref):
    in_page = None
    total = None
    for h in range(N_HEADS):
        for piece in _split3(lf_ref[:, h, :]):
            a = jnp.dot(piece, after_ref[h], preferred_element_type=F32)
            t = jnp.dot(piece, same_ref[h], preferred_element_type=F32)
            in_page = a if in_page is None else in_page + a
            total = t if total is None else total + t
    bias_ref[:, 0, :] = in_page
    bias_ref[:, 1, :] = total


def _pool_bias(cache_lf):
    n, nh, page = cache_lf.shape
    w = page * nh
    tp = min(n, 512)
    src_tok = jnp.arange(page)[None, :, None]
    src_head = jnp.arange(nh)[:, None, None]
    dst = jnp.arange(w)[None, None, :]
    same = (dst % nh) == src_head
    after = same & (src_tok > dst // nh)
    const = pl.BlockSpec((nh, page, w), lambda i: (0, 0, 0))
    return pl.pallas_call(
        _pool_bias_kernel,
        out_shape=jax.ShapeDtypeStruct((n, 2, w), F32),
        grid=(n // tp,),
        in_specs=[pl.BlockSpec((tp, nh, page), lambda i: (i, 0, 0)), const, const],
        out_specs=pl.BlockSpec((tp, 2, w), lambda i: (i, 0, 0)),
        compiler_params=_params("parallel"),
        name="pool_bias",
    )(cache_lf, after.astype(BF16), jnp.broadcast_to(same, after.shape).astype(BF16))


def _fox_decode_kernel(pt_ref, q_ref, kn_ref, vn_ref, lfn_ref, g_ref, *rest, n_pages, n_seq):
    n_blk = n_seq * n_pages
    k_refs, v_refs, bias_refs = rest[:n_blk], rest[n_blk:2 * n_blk], rest[2 * n_blk:3 * n_blk]
    o_ref = rest[3 * n_blk]
    del pt_ref
    own = _own_head_mask(k_refs[0].shape[0])
    for i in range(n_seq):
        pages = range(i * n_pages, (i + 1) * n_pages)
        q8 = q_ref[i]
        qb = q8.astype(BF16)

        later = lfn_ref[i]
        scores = {}
        for p in reversed(pages):
            s = _dot_nt(qb, k_refs[p][...]) + (bias_refs[p][0:1, :] + later)
            scores[p] = jnp.where(own, s, NEG_BIG)
            later = later + bias_refs[p][1:2, :]
        s_self = jnp.sum(q8 * kn_ref[i], axis=-1, keepdims=True)
        m = s_self
        for p in pages:
            m = jnp.maximum(m, jnp.max(scores[p], axis=-1, keepdims=True))
        p_self = jnp.exp(s_self - m)
        l = p_self
        acc = p_self * vn_ref[i]
        for p in pages:
            pr = jnp.exp(scores[p] - m)
            l = l + jnp.sum(pr, axis=-1, keepdims=True)
            acc = acc + _dot(pr, v_refs[p][...])
        o_ref[i] = (_heads_to_row(acc / l) * g_ref[i]).astype(BF16)


def _fox_decode(q8, k8, v8, lf_new, gate, cache_k, cache_v, page_bias, page_table, layer, n_pool):
    nb = q8.shape[0]
    n_pages = page_table.shape[0] // nb
    n_seq = DECODE_SEQS if nb % DECODE_SEQS == 0 else 1
    kv_rows = cache_k.shape[1]
    base = layer * n_pool
    row = pl.BlockSpec((n_seq, 1, BR_WIDTH), lambda b, pt: (b, 0, 0))
    heads = pl.BlockSpec((n_seq, SUBLANES, HEAD_DIM), lambda b, pt: (b, 0, 0))

    def paged(shape, i, p):
        return pl.BlockSpec((None,) + shape, lambda b, pt: (base + pt[(b * n_seq + i) * n_pages + p], 0, 0))

    blocks = [(i, p) for i in range(n_seq) for p in range(n_pages)]
    in_specs = [heads, heads, heads, pl.BlockSpec((n_seq, 1, kv_rows), lambda b, pt: (b, 0, 0)), row]
    in_specs += [paged((kv_rows, HEAD_DIM), i, p) for i, p in blocks]
    in_specs += [paged((kv_rows, HEAD_DIM), i, p) for i, p in blocks]
    in_specs += [paged((2, kv_rows), i, p) for i, p in blocks]
    grid_spec = pltpu.PrefetchScalarGridSpec(
        num_scalar_prefetch=1,
        grid=(nb // n_seq,),
        in_specs=in_specs,
        out_specs=row)
    n_blk = len(blocks)
    return pl.pallas_call(
        functools.partial(_fox_decode_kernel, n_pages=n_pages, n_seq=n_seq),
        out_shape=jax.ShapeDtypeStruct((nb, 1, BR_WIDTH), BF16),
        grid_spec=grid_spec,
        compiler_params=_params("parallel"),
        name="fox_decode",
    )(page_table, q8, k8, v8, lf_new, gate, *([cache_k] * n_blk), *([cache_v] * n_blk), *([page_bias] * n_blk))


def _mem_decode_kernel(q_ref, g_ref, k_ref, v_ref, qw_ref, o_ref, o_scr, *, bs):
    own = _own_head_mask(k_ref.shape[1])
    scale = HEAD_DIM ** -0.5
    qs = [_rms(q_ref[:, _head(h)], qw_ref[...]) * scale for h in range(N_HEADS)]
    pad = jnp.zeros((SUBLANES - N_HEADS, HEAD_DIM), F32)
    for b in range(bs):
        q8 = jnp.concatenate([q[b:b + 1] for q in qs] + [pad], axis=0)
        s = jnp.where(own, _dot_nt(q8, k_ref[b]), NEG_BIG)
        p = jnp.exp(s - jnp.max(s, axis=-1, keepdims=True))
        o = _dot(p, v_ref[b]) / jnp.sum(p, axis=-1, keepdims=True)
        o_scr[b:b + 1, :] = _heads_to_row(o)
    o_ref[...] = (o_scr[...] * _silu(g_ref[...])).astype(BF16)


def _mem_decode(proj, cache_k, cache_v, qn_w, layer):
    nb = proj.shape[0]
    bs = SUBLANES
    m = cache_k.shape[1]
    base = layer * (nb // bs)
    kv = pl.BlockSpec((bs, m, HEAD_DIM), lambda i: (base + i, 0, 0))
    return pl.pallas_call(
        functools.partial(_mem_decode_kernel, bs=bs),
        out_shape=jax.ShapeDtypeStruct((nb, BR_WIDTH), BF16),
        grid=(nb // bs,),
        in_specs=[pl.BlockSpec((bs, BR_WIDTH), lambda i: (i, C_MQ // BR_WIDTH)),
                  pl.BlockSpec((bs, BR_WIDTH), lambda i: (i, C_MG // BR_WIDTH)),
                  kv, kv, pl.BlockSpec((1, HEAD_DIM), lambda i: (0, 0))],
        out_specs=pl.BlockSpec((bs, BR_WIDTH), lambda i: (i, 0)),
        scratch_shapes=[pltpu.VMEM((bs, BR_WIDTH), F32)],
        compiler_params=_params("parallel"),
        name="mem_decode",
    )(proj, proj, cache_k, cache_v, qn_w)


def _permute_columns(a):
    runs = sorted(COLUMN_RUNS, key=lambda r: r[2])
    pad = jnp.zeros(a.shape[:-1] + (N_PERM - C_SMALL - N_SMALL,), a.dtype)
    return jnp.concatenate([a[..., start:stop] for start, stop, _ in runs] + [pad], axis=-1)


def _block_diag(w):
    nblk, e, f = w.shape
    eye = jnp.eye(nblk, dtype=w.dtype)
    return (eye[:, None, :, None] * w[:, :, None, :]).reshape(nblk * e, nblk * f)


def _decay_lane_row(v):
    return jnp.zeros((1, LANES), F32).at[0, SM_DECAY:SM_DECAY + N_HEADS].set(v)


def kernel(x_prompt, x_sample, cache_fox_k, cache_fox_v, cache_fox_logf, cache_mem_k, cache_mem_v, state_dn, state_dn_conv, state_lru_h, state_lru_conv, page_table, mem_prompt, norm_w, w_in, b_in, dn_conv_w, dn_A_log, dn_dt_bias, dn_onorm_w, lru_conv_w, lru_conv_b, lru_w_r, lru_b_r, lru_w_i, lru_b_i, lru_lambda, fox_qn_w, fox_kn_w, mem_norm_w, w_mem_kv, mem_qn_w, mem_kn_w, w_branch, w_out):
    bp, s, d = x_prompt.shape
    bd = x_sample.shape[0]
    depth = w_in.shape[0]
    n_pool, page = cache_fox_k.shape[1], cache_fox_k.shape[2]
    mem_tokens = mem_prompt.shape[1]
    tp = bp * s

    yp = x_prompt.reshape(tp, d)
    ys = x_sample.reshape(bd, d)
    mem2 = mem_prompt.reshape(bp * mem_tokens, d)
    kv_rows = page * N_HEADS
    cache_k2 = cache_fox_k.reshape(depth * n_pool, kv_rows, HEAD_DIM)
    cache_v2 = cache_fox_v.reshape(depth * n_pool, kv_rows, HEAD_DIM)
    page_bias = _pool_bias(jnp.swapaxes(cache_fox_logf, 2, 3).reshape(depth * n_pool, N_HEADS, page))
    cmem_k2 = cache_mem_k.reshape(depth * bd, mem_tokens * N_HEADS, HEAD_DIM)
    cmem_v2 = cache_mem_v.reshape(depth * bd, mem_tokens * N_HEADS, HEAD_DIM)
    pt_flat = page_table.reshape(-1)

    row = lambda v: v.reshape(1, -1)
    acc = {n: [] for n in ("pk", "pv", "plf", "pmk", "pmv", "pdn", "pdc", "plh", "plc",
                           "sk", "sv", "slf", "sdn", "sdc", "slh", "slc")}
    w_p, w_g = _repack_w_in(w_in)
    for l in range(depth):
        b_p = _permute_columns(b_in[l]).reshape(1, N_PERM)
        b_g = b_in[l, GATE_RUN[0]:GATE_RUN[1]].reshape(1, N_GATE)
        nw = row(norm_w[l])
        al_row = _decay_lane_row(dn_A_log[l])
        dt_row = _decay_lane_row(dn_dt_bias[l])
        ow = row(dn_onorm_w[l])
        wr = _block_diag(lru_w_r[l]).astype(BF16)
        wi = _block_diag(lru_w_i[l]).astype(BF16)
        lru_args = (lru_conv_w[l], row(lru_conv_b[l]), wr, row(lru_b_r[l]), wi, row(lru_b_i[l]), row(lru_lambda[l]))
        wb = w_branch[l].astype(BF16)
        wo = w_out[l].astype(BF16)
        fqw, fkw, mqw = row(fox_qn_w[l]), row(fox_kn_w[l]), row(mem_qn_w[l])

        hp = _norm(yp, nw)
        qb, kn, fv, kb, vb, gate_c, lf, c, ct = _fox_prep(hp, w_p, b_p, l, fqw, fkw, bp, s, BF16)
        out_c = _fox_flash(qb, kb, vb, c, ct, gate_c, bp, s)
        mk, mv = _mem_kv(mem2, row(mem_norm_w[l]), w_mem_kv[l].astype(BF16), row(mem_kn_w[l]))
        out_m = _mem_attn(hp, w_p, b_p, l, mk, mv, mqw, bp, s)
        out_a, dn_s, dn_tail = _dn_prompt(hp, w_p, b_p, l, dn_conv_w[l], al_row, dt_row, ow, bp, s)
        out_b, lru_h, lru_tail = _lru_prompt(hp, w_p, b_p, l, *lru_args, bp, s)
        acc["pk"].append(kn.reshape(bp, s, N_HEADS, HEAD_DIM))
        acc["pv"].append(fv.reshape(bp, s, N_HEADS, HEAD_DIM))
        acc["plf"].append(lf.reshape(bp, s, N_HEADS))
        acc["pmk"].append(mk.reshape(bp, mem_tokens, N_HEADS, HEAD_DIM))
        acc["pmv"].append(mv.reshape(bp, mem_tokens, N_HEADS, HEAD_DIM))
        acc["pdn"].append(dn_s)
        acc["pdc"].append(dn_tail[:, SUBLANES - (CONV_W - 1):, :])
        acc["plh"].append(lru_h.reshape(bp, BR_WIDTH))
        acc["plc"].append(lru_tail[:, SUBLANES - (CONV_W - 1):, :])
        yp = _merge(yp, hp, out_a, out_b, out_c, out_m, w_g, b_g, wb, wo, l)

        hs = _norm(ys, nw)
        proj_s = _inproj(hs, w_p, b_p, l)
        qn_s, kn_s, fv_s, _, _, gate_s, lf_s, _, _ = _fox_prep(hs, w_p, b_p, l, fqw, fkw, 1, bd, F32)
        head_rows = lambda a: jnp.pad(a.reshape(bd, N_HEADS, HEAD_DIM), ((0, 0), (0, SUBLANES - N_HEADS), (0, 0)))
        lf_new = jnp.tile(lf_s, (1, page)).reshape(bd, 1, kv_rows)
        out_c_s = _fox_decode(head_rows(qn_s), head_rows(kn_s), head_rows(fv_s), lf_new,
                              gate_s.reshape(bd, 1, BR_WIDTH),
                              cache_k2, cache_v2, page_bias, pt_flat, l, n_pool).reshape(bd, BR_WIDTH)
        out_m_s = _mem_decode(proj_s, cmem_k2, cmem_v2, mqw, l)
        out_a_s, out_b_s, dn_s_s, dn_c_s, lru_h_s, lru_c_s = _sample_step(
            proj_s, l, state_dn, state_dn_conv, state_lru_h, state_lru_conv,
            dn_conv_w[l], al_row, dt_row, ow, *lru_args)
        acc["sk"].append(kn_s.reshape(bd, 1, N_HEADS, HEAD_DIM))
        acc["sv"].append(fv_s.reshape(bd, 1, N_HEADS, HEAD_DIM))
        acc["slf"].append(lf_s.reshape(bd, 1, N_HEADS))
        acc["sdn"].append(dn_s_s)
        acc["sdc"].append(dn_c_s)
        acc["slh"].append(lru_h_s)
        acc["slc"].append(lru_c_s)
        ys = _merge(ys, hs, out_a_s, out_b_s, out_c_s, out_m_s, w_g, b_g, wb, wo, l)

    st = lambda n: jnp.stack(acc[n])
    return (yp.reshape(bp, s, d), ys.reshape(bd, 1, d),
            st("pk"), st("pv"), st("plf"), st("pmk"), st("pmv"), st("pdn"), st("pdc"), st("plh"), st("plc"),
            st("sk"), st("sv"), st("slf"), st("sdn"), st("sdc"), st("slh"), st("slc"))
```

```python
import functools

import jax
import jax.numpy as jnp
from jax import lax
from jax.experimental import pallas as pl
from jax.experimental.pallas import tpu as pltpu

F32 = jnp.float32
BF16 = jnp.bfloat16

EPS = 1e-6
LRU_C = 8.0
CONV_W = 4
N_HEADS = 4
HEAD_DIM = 128
BR_WIDTH = N_HEADS * HEAD_DIM
N_BRANCH = 4
LANES = 128
SUBLANES = 8
DN_CHUNK = 128
DECODE_SEQS = 2
NEG_BIG = -1e30
VMEM_LIMIT_BYTES = 56 * 1024 * 1024

C_DNQKV = 0
C_DNZ = 1536
C_LRUX = 2048
C_LRUG = 2560
C_FQ = 3072
C_FK = 3584
C_FV = 4096
C_FG = 4608
C_MQ = 5120
C_MG = 5632
C_SMALL = 6144
N_PERM = 6400
INPROJ_TN = 1280
N_SMALL = 3 * N_HEADS
N_GATE = 4096
COLUMN_RUNS = ((0, 2048, 0), (2056, 4616, 2048), (4620, 6156, 4608), (2048, 2056, C_SMALL), (4616, 4620, C_SMALL + 8))
GATE_RUN = (6156, 10252)
SM_BETA = 0
SM_DECAY = 4
SM_FORGET = 8

NT_DIMS = (((1,), (1,)), ((), ()))


def _params(*sem):
    return pltpu.CompilerParams(dimension_semantics=sem, vmem_limit_bytes=VMEM_LIMIT_BYTES)


def _rms(x, w):
    return x * lax.rsqrt(jnp.mean(x * x, axis=-1, keepdims=True) + EPS) * w


def _l2norm(x):
    return x * lax.rsqrt(jnp.sum(x * x, axis=-1, keepdims=True) + EPS)


def _sigmoid(x):
    return 0.5 * jnp.tanh(0.5 * x) + 0.5


def _silu(x):
    return x * _sigmoid(x)


def _softplus(x):
    return jnp.maximum(x, 0.0) + jnp.log1p(jnp.exp(-jnp.abs(x)))


def _log_sigmoid(x):
    return -_softplus(-x)


def _dot(a, b):
    return jnp.dot(a.astype(BF16), b.astype(BF16), preferred_element_type=F32)


def _dot_nt(a, b):
    return lax.dot_general(a.astype(BF16), b.astype(BF16), NT_DIMS, preferred_element_type=F32)


def _split3(x):
    hi = x.astype(BF16)
    r1 = x - hi.astype(F32)
    mid = r1.astype(BF16)
    lo = (r1 - mid.astype(F32)).astype(BF16)
    return hi, mid, lo


def _dot_select(sel, x):
    return sum(jnp.dot(sel, piece, preferred_element_type=F32) for piece in _split3(x))


def _iota2(shape, axis):
    return lax.broadcasted_iota(jnp.int32, shape, axis)


def _head(h):
    return slice(h * HEAD_DIM, (h + 1) * HEAD_DIM)


def _norm_kernel(x_ref, nw_ref, h_ref):
    h_ref[...] = _rms(x_ref[...], nw_ref[...]).astype(BF16)


def _norm(x, norm_w):
    t, d = x.shape
    tm = min(t, 1024)
    return pl.pallas_call(
        _norm_kernel,
        out_shape=jax.ShapeDtypeStruct((t, d), BF16),
        grid=(t // tm,),
        in_specs=[pl.BlockSpec((tm, d), lambda i: (i, 0)), pl.BlockSpec((1, d), lambda i: (0, 0))],
        out_specs=pl.BlockSpec((tm, d), lambda i: (i, 0)),
        compiler_params=_params("parallel"),
        name="norm",
    )(x, norm_w)


def _inproj_kernel(h_ref, w_ref, b_ref, o_ref):
    o_ref[...] = jnp.dot(h_ref[...], w_ref[...], preferred_element_type=F32) + b_ref[...]


def _repack_kernel(w_ref, o_ref, g_ref):
    def run(start, stop):
        lo = start // LANES * LANES
        hi = min(-(-stop // LANES) * LANES, w_ref.shape[-1])
        return w_ref[:, lo:hi][:, start - lo:stop - lo].astype(BF16)

    o_ref[...] = jnp.zeros_like(o_ref)
    for start, stop, dst in COLUMN_RUNS:
        o_ref[:, dst:dst + stop - start] = run(start, stop)
    g_ref[...] = run(*GATE_RUN)


def _repack_w_in(w_in):
    depth, d, n_in = w_in.shape
    tr = 64
    return pl.pallas_call(
        _repack_kernel,
        out_shape=(jax.ShapeDtypeStruct((depth, d, N_PERM), BF16),
                   jax.ShapeDtypeStruct((depth, d, N_GATE), BF16)),
        grid=(depth, d // tr),
        in_specs=[pl.BlockSpec((None, tr, n_in), lambda l, i: (l, i, 0))],
        out_specs=(pl.BlockSpec((None, tr, N_PERM), lambda l, i: (l, i, 0)),
                   pl.BlockSpec((None, tr, N_GATE), lambda l, i: (l, i, 0))),
        compiler_params=_params("parallel", "parallel"),
        name="repack_w_in",
    )(w_in)


def _inproj(hn, w_all, b, layer):
    t, d = hn.shape
    n = w_all.shape[2]
    tm = min(t, 1024)
    tn = INPROJ_TN
    return pl.pallas_call(
        _inproj_kernel,
        out_shape=jax.ShapeDtypeStruct((t, n), F32),
        grid=(t // tm, n // tn),
        in_specs=[pl.BlockSpec((tm, d), lambda i, j: (i, 0)),
                  pl.BlockSpec((None, d, tn), lambda i, j: (layer, 0, j)),
                  pl.BlockSpec((1, tn), lambda i, j: (0, j))],
        out_specs=pl.BlockSpec((tm, tn), lambda i, j: (i, j)),
        compiler_params=_params("parallel", "parallel"),
        name="inproj",
    )(hn, w_all, b)


def _fox_prep_kernel(h_ref, wq_ref, bq_ref, wg_ref, bg_ref, ws_ref, bs_ref, qw_ref, kw_ref,
                     qn_ref, kn_ref, vo_ref, kb_ref, vb_ref, gate_ref, lf_ref, c_ref, ct_ref, carry_ref, *, tm):
    @pl.when(pl.program_id(1) == 0)
    def _():
        carry_ref[...] = jnp.zeros_like(carry_ref)

    hn = h_ref[...]
    qkv = jnp.dot(hn, wq_ref[...], preferred_element_type=F32) + bq_ref[...]
    gate_ref[...] = _silu(jnp.dot(hn, wg_ref[...], preferred_element_type=F32) + bg_ref[...])
    sm = jnp.dot(hn, ws_ref[...], preferred_element_type=F32) + bs_ref[...]
    scale = HEAD_DIM ** -0.5
    for h in range(N_HEADS):
        qn_ref[:, _head(h)] = (_rms(qkv[:, _head(h)], qw_ref[...]) * scale).astype(qn_ref.dtype)
        kn = _rms(qkv[:, BR_WIDTH + h * HEAD_DIM:BR_WIDTH + (h + 1) * HEAD_DIM], kw_ref[...])
        v = qkv[:, 2 * BR_WIDTH + h * HEAD_DIM:2 * BR_WIDTH + (h + 1) * HEAD_DIM]
        kn_ref[pl.ds(h, tm, stride=N_HEADS), :] = kn
        vo_ref[pl.ds(h, tm, stride=N_HEADS), :] = v
        kb_ref[:, _head(h)] = kn.astype(BF16)
    vb_ref[...] = qkv[:, 2 * BR_WIDTH:].astype(BF16)
    lf = _log_sigmoid(sm)
    lf_ref[...] = lf[:, SM_FORGET:SM_FORGET + N_HEADS]
    tri = (_iota2((tm, tm), 0) >= _iota2((tm, tm), 1)).astype(BF16)
    c = _dot_select(tri, lf) + carry_ref[...]
    c_ref[...] = c
    ct_ref[...] = c.T
    carry_ref[...] = c[tm - 1:tm, :]


def _fox_prep(hn, w_p, b_p, layer, qn_w, kn_w, nb, s, q_dtype):
    t, d = hn.shape
    tm = min(s, 512)
    nj = s // tm
    row = lambda b, j: b * nj + j
    qkv_w = 3 * BR_WIDTH

    def cols(width, off):
        return (pl.BlockSpec((None, d, width), lambda b, j: (layer, 0, off // width)),
                pl.BlockSpec((1, width), lambda b, j: (0, off // width)))

    out_rows = pl.BlockSpec((tm, BR_WIDTH), lambda b, j: (row(b, j), 0))
    head_rows = pl.BlockSpec((tm * N_HEADS, HEAD_DIM), lambda b, j: (row(b, j), 0))
    wide_out = lambda dt: jax.ShapeDtypeStruct((t, BR_WIDTH), dt)
    head_out = jax.ShapeDtypeStruct((t * N_HEADS, HEAD_DIM), F32)
    return pl.pallas_call(
        functools.partial(_fox_prep_kernel, tm=tm),
        out_shape=(wide_out(q_dtype), head_out, head_out, wide_out(BF16), wide_out(BF16), wide_out(F32),
                   jax.ShapeDtypeStruct((t, N_HEADS), F32),
                   jax.ShapeDtypeStruct((t, LANES), F32),
                   jax.ShapeDtypeStruct((nb, LANES, s), F32)),
        grid=(nb, nj),
        in_specs=[pl.BlockSpec((tm, d), lambda b, j: (row(b, j), 0)),
                  *cols(qkv_w, C_FQ), *cols(BR_WIDTH, C_FG), *cols(LANES, C_SMALL),
                  pl.BlockSpec((1, HEAD_DIM), lambda b, j: (0, 0)),
                  pl.BlockSpec((1, HEAD_DIM), lambda b, j: (0, 0))],
        out_specs=(out_rows, head_rows, head_rows, out_rows, out_rows, out_rows,
                   pl.BlockSpec((tm, N_HEADS), lambda b, j: (row(b, j), 0)),
                   pl.BlockSpec((tm, LANES), lambda b, j: (row(b, j), 0)),
                   pl.BlockSpec((None, LANES, tm), lambda b, j: (b, 0, j))),
        scratch_shapes=[pltpu.VMEM((1, LANES), F32)],
        compiler_params=_params("parallel", "arbitrary"),
        name="fox_prep",
    )(hn, w_p, b_p, w_p, b_p, w_p, b_p, qn_w, kn_w)


def _fox_flash_kernel(q_ref, k_ref, v_ref, cq_ref, ck_ref, g_ref, o_ref, m_ref, acc_ref, cqb_ref, *, tq):
    i = pl.program_id(1)
    j = pl.program_id(2)
    lane_tiles = tq // LANES

    @pl.when(j == 0)
    def _():
        m_ref[...] = jnp.full_like(m_ref, NEG_BIG)
        acc_ref[...] = jnp.zeros_like(acc_ref)
        for h in range(N_HEADS):
            cqb_ref[h] = jnp.broadcast_to(cq_ref[:, SM_FORGET + h:SM_FORGET + h + 1], (tq, LANES))

    def update(diagonal):
        ones = jnp.ones((tq, HEAD_DIM), BF16)
        for h in range(N_HEADS):
            s = _dot_nt(q_ref[:, _head(h)], k_ref[:, _head(h)]) - ck_ref[h:h + 1, :]
            if diagonal:
                s = jnp.where(_iota2((tq, tq), 0) >= _iota2((tq, tq), 1), s, NEG_BIG)
            cqb = cqb_ref[h]
            m_prev = m_ref[h]
            m_new = jnp.maximum(m_prev, jnp.max(s, axis=-1, keepdims=True) + cqb)
            shift = m_new - cqb
            p = jnp.exp((s - jnp.concatenate([shift] * lane_tiles, axis=1)).astype(BF16))
            alpha = jnp.exp(m_prev - m_new)
            v_ones = jnp.concatenate([v_ref[:, _head(h)], ones], axis=1)
            acc_ref[h] = (jnp.concatenate([alpha, alpha], axis=1) * acc_ref[h]
                          + jnp.dot(p, v_ones, preferred_element_type=F32))
            m_ref[h] = m_new

    @pl.when(j < i)
    def _():
        update(False)

    @pl.when(j == i)
    def _():
        update(True)
        for h in range(N_HEADS):
            acc = acc_ref[h]
            o = acc[:, :HEAD_DIM] / acc[:, HEAD_DIM:]
            o_ref[:, _head(h)] = (o * g_ref[:, _head(h)]).astype(BF16)


def _fox_flash(qn, kn, v, c, ct, gate, nb, s):
    t = qn.shape[0]
    tq = min(s, 1024)
    nq = s // tq
    qrow = lambda b, i, j: (b * nq + i, 0)
    krow = lambda b, i, j: (b * nq + jnp.minimum(i, j), 0)
    stat = pltpu.VMEM((N_HEADS, tq, LANES), F32)
    return pl.pallas_call(
        functools.partial(_fox_flash_kernel, tq=tq),
        out_shape=jax.ShapeDtypeStruct((t, BR_WIDTH), BF16),
        grid=(nb, nq, nq),
        in_specs=[pl.BlockSpec((tq, BR_WIDTH), qrow),
                  pl.BlockSpec((tq, BR_WIDTH), krow),
                  pl.BlockSpec((tq, BR_WIDTH), krow),
                  pl.BlockSpec((tq, LANES), qrow),
                  pl.BlockSpec((None, SUBLANES, tq), lambda b, i, j: (b, SM_FORGET // SUBLANES, jnp.minimum(i, j))),
                  pl.BlockSpec((tq, BR_WIDTH), qrow)],
        out_specs=pl.BlockSpec((tq, BR_WIDTH), qrow),
        scratch_shapes=[stat, pltpu.VMEM((N_HEADS, tq, 2 * HEAD_DIM), F32), stat],
        compiler_params=_params("parallel", "parallel", "arbitrary"),
        name="fox_flash",
    )(qn, kn, v, c, ct, gate)


def _mem_kv_kernel(m_ref, nw_ref, w_ref, kw_ref, mk_ref, mv_ref):
    kv = _dot(_rms(m_ref[...], nw_ref[...]), w_ref[...])
    for h in range(N_HEADS):
        mk_ref[:, _head(h)] = _rms(kv[:, _head(h)], kw_ref[...])
    mv_ref[...] = kv[:, BR_WIDTH:]


def _mem_kv(mem, norm_w, w_kv, kn_w):
    t, d = mem.shape
    tm = min(t, 256)
    return pl.pallas_call(
        _mem_kv_kernel,
        out_shape=(jax.ShapeDtypeStruct((t, BR_WIDTH), F32), jax.ShapeDtypeStruct((t, BR_WIDTH), F32)),
        grid=(t // tm,),
        in_specs=[pl.BlockSpec((tm, d), lambda i: (i, 0)),
                  pl.BlockSpec((1, d), lambda i: (0, 0)),
                  pl.BlockSpec((d, 2 * BR_WIDTH), lambda i: (0, 0)),
                  pl.BlockSpec((1, HEAD_DIM), lambda i: (0, 0))],
        out_specs=(pl.BlockSpec((tm, BR_WIDTH), lambda i: (i, 0)),
                   pl.BlockSpec((tm, BR_WIDTH), lambda i: (i, 0))),
        compiler_params=_params("parallel"),
        name="mem_kv",
    )(mem, norm_w, w_kv, kn_w)


def _mem_attn_kernel(h_ref, wp_ref, bp_ref, k_ref, v_ref, qw_ref, o_ref):
    proj = jnp.dot(h_ref[...], wp_ref[...], preferred_element_type=F32) + bp_ref[...]
    scale = HEAD_DIM ** -0.5
    for h in range(N_HEADS):
        q = _rms(proj[:, _head(h)], qw_ref[...]) * scale
        s = _dot_nt(q, k_ref[:, _head(h)])
        p = jnp.exp(s - jnp.max(s, axis=-1, keepdims=True))
        o = _dot(p, v_ref[:, _head(h)]) / jnp.sum(p, axis=-1, keepdims=True)
        gate = _silu(proj[:, BR_WIDTH + h * HEAD_DIM:BR_WIDTH + (h + 1) * HEAD_DIM])
        o_ref[:, _head(h)] = (o * gate).astype(BF16)


def _mem_attn(hn, w_p, b_p, layer, mk, mv, qn_w, nb, s):
    t, d = hn.shape
    pair_w = 2 * BR_WIDTH
    m = mk.shape[0] // nb
    tq = min(s, 512)
    nq = s // tq
    return pl.pallas_call(
        _mem_attn_kernel,
        out_shape=jax.ShapeDtypeStruct((t, BR_WIDTH), BF16),
        grid=(nb, nq),
        in_specs=[pl.BlockSpec((tq, d), lambda b, i: (b * nq + i, 0)),
                  pl.BlockSpec((None, d, pair_w), lambda b, i: (layer, 0, C_MQ // pair_w)),
                  pl.BlockSpec((1, pair_w), lambda b, i: (0, C_MQ // pair_w)),
                  pl.BlockSpec((m, BR_WIDTH), lambda b, i: (b, 0)),
                  pl.BlockSpec((m, BR_WIDTH), lambda b, i: (b, 0)),
                  pl.BlockSpec((1, HEAD_DIM), lambda b, i: (0, 0))],
        out_specs=pl.BlockSpec((tq, BR_WIDTH), lambda b, i: (b * nq + i, 0)),
        compiler_params=_params("parallel", "parallel"),
        name="mem_attn",
    )(hn, w_p, b_p, mk, mv, qn_w)


def _causal_conv_tile(x, halo, w_ref):
    rows8 = _iota2(halo.shape, 0)
    y = x * w_ref[CONV_W - 1:CONV_W, :]
    for k in range(1, CONV_W):
        xr = pltpu.roll(x, k, axis=0)
        hr = pltpu.roll(halo, k, axis=0)
        top = jnp.where(rows8 < k, hr, xr[:SUBLANES])
        xs = jnp.concatenate([top, xr[SUBLANES:]], axis=0)
        y = y + xs * w_ref[CONV_W - 1 - k:CONV_W - k, :]
    return y


def _unit_lower_inverses(mats, row, col):
    n = mats[0].shape[0]

    def off_block(shift):
        return (((row >> (shift + 1)) == (col >> (shift + 1)))
                & (((row >> shift) & 1) == 1) & (((col >> shift) & 1) == 0))

    eye = (row == col).astype(F32)
    first = off_block(0)
    ds = [eye - jnp.where(first, a, 0.0) for a in mats]
    shift = 1
    while (1 << shift) < n:
        mask = off_block(shift)
        ts = [_dot(jnp.where(mask, a, 0.0), d) for a, d in zip(mats, ds)]
        ds = [d - _dot(d, t) for d, t in zip(ds, ts)]
        shift += 1
    return ds


def _dn_kernel(h_ref, wm_ref, bm_ref, ws_ref, bs_ref, cw_ref, al_ref, dt_ref, ow_ref,
               o_ref, st_ref, tail_ref, stt_ref, halo_ref, *, tl):
    j = pl.program_id(1)

    @pl.when(j == 0)
    def _():
        stt_ref[...] = jnp.zeros_like(stt_ref)
        halo_ref[...] = jnp.zeros_like(halo_ref)

    hn = h_ref[...]
    proj = jnp.dot(hn, wm_ref[...], preferred_element_type=F32) + bm_ref[...]
    sm = jnp.dot(hn, ws_ref[...], preferred_element_type=F32) + bs_ref[...]
    x = proj[:, :3 * BR_WIDTH]
    z = proj[:, 3 * BR_WIDTH:]
    qkv = _silu(_causal_conv_tile(x, halo_ref[...], cw_ref))
    halo_ref[...] = x[tl - SUBLANES:, :]
    tail_ref[...] = x[tl - SUBLANES:, :]
    beta_all = _sigmoid(sm)
    la_all = -jnp.exp(al_ref[...]) * _softplus(sm + dt_ref[...])

    cc = DN_CHUNK
    row = _iota2((cc, cc), 0)
    col = _iota2((cc, cc), 1)
    incl = row >= col
    strict = row > col
    tri = incl.astype(BF16)
    kd = N_HEADS * HEAD_DIM

    systems = []
    for c in range(tl // cc):
        rows = slice(c * cc, (c + 1) * cc)
        g_all = _dot_select(tri, la_all[rows])
        gt_all = g_all.T
        eg_all = jnp.exp(g_all)
        for h in range(N_HEADS):
            q = _l2norm(qkv[rows, h * HEAD_DIM:(h + 1) * HEAD_DIM]) * (HEAD_DIM ** -0.5)
            k = _l2norm(qkv[rows, kd + h * HEAD_DIM:kd + (h + 1) * HEAD_DIM])
            v = qkv[rows, 2 * kd + h * HEAD_DIM:2 * kd + (h + 1) * HEAD_DIM]
            beta = beta_all[rows, SM_BETA + h:SM_BETA + h + 1]
            gc = g_all[:, SM_DECAY + h:SM_DECAY + h + 1]
            gr = gt_all[SM_DECAY + h:SM_DECAY + h + 1, :]
            eg = eg_all[:, SM_DECAY + h:SM_DECAY + h + 1]
            g_last = g_all[cc - 1:cc, SM_DECAY + h:SM_DECAY + h + 1]
            decay = jnp.exp(jnp.where(incl, gc - gr, NEG_BIG))
            kb = k.astype(BF16)
            qb = q.astype(BF16)
            systems.append(dict(
                a=jnp.where(strict, beta * decay * _dot_nt(kb, kb), 0.0),
                qk=(decay * _dot_nt(qb, kb)).astype(BF16),
                rhs=jnp.concatenate([beta * v, (beta * eg) * k], axis=1).astype(BF16),
                kdec_t=(jnp.exp(g_last - gc) * k).T.astype(BF16),
                q=qb, eg=eg, g_end=jnp.exp(g_last)))

    ds = _unit_lower_inverses([sy["a"] for sy in systems], row, col)
    sols = [_dot(d, sy["rhs"]) for d, sy in zip(ds, systems)]

    heads = range(N_HEADS)
    stts = [stt_ref[h] for h in heads]
    for c in range(tl // cc):
        rows = slice(c * cc, (c + 1) * cc)
        sys_c = systems[c * N_HEADS:(c + 1) * N_HEADS]
        sol_c = sols[c * N_HEADS:(c + 1) * N_HEADS]
        us = [sol_c[h][:, :HEAD_DIM] - _dot(sol_c[h][:, HEAD_DIM:], stts[h]) for h in heads]
        os = [sys_c[h]["eg"] * _dot(sys_c[h]["q"], stts[h]) + _dot(sys_c[h]["qk"], us[h]) for h in heads]
        stts = [sys_c[h]["g_end"] * stts[h] + _dot(sys_c[h]["kdec_t"], us[h]) for h in heads]
        for h in heads:
            o = _rms(os[h], ow_ref[...]) * _silu(z[rows, _head(h)])
            o_ref[rows, _head(h)] = o.astype(BF16)
    for h in heads:
        stt_ref[h] = stts[h]

    @pl.when(j == pl.num_programs(1) - 1)
    def _():
        for h in range(N_HEADS):
            st_ref[h] = stt_ref[h].T


def _dn_prompt(hn, w_p, b_p, layer, conv_w, al_row, dt_row, onorm_w, nb, s):
    t, d = hn.shape
    tl = min(s, 512)
    nj = s // tl
    qkv_w = 3 * BR_WIDTH
    main_w = qkv_w + BR_WIDTH
    row = lambda b, j: b * nj + j
    return pl.pallas_call(
        functools.partial(_dn_kernel, tl=tl),
        out_shape=(jax.ShapeDtypeStruct((t, BR_WIDTH), BF16),
                   jax.ShapeDtypeStruct((nb, N_HEADS, HEAD_DIM, HEAD_DIM), F32),
                   jax.ShapeDtypeStruct((nb, SUBLANES, qkv_w), F32)),
        grid=(nb, nj),
        in_specs=[pl.BlockSpec((tl, d), lambda b, j: (row(b, j), 0)),
                  pl.BlockSpec((None, d, main_w), lambda b, j: (layer, 0, C_DNQKV // main_w)),
                  pl.BlockSpec((1, main_w), lambda b, j: (0, C_DNQKV // main_w)),
                  pl.BlockSpec((None, d, LANES), lambda b, j: (layer, 0, C_SMALL // LANES)),
                  pl.BlockSpec((1, LANES), lambda b, j: (0, C_SMALL // LANES)),
                  pl.BlockSpec((CONV_W, qkv_w), lambda b, j: (0, 0)),
                  pl.BlockSpec((1, LANES), lambda b, j: (0, 0)),
                  pl.BlockSpec((1, LANES), lambda b, j: (0, 0)),
                  pl.BlockSpec((1, HEAD_DIM), lambda b, j: (0, 0))],
        out_specs=(pl.BlockSpec((tl, BR_WIDTH), lambda b, j: (row(b, j), 0)),
                   pl.BlockSpec((None, N_HEADS, HEAD_DIM, HEAD_DIM), lambda b, j: (b, 0, 0, 0)),
                   pl.BlockSpec((None, SUBLANES, qkv_w), lambda b, j: (b, 0, 0))),
        scratch_shapes=[pltpu.VMEM((N_HEADS, HEAD_DIM, HEAD_DIM), F32),
                        pltpu.VMEM((SUBLANES, qkv_w), F32)],
        compiler_params=_params("parallel", "arbitrary"),
        name="dn_prompt",
    )(hn, w_p, b_p, w_p, b_p, conv_w, al_row, dt_row, onorm_w)


def _lru_gates(xc, wr_ref, br_ref, wi_ref, bi_ref, lam_ref):
    r = _sigmoid(_dot(xc, wr_ref[...]) + br_ref[...])
    i = _sigmoid(_dot(xc, wi_ref[...]) + bi_ref[...])
    log_a = -LRU_C * r * _softplus(-lam_ref[...])
    a = jnp.exp(log_a)
    one_minus_a2 = -jnp.tanh(log_a) * (a * a + 1.0)
    return a, jnp.sqrt(one_minus_a2) * (i * xc)


def _lru_kernel(h_ref, wp_ref, bp_ref, cw_ref, cb_ref, wr_ref, br_ref, wi_ref, bi_ref, lam_ref,
                o_ref, hl_ref, tail_ref, hc_ref, halo_ref, *, tl):
    @pl.when(pl.program_id(1) == 0)
    def _():
        hc_ref[...] = jnp.zeros_like(hc_ref)
        halo_ref[...] = jnp.zeros_like(halo_ref)

    proj = jnp.dot(h_ref[...], wp_ref[...], preferred_element_type=F32) + bp_ref[...]
    x = proj[:, :BR_WIDTH]
    xc = _causal_conv_tile(x, halo_ref[...], cw_ref) + cb_ref[...]
    halo_ref[...] = x[tl - SUBLANES:, :]
    tail_ref[...] = x[tl - SUBLANES:, :]
    a, b = _lru_gates(xc, wr_ref, br_ref, wi_ref, bi_ref, lam_ref)
    rows = _iota2(a.shape, 0)
    d = 1
    while d < tl:
        a_up = jnp.where(rows >= d, pltpu.roll(a, d, axis=0), 1.0)
        b_up = jnp.where(rows >= d, pltpu.roll(b, d, axis=0), 0.0)
        b = a * b_up + b
        a = a * a_up
        d *= 2
    hseq = a * hc_ref[...] + b
    hc_ref[...] = hseq[tl - 1:tl, :]
    hl_ref[...] = hseq[tl - 1:tl, :]
    o_ref[...] = (hseq * _silu(proj[:, BR_WIDTH:])).astype(BF16)


def _lru_prompt(hn, w_p, b_p, layer, conv_w, conv_b, w_r, b_r, w_i, b_i, lam, nb, s):
    t, d = hn.shape
    pair_w = 2 * BR_WIDTH
    tl = min(s, 512)
    nj = s // tl
    row = lambda b, j: b * nj + j
    vec = pl.BlockSpec((1, BR_WIDTH), lambda b, j: (0, 0))
    mat = pl.BlockSpec((BR_WIDTH, BR_WIDTH), lambda b, j: (0, 0))
    return pl.pallas_call(
        functools.partial(_lru_kernel, tl=tl),
        out_shape=(jax.ShapeDtypeStruct((t, BR_WIDTH), BF16),
                   jax.ShapeDtypeStruct((nb, 1, BR_WIDTH), F32),
                   jax.ShapeDtypeStruct((nb, SUBLANES, BR_WIDTH), F32)),
        grid=(nb, nj),
        in_specs=[pl.BlockSpec((tl, d), lambda b, j: (row(b, j), 0)),
                  pl.BlockSpec((None, d, pair_w), lambda b, j: (layer, 0, C_LRUX // pair_w)),
                  pl.BlockSpec((1, pair_w), lambda b, j: (0, C_LRUX // pair_w)),
                  pl.BlockSpec((CONV_W, BR_WIDTH), lambda b, j: (0, 0)),
                  vec, mat, vec, mat, vec, vec],
        out_specs=(pl.BlockSpec((tl, BR_WIDTH), lambda b, j: (row(b, j), 0)),
                   pl.BlockSpec((None, 1, BR_WIDTH), lambda b, j: (b, 0, 0)),
                   pl.BlockSpec((None, SUBLANES, BR_WIDTH), lambda b, j: (b, 0, 0))),
        scratch_shapes=[pltpu.VMEM((1, BR_WIDTH), F32), pltpu.VMEM((SUBLANES, BR_WIDTH), F32)],
        compiler_params=_params("parallel", "arbitrary"),
        name="lru_prompt",
    )(hn, w_p, b_p, conv_w, conv_b, w_r, b_r, w_i, b_i, lam)


def _merge_kernel(x_ref, h_ref, a_ref, b_ref, c_ref, m_ref, wg_ref, bg_ref, wb_ref, wo_ref, y_ref):
    x = x_ref[...]
    d = x.shape[1]
    h = h_ref[...]
    merged = None
    for n, br in enumerate((a_ref, b_ref, c_ref, m_ref)):
        cols = slice(n * d, (n + 1) * d)
        gate = jnp.dot(h, wg_ref[:, cols], preferred_element_type=F32) + bg_ref[:, cols]
        up = jnp.dot(br[...], wb_ref[n], preferred_element_type=F32)
        term = _sigmoid(gate) * up
        merged = term if merged is None else merged + term
    y_ref[...] = x + jnp.dot(merged.astype(BF16), wo_ref[...], preferred_element_type=F32)


def _merge(x, hn, out_a, out_b, out_c, out_m, w_gate, b_gate, w_branch, w_out, layer):
    t, d = x.shape
    tm = min(t, 1024)
    rows = lambda w: pl.BlockSpec((tm, w), lambda i: (i, 0))
    resident = pl.Buffered(1)
    return pl.pallas_call(
        _merge_kernel,
        out_shape=jax.ShapeDtypeStruct((t, d), F32),
        grid=(t // tm,),
        in_specs=[rows(d), rows(d),
                  rows(BR_WIDTH), rows(BR_WIDTH), rows(BR_WIDTH), rows(BR_WIDTH),
                  pl.BlockSpec((None, d, N_GATE), lambda i: (layer, 0, 0), pipeline_mode=resident),
                  pl.BlockSpec((1, N_GATE), lambda i: (0, 0)),
                  pl.BlockSpec((N_BRANCH, BR_WIDTH, d), lambda i: (0, 0, 0), pipeline_mode=resident),
                  pl.BlockSpec((d, d), lambda i: (0, 0), pipeline_mode=resident)],
        out_specs=rows(d),
        compiler_params=_params("parallel"),
        name="merge",
    )(x, hn, out_a, out_b, out_c, out_m, w_gate, b_gate, w_branch, w_out)


def _sample_step_kernel(qkv_ref, z_ref, lx_ref, lg_ref, sm_ref, st_ref, dbuf_ref, lh_ref, lbuf_ref,
                        dcw_ref, al_ref, dt_ref, ow_ref, lcw_ref, lcb_ref, wr_ref, br_ref, wi_ref, bi_ref, lam_ref,
                        oa_ref, ob_ref, sto_ref, dbo_ref, lho_ref, lbo_ref, o_scr, *, bs):
    kd = N_HEADS * HEAD_DIM

    def step_conv(x, buf_ref, bufo_ref, w_ref):
        y = x * w_ref[CONV_W - 1:CONV_W, :]
        for k in range(CONV_W - 1):
            y = y + buf_ref[:, k, :] * w_ref[k:k + 1, :]
        for k in range(CONV_W - 2):
            bufo_ref[:, k, :] = buf_ref[:, k + 1, :]
        bufo_ref[:, CONV_W - 2, :] = x
        return y

    qkv = _silu(step_conv(qkv_ref[...], dbuf_ref, dbo_ref, dcw_ref))
    sm = sm_ref[...]
    beta_all = _sigmoid(sm)
    eg_all = jnp.exp(-jnp.exp(al_ref[...]) * _softplus(sm + dt_ref[...]))
    eye = _iota2((HEAD_DIM, HEAD_DIM), 0) == _iota2((HEAD_DIM, HEAD_DIM), 1)
    pad = jnp.zeros((SUBLANES - 2, HEAD_DIM), F32)
    items = []
    for h in range(N_HEADS):
        q = _l2norm(qkv[:, h * HEAD_DIM:(h + 1) * HEAD_DIM]) * (HEAD_DIM ** -0.5)
        k = _l2norm(qkv[:, kd + h * HEAD_DIM:kd + (h + 1) * HEAD_DIM])
        v = qkv[:, 2 * kd + h * HEAD_DIM:2 * kd + (h + 1) * HEAD_DIM]
        beta = beta_all[:, SM_BETA + h:SM_BETA + h + 1]
        eg = eg_all[:, SM_DECAY + h:SM_DECAY + h + 1]
        qk = jnp.sum(q * k, axis=-1, keepdims=True)
        for b in range(bs):
            one = slice(b, b + 1)
            items.append(dict(b=b, h=h, q=q[one], k=k[one], v=v[one], beta=beta[one], eg=eg[one], qk=qk[one]))
    sk_sqs = [_dot_nt(jnp.concatenate([it["k"], it["q"], pad], axis=0), st_ref[it["b"], it["h"]])
              for it in items]
    us = [it["beta"] * (it["v"] - it["eg"] * r[0:1]) for it, r in zip(items, sk_sqs)]
    for it, r, u in zip(items, sk_sqs, us):
        o_scr[it["b"]:it["b"] + 1, _head(it["h"])] = it["eg"] * r[1:2] + it["qk"] * u
    updates = [_dot(jnp.where(eye, jnp.broadcast_to(u, (HEAD_DIM, HEAD_DIM)), 0.0),
                    jnp.broadcast_to(it["k"], (HEAD_DIM, HEAD_DIM))) for it, u in zip(items, us)]
    for it, upd in zip(items, updates):
        sto_ref[it["b"], it["h"]] = it["eg"] * st_ref[it["b"], it["h"]] + upd
    o = o_scr[...]
    for h in range(N_HEADS):
        oa_ref[:, _head(h)] = (_rms(o[:, _head(h)], ow_ref[...]) * _silu(z_ref[:, _head(h)])).astype(BF16)

    xc = step_conv(lx_ref[...], lbuf_ref, lbo_ref, lcw_ref) + lcb_ref[...]
    a, bx = _lru_gates(xc, wr_ref, br_ref, wi_ref, bi_ref, lam_ref)
    hnew = a * lh_ref[...] + bx
    lho_ref[...] = hnew
    ob_ref[...] = (hnew * _silu(lg_ref[...])).astype(BF16)


def _sample_step(proj, layer, state_dn, state_dn_conv, state_lru_h, state_lru_conv,
                 dn_conv_w, al_row, dt_row, onorm_w, lru_conv_w, lru_conv_b, w_r, b_r, w_i, b_i, lam):
    nb = proj.shape[0]
    bs = SUBLANES
    qkv_w = 3 * BR_WIDTH
    wide = lambda off: pl.BlockSpec((bs, BR_WIDTH), lambda i: (i, off // BR_WIDTH))
    const = lambda shape: pl.BlockSpec(shape, lambda i: (0,) * len(shape))
    st_in = pl.BlockSpec((None, bs, N_HEADS, HEAD_DIM, HEAD_DIM), lambda i: (layer, i, 0, 0, 0))
    dbuf_in = pl.BlockSpec((None, bs, CONV_W - 1, qkv_w), lambda i: (layer, i, 0, 0))
    lh_in = pl.BlockSpec((None, bs, BR_WIDTH), lambda i: (layer, i, 0))
    lbuf_in = pl.BlockSpec((None, bs, CONV_W - 1, BR_WIDTH), lambda i: (layer, i, 0, 0))
    return pl.pallas_call(
        functools.partial(_sample_step_kernel, bs=bs),
        out_shape=(jax.ShapeDtypeStruct((nb, BR_WIDTH), BF16),
                   jax.ShapeDtypeStruct((nb, BR_WIDTH), BF16),
                   jax.ShapeDtypeStruct((nb, N_HEADS, HEAD_DIM, HEAD_DIM), F32),
                   jax.ShapeDtypeStruct((nb, CONV_W - 1, qkv_w), F32),
                   jax.ShapeDtypeStruct((nb, BR_WIDTH), F32),
                   jax.ShapeDtypeStruct((nb, CONV_W - 1, BR_WIDTH), F32)),
        grid=(nb // bs,),
        in_specs=[pl.BlockSpec((bs, qkv_w), lambda i: (i, 0)), wide(C_DNZ), wide(C_LRUX), wide(C_LRUG),
                  pl.BlockSpec((bs, LANES), lambda i: (i, C_SMALL // LANES)),
                  st_in, dbuf_in, lh_in, lbuf_in,
                  const((CONV_W, qkv_w)), const((1, LANES)), const((1, LANES)), const((1, HEAD_DIM)),
                  const((CONV_W, BR_WIDTH)), const((1, BR_WIDTH)),
                  const((BR_WIDTH, BR_WIDTH)), const((1, BR_WIDTH)),
                  const((BR_WIDTH, BR_WIDTH)), const((1, BR_WIDTH)), const((1, BR_WIDTH))],
        out_specs=(pl.BlockSpec((bs, BR_WIDTH), lambda i: (i, 0)),
                   pl.BlockSpec((bs, BR_WIDTH), lambda i: (i, 0)),
                   pl.BlockSpec((bs, N_HEADS, HEAD_DIM, HEAD_DIM), lambda i: (i, 0, 0, 0)),
                   pl.BlockSpec((bs, CONV_W - 1, qkv_w), lambda i: (i, 0, 0)),
                   pl.BlockSpec((bs, BR_WIDTH), lambda i: (i, 0)),
                   pl.BlockSpec((bs, CONV_W - 1, BR_WIDTH), lambda i: (i, 0, 0))),
        scratch_shapes=[pltpu.VMEM((bs, BR_WIDTH), F32)],
        compiler_params=_params("parallel"),
        name="sample_step",
    )(proj, proj, proj, proj, proj, state_dn, state_dn_conv, state_lru_h, state_lru_conv,
      dn_conv_w, al_row, dt_row, onorm_w, lru_conv_w, lru_conv_b, w_r, b_r, w_i, b_i, lam)


def _own_head_mask(n_rows_kv):
    shape = (SUBLANES, n_rows_kv)
    return (_iota2(shape, 1) % N_HEADS) == _iota2(shape, 0)


def _heads_to_row(o8):
    return jnp.concatenate([o8[h:h + 1] for h in range(N_HEADS)], axis=1)


def _pool_bias_kernel(lf_ref, after_ref, same_ref, bias_ref):
    in_page = None
    total = None
    for h in range(N_HEADS):
        for piece in _split3(lf_ref[:, h, :]):
            a = jnp.dot(piece, after_ref[h], preferred_element_type=F32)
            t = jnp.dot(piece, same_ref[h], preferred_element_type=F32)
            in_page = a if in_page is None else in_page + a
            total = t if total is None else total + t
    bias_ref[:, 0, :] = in_page
    bias_ref[:, 1, :] = total


def _pool_bias(cache_lf):
    n, nh, page = cache_lf.shape
    w = page * nh
    tp = min(n, 512)
    src_tok = jnp.arange(page)[None, :, None]
    src_head = jnp.arange(nh)[:, None, None]
    dst = jnp.arange(w)[None, None, :]
    same = (dst % nh) == src_head
    after = same & (src_tok > dst // nh)
    const = pl.BlockSpec((nh, page, w), lambda i: (0, 0, 0))
    return pl.pallas_call(
        _pool_bias_kernel,
        out_shape=jax.ShapeDtypeStruct((n, 2, w), F32),
        grid=(n // tp,),
        in_specs=[pl.BlockSpec((tp, nh, page), lambda i: (i, 0, 0)), const, const],
        out_specs=pl.BlockSpec((tp, 2, w), lambda i: (i, 0, 0)),
        compiler_params=_params("parallel"),
        name="pool_bias",
    )(cache_lf, after.astype(BF16), jnp.broadcast_to(same, after.shape).astype(BF16))


def _fox_decode_kernel(pt_ref, q_ref, kn_ref, vn_ref, lfn_ref, g_ref, *rest, n_pages, n_seq):
    n_blk = n_seq * n_pages
    k_refs, v_refs, bias_refs = rest[:n_blk], rest[n_blk:2 * n_blk], rest[2 * n_blk:3 * n_blk]
    o_ref = rest[3 * n_blk]
    del pt_ref
    own = _own_head_mask(k_refs[0].shape[0])
    for i in range(n_seq):
        pages = range(i * n_pages, (i + 1) * n_pages)
        q8 = q_ref[i]
        qb = q8.astype(BF16)

        later = lfn_ref[i]
        scores = {}
        for p in reversed(pages):
            s = _dot_nt(qb, k_refs[p][...]) + (bias_refs[p][0:1, :] + later)
            scores[p] = jnp.where(own, s, NEG_BIG)
            later = later + bias_refs[p][1:2, :]
        s_self = jnp.sum(q8 * kn_ref[i], axis=-1, keepdims=True)
        m = s_self
        for p in pages:
            m = jnp.maximum(m, jnp.max(scores[p], axis=-1, keepdims=True))
        p_self = jnp.exp(s_self - m)
        l = p_self
        acc = p_self * vn_ref[i]
        for p in pages:
            pr = jnp.exp(scores[p] - m)
            l = l + jnp.sum(pr, axis=-1, keepdims=True)
            acc = acc + _dot(pr, v_refs[p][...])
        o_ref[i] = (_heads_to_row(acc / l) * g_ref[i]).astype(BF16)


def _fox_decode(q8, k8, v8, lf_new, gate, cache_k, cache_v, page_bias, page_table, layer, n_pool):
    nb = q8.shape[0]
    n_pages = page_table.shape[0] // nb
    n_seq = DECODE_SEQS if nb % DECODE_SEQS == 0 else 1
    kv_rows = cache_k.shape[1]
    base = layer * n_pool
    row = pl.BlockSpec((n_seq, 1, BR_WIDTH), lambda b, pt: (b, 0, 0))
    heads = pl.BlockSpec((n_seq, SUBLANES, HEAD_DIM), lambda b, pt: (b, 0, 0))

    def paged(shape, i, p):
        return pl.BlockSpec((None,) + shape, lambda b, pt: (base + pt[(b * n_seq + i) * n_pages + p], 0, 0))

    blocks = [(i, p) for i in range(n_seq) for p in range(n_pages)]
    in_specs = [heads, heads, heads, pl.BlockSpec((n_seq, 1, kv_rows), lambda b, pt: (b, 0, 0)), row]
    in_specs += [paged((kv_rows, HEAD_DIM), i, p) for i, p in blocks]
    in_specs += [paged((kv_rows, HEAD_DIM), i, p) for i, p in blocks]
    in_specs += [paged((2, kv_rows), i, p) for i, p in blocks]
    grid_spec = pltpu.PrefetchScalarGridSpec(
        num_scalar_prefetch=1,
        grid=(nb // n_seq,),
        in_specs=in_specs,
        out_specs=row)
    n_blk = len(blocks)
    return pl.pallas_call(
        functools.partial(_fox_decode_kernel, n_pages=n_pages, n_seq=n_seq),
        out_shape=jax.ShapeDtypeStruct((nb, 1, BR_WIDTH), BF16),
        grid_spec=grid_spec,
        compiler_params=_params("parallel"),
        name="fox_decode",
    )(page_table, q8, k8, v8, lf_new, gate, *([cache_k] * n_blk), *([cache_v] * n_blk), *([page_bias] * n_blk))


def _mem_decode_kernel(q_ref, g_ref, k_ref, v_ref, qw_ref, o_ref, o_scr, *, bs):
    own = _own_head_mask(k_ref.shape[1])
    scale = HEAD_DIM ** -0.5
    qs = [_rms(q_ref[:, _head(h)], qw_ref[...]) * scale for h in range(N_HEADS)]
    pad = jnp.zeros((SUBLANES - N_HEADS, HEAD_DIM), F32)
    for b in range(bs):
        q8 = jnp.concatenate([q[b:b + 1] for q in qs] + [pad], axis=0)
        s = jnp.where(own, _dot_nt(q8, k_ref[b]), NEG_BIG)
        p = jnp.exp(s - jnp.max(s, axis=-1, keepdims=True))
        o = _dot(p, v_ref[b]) / jnp.sum(p, axis=-1, keepdims=True)
        o_scr[b:b + 1, :] = _heads_to_row(o)
    o_ref[...] = (o_scr[...] * _silu(g_ref[...])).astype(BF16)


def _mem_decode(proj, cache_k, cache_v, qn_w, layer):
    nb = proj.shape[0]
    bs = SUBLANES
    m = cache_k.shape[1]
    base = layer * (nb // bs)
    kv = pl.BlockSpec((bs, m, HEAD_DIM), lambda i: (base + i, 0, 0))
    return pl.pallas_call(
        functools.partial(_mem_decode_kernel, bs=bs),
        out_shape=jax.ShapeDtypeStruct((nb, BR_WIDTH), BF16),
        grid=(nb // bs,),
        in_specs=[pl.BlockSpec((bs, BR_WIDTH), lambda i: (i, C_MQ // BR_WIDTH)),
                  pl.BlockSpec((bs, BR_WIDTH), lambda i: (i, C_MG // BR_WIDTH)),
                  kv, kv, pl.BlockSpec((1, HEAD_DIM), lambda i: (0, 0))],
        out_specs=pl.BlockSpec((bs, BR_WIDTH), lambda i: (i, 0)),
        scratch_shapes=[pltpu.VMEM((bs, BR_WIDTH), F32)],
        compiler_params=_params("parallel"),
        name="mem_decode",
    )(proj, proj, cache_k, cache_v, qn_w)


def _permute_columns(a):
    runs = sorted(COLUMN_RUNS, key=lambda r: r[2])
    pad = jnp.zeros(a.shape[:-1] + (N_PERM - C_SMALL - N_SMALL,), a.dtype)
    return jnp.concatenate([a[..., start:stop] for start, stop, _ in runs] + [pad], axis=-1)


def _block_diag(w):
    nblk, e, f = w.shape
    eye = jnp.eye(nblk, dtype=w.dtype)
    return (eye[:, None, :, None] * w[:, :, None, :]).reshape(nblk * e, nblk * f)


def _decay_lane_row(v):
    return jnp.zeros((1, LANES), F32).at[0, SM_DECAY:SM_DECAY + N_HEADS].set(v)


def kernel(x_prompt, x_sample, cache_fox_k, cache_fox_v, cache_fox_logf, cache_mem_k, cache_mem_v, state_dn, state_dn_conv, state_lru_h, state_lru_conv, page_table, mem_prompt, norm_w, w_in, b_in, dn_conv_w, dn_A_log, dn_dt_bias, dn_onorm_w, lru_conv_w, lru_conv_b, lru_w_r, lru_b_r, lru_w_i, lru_b_i, lru_lambda, fox_qn_w, fox_kn_w, mem_norm_w, w_mem_kv, mem_qn_w, mem_kn_w, w_branch, w_out):
    bp, s, d = x_prompt.shape
    bd = x_sample.shape[0]
    depth = w_in.shape[0]
    n_pool, page = cache_fox_k.shape[1], cache_fox_k.shape[2]
    mem_tokens = mem_prompt.shape[1]
    tp = bp * s

    yp = x_prompt.reshape(tp, d)
    ys = x_sample.reshape(bd, d)
    mem2 = mem_prompt.reshape(bp * mem_tokens, d)
    kv_rows = page * N_HEADS
    cache_k2 = cache_fox_k.reshape(depth * n_pool, kv_rows, HEAD_DIM)
    cache_v2 = cache_fox_v.reshape(depth * n_pool, kv_rows, HEAD_DIM)
    page_bias = _pool_bias(jnp.swapaxes(cache_fox_logf, 2, 3).reshape(depth * n_pool, N_HEADS, page))
    cmem_k2 = cache_mem_k.reshape(depth * bd, mem_tokens * N_HEADS, HEAD_DIM)
    cmem_v2 = cache_mem_v.reshape(depth * bd, mem_tokens * N_HEADS, HEAD_DIM)
    pt_flat = page_table.reshape(-1)

    row = lambda v: v.reshape(1, -1)
    acc = {n: [] for n in ("pk", "pv", "plf", "pmk", "pmv", "pdn", "pdc", "plh", "plc",
                           "sk", "sv", "slf", "sdn", "sdc", "slh", "slc")}
    w_p, w_g = _repack_w_in(w_in)
    for l in range(depth):
        b_p = _permute_columns(b_in[l]).reshape(1, N_PERM)
        b_g = b_in[l, GATE_RUN[0]:GATE_RUN[1]].reshape(1, N_GATE)
        nw = row(norm_w[l])
        al_row = _decay_lane_row(dn_A_log[l])
        dt_row = _decay_lane_row(dn_dt_bias[l])
        ow = row(dn_onorm_w[l])
        wr = _block_diag(lru_w_r[l]).astype(BF16)
        wi = _block_diag(lru_w_i[l]).astype(BF16)
        lru_args = (lru_conv_w[l], row(lru_conv_b[l]), wr, row(lru_b_r[l]), wi, row(lru_b_i[l]), row(lru_lambda[l]))
        wb = w_branch[l].astype(BF16)
        wo = w_out[l].astype(BF16)
        fqw, fkw, mqw = row(fox_qn_w[l]), row(fox_kn_w[l]), row(mem_qn_w[l])

        hp = _norm(yp, nw)
        qb, kn, fv, kb, vb, gate_c, lf, c, ct = _fox_prep(hp, w_p, b_p, l, fqw, fkw, bp, s, BF16)
        out_c = _fox_flash(qb, kb, vb, c, ct, gate_c, bp, s)
        mk, mv = _mem_kv(mem2, row(mem_norm_w[l]), w_mem_kv[l].astype(BF16), row(mem_kn_w[l]))
        out_m = _mem_attn(hp, w_p, b_p, l, mk, mv, mqw, bp, s)
        out_a, dn_s, dn_tail = _dn_prompt(hp, w_p, b_p, l, dn_conv_w[l], al_row, dt_row, ow, bp, s)
        out_b, lru_h, lru_tail = _lru_prompt(hp, w_p, b_p, l, *lru_args, bp, s)
        acc["pk"].append(kn.reshape(bp, s, N_HEADS, HEAD_DIM))
        acc["pv"].append(fv.reshape(bp, s, N_HEADS, HEAD_DIM))
        acc["plf"].append(lf.reshape(bp, s, N_HEADS))
        acc["pmk"].append(mk.reshape(bp, mem_tokens, N_HEADS, HEAD_DIM))
        acc["pmv"].append(mv.reshape(bp, mem_tokens, N_HEADS, HEAD_DIM))
        acc["pdn"].append(dn_s)
        acc["pdc"].append(dn_tail[:, SUBLANES - (CONV_W - 1):, :])
        acc["plh"].append(lru_h.reshape(bp, BR_WIDTH))
        acc["plc"].append(lru_tail[:, SUBLANES - (CONV_W - 1):, :])
        yp = _merge(yp, hp, out_a, out_b, out_c, out_m, w_g, b_g, wb, wo, l)

        hs = _norm(ys, nw)
        proj_s = _inproj(hs, w_p, b_p, l)
        qn_s, kn_s, fv_s, _, _, gate_s, lf_s, _, _ = _fox_prep(hs, w_p, b_p, l, fqw, fkw, 1, bd, F32)
        head_rows = lambda a: jnp.pad(a.reshape(bd, N_HEADS, HEAD_DIM), ((0, 0), (0, SUBLANES - N_HEADS), (0, 0)))
        lf_new = jnp.tile(lf_s, (1, page)).reshape(bd, 1, kv_rows)
        out_c_s = _fox_decode(head_rows(qn_s), head_rows(kn_s), head_rows(fv_s), lf_new,
                              gate_s.reshape(bd, 1, BR_WIDTH),
                              cache_k2, cache_v2, page_bias, pt_flat, l, n_pool).reshape(bd, BR_WIDTH)
        out_m_s = _mem_decode(proj_s, cmem_k2, cmem_v2, mqw, l)
        out_a_s, out_b_s, dn_s_s, dn_c_s, lru_h_s, lru_c_s = _sample_step(
            proj_s, l, state_dn, state_dn_conv, state_lru_h, state_lru_conv,
            dn_conv_w[l], al_row, dt_row, ow, *lru_args)
        acc["sk"].append(kn_s.reshape(bd, 1, N_HEADS, HEAD_DIM))
        acc["sv"].append(fv_s.reshape(bd, 1, N_HEADS, HEAD_DIM))
        acc["slf"].append(lf_s.reshape(bd, 1, N_HEADS))
        acc["sdn"].append(dn_s_s)
        acc["sdc"].append(dn_c_s)
        acc["slh"].append(lru_h_s)
        acc["slc"].append(lru_c_s)
        ys = _merge(ys, hs, out_a_s, out_b_s, out_c_s, out_m_s, w_g, b_g, wb, wo, l)

    st = lambda n: jnp.stack(acc[n])
    return (yp.reshape(bp, s, d), ys.reshape(bd, 1, d),
            st("pk"), st("pv"), st("plf"), st("pmk"), st("pmv"), st("pdn"), st("pdc"), st("plh"), st("plc"),
            st("sk"), st("sv"), st("slf"), st("sdn"), st("sdc"), st("slh"), st("slc"))
```
